```python
import jax
import jax.numpy as jnp
from jax import lax
import numpy as np

D_MODEL = 1024
BATCH = 8
SEQ = 4096
DEPTH = 2

CTX_LEN = 256
GRID_W = 64
MIX_WIDTH = D_MODEL
A_WIDTH = MIX_WIDTH // 2
A_GROUPS = 4
A_GROUP_CH = A_WIDTH // A_GROUPS
A_CHUNK = 128
NA_HEADS = 8
NA_HEAD_DIM = (MIX_WIDTH - A_WIDTH) // NA_HEADS
NA_WIDTH = NA_HEADS * NA_HEAD_DIM
NA_WIN_ROWS = 8
NA_WIN_COLS = 16
O_NA_Q = 2 * A_WIDTH
O_NA_K = O_NA_Q + NA_WIDTH
O_NA_V = O_NA_K + NA_WIDTH
AB_IN_WIDTH = O_NA_V + NA_WIDTH
FN_WIDTH = MIX_WIDTH // 2
FN_GROUPS = 4
FN_GROUP_CH = FN_WIDTH // FN_GROUPS
GLA_HEADS = 4
GLA_VAL_WIDTH = MIX_WIDTH - FN_WIDTH
GLA_KEY_WIDTH = GLA_VAL_WIDTH // 2
GLA_DK = GLA_KEY_WIDTH // GLA_HEADS
GLA_DV = GLA_VAL_WIDTH // GLA_HEADS
GLA_LOW_RANK = 16
GLA_GATE_TEMP = 16.0
GLA_CHUNK = 64
O_Q = FN_WIDTH
O_K = O_Q + GLA_KEY_WIDTH
O_V = O_K + GLA_KEY_WIDTH
O_G = O_V + GLA_VAL_WIDTH
O_A = O_G + GLA_VAL_WIDTH
CD_IN_WIDTH = O_A + 2 * GLA_LOW_RANK
FFN_HIDDEN = -(-8 * D_MODEL // (3 * 256)) * 256
ROPE_BASE = 10000.0
RMS_EPS = 1e-6
NEG_INF = -1e30
N_EVEN = (DEPTH + 1) // 2
N_ODD = DEPTH // 2

kernel_name = "hybrid_dit_gmlp_natten_fnet_gla"


def rmsnorm(x, w):
    xf = x.astype(jnp.float32)
    y = xf * lax.rsqrt(jnp.mean(xf * xf, axis=-1, keepdims=True) + RMS_EPS)
    return (y * w.astype(jnp.float32)).astype(x.dtype)


def modulate(h, shift, scale):
    return h * (1.0 + scale) + shift


def swiglu(h, w_gate, w_up, w_down):
    return (jax.nn.silu(h @ w_gate) * (h @ w_up)) @ w_down


def _rope_1d(xh, pos):
    half = xh.shape[-1] // 2
    inv_freq = ROPE_BASE ** (-jnp.arange(half, dtype=jnp.float32) / half)
    ang = pos.astype(jnp.float32)[:, None] * inv_freq[None, :]
    cos = jnp.cos(ang)[None, :, None, :].astype(xh.dtype)
    sin = jnp.sin(ang)[None, :, None, :].astype(xh.dtype)
    x1, x2 = xh[..., :half], xh[..., half:]
    return jnp.concatenate([x1 * cos - x2 * sin, x2 * cos + x1 * sin], axis=-1)


def axial_rope(x):
    t = jnp.arange(x.shape[1])
    d2 = x.shape[-1] // 2
    return jnp.concatenate([_rope_1d(x[..., :d2], t // GRID_W), _rope_1d(x[..., d2:], t % GRID_W)], axis=-1)


def chunk_token_mlp(uv, norm_w, sgu_w, sgu_b):
    B, T, _ = uv.shape
    uv = jax.nn.gelu(uv)
    u, v = uv[..., :A_WIDTH], uv[..., A_WIDTH:]
    v = rmsnorm(v, norm_w)
    vg = v.reshape(B, T // A_CHUNK, A_CHUNK, A_GROUPS, A_GROUP_CH)
    gate = jnp.einsum('gpq,bnqgc->bnpgc', sgu_w, vg) + sgu_b.T[None, None, :, :, None]
    return u * gate.reshape(B, T, A_WIDTH)


def _na_heads(t):
    return t.reshape(t.shape[0], t.shape[1], NA_HEADS, NA_HEAD_DIM)


def neighbourhood_attention(q, k, v, k_ctx, v_ctx, rel_bias):
    B, T, H, dh = q.shape
    rows = T // GRID_W
    wr = min(NA_WIN_ROWS, rows)
    scale = dh ** -0.5
    qg = q.reshape(B, rows, GRID_W, H, dh)
    kg = k.reshape(B, rows, GRID_W, H, dh)
    vg = v.reshape(B, rows, GRID_W, H, dh)
    col = jnp.arange(GRID_W)
    col_start = jnp.clip(col - NA_WIN_COLS // 2, 0, GRID_W - NA_WIN_COLS)
    col_mask = (col[None, :] >= col_start[:, None]) & (col[None, :] < col_start[:, None] + NA_WIN_COLS)
    col_idx = jnp.clip(col[None, :] - col[:, None] + NA_WIN_COLS - 1, 0, 2 * NA_WIN_COLS - 2)

    def row_block(r):
        row_start = jnp.clip(r - wr // 2, 0, rows - wr)
        qr = lax.dynamic_index_in_dim(qg, r, axis=1, keepdims=False)
        kb = lax.dynamic_slice_in_dim(kg, row_start, wr, axis=1)
        vb = lax.dynamic_slice_in_dim(vg, row_start, wr, axis=1)
        row_idx = row_start + jnp.arange(wr) - r + NA_WIN_ROWS - 1
        bias = jnp.transpose(rel_bias[:, row_idx][:, :, col_idx], (0, 2, 1, 3))
        s_nb = jnp.einsum('bqhd,bwkhd->bhqwk', qr, kb).astype(jnp.float32) * scale + bias[None].astype(jnp.float32)
        s_nb = jnp.where(col_mask[:, None, :], s_nb, NEG_INF).reshape(B, H, GRID_W, wr * GRID_W)
        s_cx = jnp.einsum('bqhd,bchd->bhqc', qr, k_ctx).astype(jnp.float32) * scale
        p = jax.nn.softmax(jnp.concatenate([s_nb, s_cx], axis=-1), axis=-1).astype(v.dtype)
        p_nb = p[..., :wr * GRID_W].reshape(B, H, GRID_W, wr, GRID_W)
        p_cx = p[..., wr * GRID_W:]
        return jnp.einsum('bhqwk,bwkhd->bqhd', p_nb, vb) + jnp.einsum('bhqc,bchd->bqhd', p_cx, v_ctx)

    out = lax.map(row_block, jnp.arange(rows))
    return jnp.moveaxis(out, 0, 1).reshape(B, T, H * dh)


def context_attention(q, k, v):
    B, T, H, dh = q.shape
    s = jnp.einsum('bqhd,bkhd->bhqk', q, k).astype(jnp.float32) * dh ** -0.5
    p = jax.nn.softmax(s, axis=-1).astype(v.dtype)
    return jnp.einsum('bhqk,bkhd->bqhd', p, v).reshape(B, T, H * dh)


def ab_mixer(h, hc, w_in, w_out, sgu_norm_w, sgu_w, sgu_b, rel_bias, with_ctx_out):
    p = h @ w_in
    a_out = chunk_token_mlp(p[..., :O_NA_Q], sgu_norm_w, sgu_w, sgu_b)
    q_l = _na_heads(p[..., O_NA_Q:O_NA_K])
    k_l = _na_heads(p[..., O_NA_K:O_NA_V])
    v_l = _na_heads(p[..., O_NA_V:])
    if with_ctx_out:
        pc = hc @ w_in
        kv_c = pc[..., O_NA_K:]
    else:
        pc = None
        kv_c = hc @ w_in[:, O_NA_K:]
    k_c = _na_heads(kv_c[..., :NA_WIDTH])
    v_c = _na_heads(kv_c[..., NA_WIDTH:])
    b_out = neighbourhood_attention(q_l, k_l, v_l, k_c, v_c, rel_bias)
    y = jnp.concatenate([a_out, b_out], axis=-1) @ w_out
    if not with_ctx_out:
        return y, None
    a_c = chunk_token_mlp(pc[..., :O_NA_Q], sgu_norm_w, sgu_w, sgu_b)
    b_c = context_attention(_na_heads(pc[..., O_NA_Q:O_NA_K]), k_c, v_c)
    y_c = jnp.concatenate([a_c, b_c], axis=-1) @ w_out
    return y, y_c


def fourier_mix(f):
    B, T, _ = f.shape
    fg = f.reshape(B, T, FN_GROUPS, FN_GROUP_CH).astype(jnp.float32)
    out = jnp.fft.fft2(fg, axes=(1, 3), norm='ortho').real
    return out.reshape(B, T, FN_WIDTH).astype(f.dtype)


def _to_heads(t, n_heads):
    B, T, _ = t.shape
    return jnp.transpose(t.reshape(B, T, n_heads, -1), (0, 2, 1, 3))


def log_decay(a_lr, w, b):
    return jax.nn.log_sigmoid((a_lr @ w + b).astype(jnp.float32)) / GLA_GATE_TEMP


def _gla_chunk_terms(k, v, log_a):
    B, H, T, dk = k.shape
    n = T // GLA_CHUNK
    kc = k.reshape(B, H, n, GLA_CHUNK, dk)
    vc = v.reshape(B, H, n, GLA_CHUNK, -1)
    b = jnp.cumsum(log_a.reshape(B, H, n, GLA_CHUNK, dk), axis=3)
    b_last = b[:, :, :, -1:, :]
    kv = jnp.einsum('bhnld,bhnle->bhnde', kc * jnp.exp(b_last - b), vc)
    return kc, vc, b, kv, jnp.exp(b_last[:, :, :, 0, :])


def _gla_scan(kv, decay, s0):
    def step(s, inp):
        d, kv_n = inp
        return d[..., None] * s + kv_n, s
    s_fin, s_prev = lax.scan(step, s0, (jnp.moveaxis(decay, 2, 0), jnp.moveaxis(kv, 2, 0)))
    return jnp.moveaxis(s_prev, 0, 2), s_fin


def gla_final_state(k, v, log_a, s0):
    _, _, _, kv, decay = _gla_chunk_terms(k, v, log_a)
    _, s_fin = _gla_scan(kv, decay, s0)
    return s_fin


def gla_chunked(q, k, v, log_a, s0):
    B, H, T, dk = q.shape
    kc, vc, b, kv, decay = _gla_chunk_terms(k, v, log_a)
    s_prev, s_fin = _gla_scan(kv, decay, s0)
    qe = q.reshape(B, H, -1, GLA_CHUNK, dk) * jnp.exp(b)
    lower = jnp.tril(jnp.ones((GLA_CHUNK, GLA_CHUNK), jnp.float32))
    att = jnp.einsum('bhnld,bhnmd->bhnlm', qe, kc * jnp.exp(-b)) * lower
    o = jnp.einsum('bhnlm,bhnme->bhnle', att, vc) + jnp.einsum('bhnld,bhnde->bhnle', qe, s_prev)
    return o.reshape(B, H, T, -1), s_fin


def gla_output(o, g, head_norm_w):
    B, H, T, dv = o.shape
    o = rmsnorm(jnp.transpose(o, (0, 2, 1, 3)), head_norm_w)
    return (o * jax.nn.silu(g.reshape(B, T, H, dv).astype(jnp.float32))).reshape(B, T, H * dv).astype(g.dtype)


def _split_cd(p):
    return p[..., :O_Q], p[..., O_Q:O_K], p[..., O_K:O_V], p[..., O_V:O_G], p[..., O_G:O_A], p[..., O_A:]


def cd_mixer(h, hc, w_in, w_out, dec_w_f, dec_b_f, dec_w_b, dec_b_b, head_norm_w, with_ctx_out):
    B, T, _ = h.shape
    f32 = jnp.float32
    flip = lambda t: jnp.flip(t, axis=2)
    f_l, q_l, k_l, v_l, g_l, a_l = _split_cd(h @ w_in)
    q_l = axial_rope(q_l.reshape(B, T, GLA_HEADS, GLA_DK)) * GLA_DK ** -0.5
    k_l = axial_rope(k_l.reshape(B, T, GLA_HEADS, GLA_DK))
    ql = jnp.transpose(q_l, (0, 2, 1, 3)).astype(f32)
    kl = jnp.transpose(k_l, (0, 2, 1, 3)).astype(f32)
    vl = _to_heads(v_l, GLA_HEADS).astype(f32)
    la_lf = _to_heads(log_decay(a_l[..., :GLA_LOW_RANK], dec_w_f, dec_b_f), GLA_HEADS)
    la_lb = _to_heads(log_decay(a_l[..., GLA_LOW_RANK:], dec_w_b, dec_b_b), GLA_HEADS)
    if with_ctx_out:
        f_c, q_c, k_c, v_c, g_c, a_c = _split_cd(hc @ w_in)
    else:
        pc = hc @ jnp.concatenate([w_in[:, O_K:O_G], w_in[:, O_A:]], axis=1)
        k_c = pc[..., :GLA_KEY_WIDTH]
        v_c = pc[..., GLA_KEY_WIDTH:GLA_KEY_WIDTH + GLA_VAL_WIDTH]
        a_c = pc[..., GLA_KEY_WIDTH + GLA_VAL_WIDTH:]
    kc = _to_heads(k_c, GLA_HEADS).astype(f32)
    vc = _to_heads(v_c, GLA_HEADS).astype(f32)
    la_cf = _to_heads(log_decay(a_c[..., :GLA_LOW_RANK], dec_w_f, dec_b_f), GLA_HEADS)
    la_cb = _to_heads(log_decay(a_c[..., GLA_LOW_RANK:], dec_w_b, dec_b_b), GLA_HEADS)
    s0 = jnp.zeros((hc.shape[0], GLA_HEADS, GLA_DK, GLA_DV), f32)
    if with_ctx_out:
        qc = (_to_heads(q_c, GLA_HEADS) * GLA_DK ** -0.5).astype(f32)
        o_cf, s_f = gla_chunked(qc, kc, vc, la_cf, s0)
        o_cb, s_b = gla_chunked(flip(qc), flip(kc), flip(vc), flip(la_cb), s0)
        o_c = o_cf + flip(o_cb)
    else:
        s_f = gla_final_state(kc, vc, la_cf, s0)
        s_b = gla_final_state(flip(kc), flip(vc), flip(la_cb), s0)
    o_lf, _ = gla_chunked(ql, kl, vl, la_lf, s_f)
    o_lb, _ = gla_chunked(flip(ql), flip(kl), flip(vl), flip(la_lb), s_b)
    o_l = o_lf + flip(o_lb)
    y = jnp.concatenate([fourier_mix(f_l), gla_output(o_l, g_l, head_norm_w)], axis=-1) @ w_out
    if not with_ctx_out:
        return y, None
    y_c = jnp.concatenate([fourier_mix(f_c), gla_output(o_c, g_c, head_norm_w)], axis=-1) @ w_out
    return y, y_c


def setup_inputs(seed: int = 0) -> dict:
    key = jax.random.key(seed)
    ks = jax.random.split(key, 26)
    D = D_MODEL

    def nrm(k, shape, s):
        return jax.random.normal(k, shape, jnp.float32) * s

    return {
        'x': nrm(ks[0], (BATCH, SEQ, D), 1.0),
        'c': nrm(ks[1], (BATCH, D), 1.0),
        'ctx': nrm(ks[2], (BATCH, CTX_LEN, D), 1.0),
        'c_ctx': nrm(ks[3], (D,), 1.0),
        'ada_w': nrm(ks[4], (DEPTH, D, 6 * D), 0.5 * D ** -0.5),
        'ada_b': nrm(ks[5], (DEPTH, 6 * D), 0.02),
        'norm_mix_w': 1.0 + nrm(ks[6], (DEPTH, D), 0.05),
        'norm_ffn_w': 1.0 + nrm(ks[7], (DEPTH, D), 0.05),
        'ffn_w_gate': nrm(ks[8], (DEPTH, D, FFN_HIDDEN), D ** -0.5),
        'ffn_w_up': nrm(ks[9], (DEPTH, D, FFN_HIDDEN), D ** -0.5),
        'ffn_w_down': nrm(ks[10], (DEPTH, FFN_HIDDEN, D), FFN_HIDDEN ** -0.5),
        'ab_w_in': nrm(ks[11], (N_EVEN, D, AB_IN_WIDTH), D ** -0.5),
        'ab_w_out': nrm(ks[12], (N_EVEN, MIX_WIDTH, D), MIX_WIDTH ** -0.5),
        'ab_sgu_norm_w': 1.0 + nrm(ks[13], (N_EVEN, A_WIDTH), 0.05),
        'ab_sgu_w': nrm(ks[14], (N_EVEN, A_GROUPS, A_CHUNK, A_CHUNK), A_CHUNK ** -0.5),
        'ab_sgu_b': 1.0 + nrm(ks[15], (N_EVEN, A_GROUPS, A_CHUNK), 0.05),
        'ab_rel_bias': nrm(ks[16], (N_EVEN, NA_HEADS, 2 * NA_WIN_ROWS - 1, 2 * NA_WIN_COLS - 1), 0.5),
        'cd_w_in': nrm(ks[17], (N_ODD, D, CD_IN_WIDTH), D ** -0.5),
        'cd_w_out': nrm(ks[18], (N_ODD, MIX_WIDTH, D), MIX_WIDTH ** -0.5),
        'cd_decay_w_fwd': nrm(ks[19], (N_ODD, GLA_LOW_RANK, GLA_KEY_WIDTH), GLA_LOW_RANK ** -0.5),
        'cd_decay_b_fwd': nrm(ks[20], (N_ODD, GLA_KEY_WIDTH), 0.1),
        'cd_decay_w_bwd': nrm(ks[21], (N_ODD, GLA_LOW_RANK, GLA_KEY_WIDTH), GLA_LOW_RANK ** -0.5),
        'cd_decay_b_bwd': nrm(ks[22], (N_ODD, GLA_KEY_WIDTH), 0.1),
        'cd_head_norm_w': 1.0 + nrm(ks[23], (N_ODD, GLA_DV), 0.05),
        'final_norm_w': 1.0 + nrm(ks[24], (D,), 0.05),
    }


def reference(x, c, ctx, c_ctx, ada_w, ada_b, norm_mix_w, norm_ffn_w, ffn_w_gate, ffn_w_up, ffn_w_down,
              ab_w_in, ab_w_out, ab_sgu_norm_w, ab_sgu_w, ab_sgu_b, ab_rel_bias,
              cd_w_in, cd_w_out, cd_decay_w_fwd, cd_decay_b_fwd, cd_decay_w_bwd, cd_decay_b_bwd, cd_head_norm_w,
              final_norm_w):
    silu_c = jax.nn.silu(c)
    silu_cc = jax.nn.silu(c_ctx)
    for i in range(DEPTH):
        last = i == DEPTH - 1
        j = i // 2
        mod = (silu_c @ ada_w[i] + ada_b[i])[:, None, :]
        mod_c = (silu_cc @ ada_w[i] + ada_b[i])[None, None, :]
        sh_m, sc_m, g_m, sh_f, sc_f, g_f = jnp.split(mod, 6, axis=-1)
        shc_m, scc_m, gc_m, shc_f, scc_f, gc_f = jnp.split(mod_c, 6, axis=-1)
        h = modulate(rmsnorm(x, norm_mix_w[i]), sh_m, sc_m)
        hc = modulate(rmsnorm(ctx, norm_mix_w[i]), shc_m, scc_m)
        if i % 2 == 0:
            y, y_c = ab_mixer(h, hc, ab_w_in[j], ab_w_out[j], ab_sgu_norm_w[j], ab_sgu_w[j], ab_sgu_b[j],
                              ab_rel_bias[j], not last)
        else:
            y, y_c = cd_mixer(h, hc, cd_w_in[j], cd_w_out[j], cd_decay_w_fwd[j], cd_decay_b_fwd[j],
                              cd_decay_w_bwd[j], cd_decay_b_bwd[j], cd_head_norm_w[j], not last)
        x = x + g_m * y
        x = x + g_f * swiglu(modulate(rmsnorm(x, norm_ffn_w[i]), sh_f, sc_f), ffn_w_gate[i], ffn_w_up[i], ffn_w_down[i])
        if not last:
            ctx = ctx + gc_m * y_c
            ctx = ctx + gc_f * swiglu(modulate(rmsnorm(ctx, norm_ffn_w[i]), shc_f, scc_f),
                                      ffn_w_gate[i], ffn_w_up[i], ffn_w_down[i])
    return rmsnorm(x, final_norm_w)
```

```python
import functools

import numpy as np
import jax
import jax.numpy as jnp
from jax import lax
from jax.experimental import pallas as pl
from jax.experimental.pallas import tpu as pltpu

F32 = jnp.float32
BF16 = jnp.bfloat16

D_MODEL = 1024
CTX_LEN = 256
GRID_W = 64
A_WIDTH = 512
A_GROUPS = 4
A_CHUNK = 128
NA_HEADS = 8
NA_HEAD_DIM = 64
NA_WIDTH = 512
NA_WIN_ROWS = 8
NA_WIN_COLS = 16
O_NA_Q = 2 * A_WIDTH
FN_WIDTH = 512
FN_GROUP_CH = 128
GLA_HEADS = 4
GLA_VAL_WIDTH = 512
GLA_KEY_WIDTH = 256
GLA_DK = 64
GLA_DV = 128
GLA_LOW_RANK = 16
GLA_GATE_TEMP = 16.0
GLA_CHUNK = 64
O_Q = FN_WIDTH
O_V = O_Q + 2 * GLA_KEY_WIDTH
O_G = O_V + GLA_VAL_WIDTH
O_A = O_G + GLA_VAL_WIDTH
FFN_HIDDEN = 2816
ROPE_BASE = 10000.0
RMS_EPS = 1e-6
NEG_INF = -1e30

LANES = 128
VMEM_LIMIT_BYTES = 56 * 2 ** 20

ROW_TILE = 512
FFN_SPLIT = 2
GLA_TILE = 512


def _cparams(*sem):
    return pltpu.CompilerParams(dimension_semantics=sem, vmem_limit_bytes=VMEM_LIMIT_BYTES)


def _resident(shape):
    nd = len(shape)
    return pl.BlockSpec(shape, lambda *_: (0,) * nd, pipeline_mode=pl.Buffered(1))


def _dot(a, b):
    return jnp.dot(a, b, preferred_element_type=F32)


def _dot_nt(a, b):
    return lax.dot_general(a, b, (((1,), (1,)), ((), ())), preferred_element_type=F32)


def _dot_tn(a, b):
    return lax.dot_general(a, b, (((0,), (0,)), ((), ())), preferred_element_type=F32)


def _rms(x, w):
    return x * lax.rsqrt(jnp.mean(x * x, axis=-1, keepdims=True) + RMS_EPS) * w


def _rms_mod(x, w, shift, scale):
    return _rms(x, w) * (1.0 + scale) + shift


def _ada_kernel(c_ref, w_ref, b_ref, o_ref):
    c = c_ref[...]
    s = c * jax.nn.sigmoid(c)
    o_ref[0] = jnp.dot(s, w_ref[0], preferred_element_type=F32, precision=lax.Precision.HIGHEST) + b_ref[0]


def _ada(cc, ada_w, ada_b):
    depth, d, n = ada_w.shape
    r = cc.shape[0]
    tn = 1536
    return pl.pallas_call(
        _ada_kernel,
        out_shape=jax.ShapeDtypeStruct((depth, r, n), F32),
        grid=(depth, n // tn),
        in_specs=[pl.BlockSpec((r, d), lambda i, j: (0, 0)),
                  pl.BlockSpec((1, d, tn), lambda i, j: (i, 0, j)),
                  pl.BlockSpec((1, 1, tn), lambda i, j: (i, 0, j))],
        out_specs=pl.BlockSpec((1, r, tn), lambda i, j: (i, 0, j)),
        compiler_params=_cparams("arbitrary", "arbitrary"),
        name="ada_mod",
    )(cc, ada_w, ada_b.reshape(depth, 1, n))


def _mod_spec(tiles_per_batch):
    if tiles_per_batch is None:
        return pl.BlockSpec((1, 6, D_MODEL), lambda i: (0, 0, 0))
    return pl.BlockSpec((1, 6, D_MODEL), lambda i: (i // tiles_per_batch, 0, 0))


def _ab_in_kernel(x_ref, mod_ref, nw_ref, w_ref, snw_ref, sguw_ref, sgub_ref,
                  a_ref, q_ref, k_ref, v_ref, *, tm):
    h = _rms_mod(x_ref[...], nw_ref[...], mod_ref[0, 0:1, :], mod_ref[0, 1:2, :]).astype(BF16)
    uv = jax.nn.gelu(_dot(h, w_ref[:, :O_NA_Q]))
    u = uv[:, :A_WIDTH]
    v = _rms(uv[:, A_WIDTH:], snw_ref[...]).astype(BF16)
    for ci in range(tm // A_CHUNK):
        rs = slice(ci * A_CHUNK, (ci + 1) * A_CHUNK)
        for g in range(A_GROUPS):
            cs = slice(g * LANES, (g + 1) * LANES)
            gate = _dot(sguw_ref[g], v[rs, cs]) + sgub_ref[g]
            a_ref[rs, cs] = (u[rs, cs] * gate).astype(BF16)
    qkv = _dot(h, w_ref[:, O_NA_Q:])
    q_ref[...] = (qkv[:, :NA_WIDTH] * NA_HEAD_DIM ** -0.5).astype(BF16)
    k_ref[...] = qkv[:, NA_WIDTH:2 * NA_WIDTH].astype(BF16)
    v_ref[...] = qkv[:, 2 * NA_WIDTH:].astype(BF16)


def _ab_in(x2, mod, tiles_per_batch, tm, nw, w_bf, snw, sguw_bf, sgub_exp):
    m = x2.shape[0]
    n_in = w_bf.shape[1]
    row = lambda w: pl.BlockSpec((tm, w), lambda i: (i, 0))
    out = jax.ShapeDtypeStruct((m, A_WIDTH), BF16)
    return pl.pallas_call(
        functools.partial(_ab_in_kernel, tm=tm),
        out_shape=(out, out, out, out),
        grid=(m // tm,),
        in_specs=[row(D_MODEL), _mod_spec(tiles_per_batch), _resident((1, D_MODEL)),
                  _resident((D_MODEL, n_in)), _resident((1, A_WIDTH)),
                  _resident((A_GROUPS, A_CHUNK, A_CHUNK)), _resident((A_GROUPS, A_CHUNK, LANES))],
        out_specs=(row(A_WIDTH),) * 4,
        compiler_params=_cparams("parallel"),
        name="ab_in",
    )(x2, mod, nw, w_bf, snw, sguw_bf, sgub_exp)


def _na_bias_table(rel_bias):
    col = np.arange(GRID_W)
    col_start = np.clip(col - NA_WIN_COLS // 2, 0, GRID_W - NA_WIN_COLS)
    col_mask = (col[None, :] >= col_start[:, None]) & (col[None, :] < col_start[:, None] + NA_WIN_COLS)
    col_idx = np.clip(col[None, :] - col[:, None] + NA_WIN_COLS - 1, 0, 2 * NA_WIN_COLS - 2)
    row_idx = np.arange(NA_WIN_ROWS)[None, :] - np.arange(NA_WIN_ROWS)[:, None] + NA_WIN_ROWS - 1
    t = rel_bias[:, row_idx][:, :, :, col_idx]
    t = jnp.where(col_mask[None, None, None], t, NEG_INF)
    t = jnp.transpose(t, (0, 1, 3, 2, 4)).reshape(NA_HEADS, NA_WIN_ROWS, GRID_W, NA_WIN_ROWS * GRID_W)
    t = t.reshape(NA_HEADS // 2, 2, NA_WIN_ROWS, GRID_W, NA_WIN_ROWS * GRID_W)
    return jnp.transpose(t, (0, 2, 1, 3, 4)).reshape(NA_HEADS // 2, NA_WIN_ROWS, 2 * GRID_W, NA_WIN_ROWS * GRID_W)


def _stack_heads(qb):
    first = lax.broadcasted_iota(jnp.int32, (1, LANES), 1) < NA_HEAD_DIM
    zero = jnp.zeros_like(qb)
    return jnp.concatenate([jnp.where(first, qb, zero), jnp.where(first, zero, qb)], axis=0)


def _unstack_heads(r):
    n = r.shape[0] // 2
    first = lax.broadcasted_iota(jnp.int32, (1, LANES), 1) < NA_HEAD_DIM
    return jnp.where(first, r[:n], r[n:])


def _na_kernel(q_ref, k_ref, v_ref, kc_ref, vc_ref, tab_ref, o_ref, *, rows):
    kc = kc_ref[0]
    vc = vc_ref[0]
    win = NA_WIN_ROWS * GRID_W

    def body(r, carry):
        start = jnp.clip(r - NA_WIN_ROWS // 2, 0, rows - NA_WIN_ROWS)
        q0 = pl.multiple_of(r * GRID_W, GRID_W)
        k0 = pl.multiple_of(start * GRID_W, GRID_W)
        qs = _stack_heads(q_ref[0, pl.ds(q0, GRID_W), :])
        kb = k_ref[0, pl.ds(k0, win), :]
        vb = v_ref[0, pl.ds(k0, win), :]
        s_nb = _dot_nt(qs, kb) + tab_ref[0, r - start]
        s_cx = _dot_nt(qs, kc)
        m = jnp.maximum(jnp.max(s_nb, axis=-1, keepdims=True), jnp.max(s_cx, axis=-1, keepdims=True))
        e_nb = jnp.exp(s_nb - m)
        e_cx = jnp.exp(s_cx - m)
        l = jnp.sum(e_nb, axis=-1, keepdims=True) + jnp.sum(e_cx, axis=-1, keepdims=True)
        acc = _dot(e_nb.astype(BF16), vb) + _dot(e_cx.astype(BF16), vc)
        o_ref[0, pl.ds(q0, GRID_W), :] = _unstack_heads(acc / l).astype(BF16)
        return carry

    lax.fori_loop(0, rows, body, 0)


def _na(q, k, v, kc, vc, table):
    b, t, _ = q.shape
    rows = t // GRID_W
    lat = pl.BlockSpec((1, t, LANES), lambda i, j: (i, 0, j))
    ctx = pl.BlockSpec((1, CTX_LEN, LANES), lambda i, j: (i, 0, j))
    return pl.pallas_call(
        functools.partial(_na_kernel, rows=rows),
        out_shape=jax.ShapeDtypeStruct((b, t, NA_WIDTH), BF16),
        grid=(b, NA_HEADS // 2),
        in_specs=[lat, lat, lat, ctx, ctx,
                  pl.BlockSpec((1, NA_WIN_ROWS, 2 * GRID_W, NA_WIN_ROWS * GRID_W), lambda i, j: (j, 0, 0, 0))],
        out_specs=lat,
        compiler_params=_cparams("parallel", "parallel"),
        name="na_attn",
    )(q, k, v, kc, vc, table)


def _ctx_attn_kernel(q_ref, k_ref, v_ref, o_ref):
    s = _dot_nt(_stack_heads(q_ref[0]), k_ref[0])
    e = jnp.exp(s - jnp.max(s, axis=-1, keepdims=True))
    acc = _dot(e.astype(BF16), v_ref[0])
    o_ref[0] = _unstack_heads(acc / jnp.sum(e, axis=-1, keepdims=True)).astype(BF16)


def _ctx_attn(q, k, v):
    b = q.shape[0]
    spec = pl.BlockSpec((1, CTX_LEN, LANES), lambda i, j: (i, 0, j))
    return pl.pallas_call(
        _ctx_attn_kernel,
        out_shape=jax.ShapeDtypeStruct((b, CTX_LEN, NA_WIDTH), BF16),
        grid=(b, NA_HEADS // 2),
        in_specs=[spec, spec, spec],
        out_specs=spec,
        compiler_params=_cparams("parallel", "parallel"),
        name="ctx_attn",
    )(q, k, v)


def _out_ffn_kernel(a_ref, b_ref, x_ref, mod_ref, wo_ref, nw_ref, wg_ref, wu_ref, wd_ref, fnw_ref, o_ref,
                    *, final):
    half = wo_ref.shape[0] // 2
    y = _dot(a_ref[...], wo_ref[:half, :]) + _dot(b_ref[...], wo_ref[half:, :])
    x1 = x_ref[...] + mod_ref[0, 2:3, :] * y
    h = _rms_mod(x1, nw_ref[...], mod_ref[0, 3:4, :], mod_ref[0, 4:5, :]).astype(BF16)
    piece = FFN_HIDDEN // FFN_SPLIT
    acc = None
    for j in range(FFN_SPLIT):
        cs = slice(j * piece, (j + 1) * piece)
        g = _dot(h, wg_ref[:, cs])
        act = (g * jax.nn.sigmoid(g) * _dot(h, wu_ref[:, cs])).astype(BF16)
        part = _dot(act, wd_ref[cs, :])
        acc = part if acc is None else acc + part
    x2 = x1 + mod_ref[0, 5:6, :] * acc
    o_ref[...] = _rms(x2, fnw_ref[...]) if final else x2


def _out_ffn(a, b, x2, mod, tiles_per_batch, tm, wo_bf, nw, wg_bf, wu_bf, wd_bf, fnw, final):
    m = x2.shape[0]
    row = lambda w: pl.BlockSpec((tm, w), lambda i: (i, 0))
    return pl.pallas_call(
        functools.partial(_out_ffn_kernel, final=final),
        out_shape=jax.ShapeDtypeStruct((m, D_MODEL), F32),
        grid=(m // tm,),
        in_specs=[row(a.shape[1]), row(b.shape[1]), row(D_MODEL), _mod_spec(tiles_per_batch),
                  _resident(wo_bf.shape), _resident((1, D_MODEL)), _resident(wg_bf.shape),
                  _resident(wu_bf.shape), _resident(wd_bf.shape), _resident((1, D_MODEL))],
        out_specs=row(D_MODEL),
        compiler_params=_cparams("parallel"),
        name="out_ffn_final" if final else "out_ffn",
    )(a, b, x2, mod, wo_bf, nw, wg_bf, wu_bf, wd_bf, fnw)


def _rope_tables(t):
    half = GLA_DK // 4
    inv_freq = ROPE_BASE ** (-np.arange(half, dtype=np.float64) / half)
    pos = np.arange(t)
    ang_r = (pos // GRID_W)[:, None] * inv_freq[None, :]
    ang_c = (pos % GRID_W)[:, None] * inv_freq[None, :]
    cos = np.concatenate([np.cos(ang_r)] * 2 + [np.cos(ang_c)] * 2, axis=1)
    sin = np.concatenate([-np.sin(ang_r), np.sin(ang_r), -np.sin(ang_c), np.sin(ang_c)], axis=1)
    return (jnp.asarray(np.tile(cos, (1, 2)), F32), jnp.asarray(np.tile(sin, (1, 2)), F32))


def _log_sigmoid(z):
    return jnp.minimum(z, 0.0) - jnp.log1p(jnp.exp(-jnp.abs(z)))


def _cd_in_kernel(x_ref, mod_ref, nw_ref, w_ref, wa_ref, wdec_ref, bdec_ref, cos_ref, sin_ref,
                  f_ref, qk_ref, v_ref, g_ref, la_ref, *, rope):
    h = _rms_mod(x_ref[...], nw_ref[...], mod_ref[0, 0:1, :], mod_ref[0, 1:2, :]).astype(BF16)
    f_ref[...] = _dot(h, w_ref[:, :O_Q])
    qk = _dot(h, w_ref[:, O_Q:O_V])
    if rope:
        reps = qk.shape[1] // LANES
        cos = jnp.concatenate([cos_ref[...]] * reps, axis=1)
        sin = jnp.concatenate([sin_ref[...]] * reps, axis=1)
        w = qk.shape[1]
        q16 = GLA_DK // 4
        first = (lax.broadcasted_iota(jnp.int32, (1, w), 1) % (2 * q16)) < q16
        partner = jnp.where(first, pltpu.roll(qk, w - q16, 1), pltpu.roll(qk, q16, 1))
        qk = qk * cos + partner * sin
    is_q = lax.broadcasted_iota(jnp.int32, (1, qk.shape[1]), 1) < GLA_KEY_WIDTH
    qk_ref[...] = jnp.where(is_q, qk * GLA_DK ** -0.5, qk)
    v_ref[...] = _dot(h, w_ref[:, O_V:O_G]).astype(BF16)
    g_ref[...] = _dot(h, w_ref[:, O_G:O_A])
    a = _dot(h, wa_ref[...]).astype(BF16)
    la_ref[...] = _log_sigmoid(_dot(a, wdec_ref[...]) + bdec_ref[...]) * (1.0 / GLA_GATE_TEMP)


def _cd_in(x2, mod, tiles_per_batch, tm, nw, w_bf, wa_bf, wdec_bf, bdec, cos, sin, rope):
    m = x2.shape[0]
    row = lambda w: pl.BlockSpec((tm, w), lambda i: (i, 0))
    if rope:
        tab = pl.BlockSpec((tm, LANES), lambda i: (i % tiles_per_batch, 0))
    else:
        tab = pl.BlockSpec((tm, LANES), lambda i: (0, 0))
    w512 = 2 * GLA_KEY_WIDTH
    outs = (jax.ShapeDtypeStruct((m, FN_WIDTH), F32), jax.ShapeDtypeStruct((m, w512), F32),
            jax.ShapeDtypeStruct((m, GLA_VAL_WIDTH), BF16), jax.ShapeDtypeStruct((m, GLA_VAL_WIDTH), F32),
            jax.ShapeDtypeStruct((m, w512), F32))
    return pl.pallas_call(
        functools.partial(_cd_in_kernel, rope=rope),
        out_shape=outs,
        grid=(m // tm,),
        in_specs=[row(D_MODEL), _mod_spec(tiles_per_batch), _resident((1, D_MODEL)),
                  _resident(w_bf.shape), _resident(wa_bf.shape), _resident(wdec_bf.shape),
                  _resident((1, w512)), tab, tab],
        out_specs=(row(FN_WIDTH), row(w512), row(GLA_VAL_WIDTH), row(GLA_VAL_WIDTH), row(w512)),
        compiler_params=_cparams("parallel"),
        name="cd_in_rope" if rope else "cd_in_ctx",
    )(x2, mod, nw, w_bf, wa_bf, wdec_bf, bdec, cos, sin)


def _fft_tables(t):
    n1 = GRID_W
    assert t == n1 * n1
    a = np.arange(n1)
    ang1 = 2 * np.pi * np.outer(a, a) / n1
    cs = np.concatenate([np.cos(ang1), -np.sin(ang1)], axis=0)
    kap = a[:, None, None] + n1 * a[None, :, None]
    ang2 = 2 * np.pi * kap * a[None, None, :] / t
    gr, gi = np.cos(ang2), -np.sin(ang2)
    g = np.concatenate([np.concatenate([gr, -gi], axis=2), np.concatenate([gi, gr], axis=2)], axis=1)
    c = np.arange(FN_GROUP_CH)
    ang3 = 2 * np.pi * np.outer(c, c) / FN_GROUP_CH
    norm = 1.0 / np.sqrt(t * FN_GROUP_CH)
    return (jnp.asarray(cs, F32), jnp.asarray(g, F32),
            jnp.asarray(np.cos(ang3) * norm, F32), jnp.asarray(np.sin(ang3) * norm, F32))


def _fft_kernel(x_ref, cs_ref, g_ref, cc_ref, sc_ref, o_ref, p_scr, zr_scr, zi_scr):
    n1 = GRID_W
    cs = cs_ref[...].astype(BF16)

    def stage1(b, carry):
        xb = x_ref[0, pl.ds(b, n1, stride=n1), :].astype(BF16)
        p_scr[pl.ds(pl.multiple_of(b * 2 * n1, 2 * n1), 2 * n1), :] = _dot(cs, xb)
        return carry

    lax.fori_loop(0, n1, stage1, 0)

    def stage2(k1, carry):
        pr = p_scr[pl.ds(k1, n1, stride=2 * n1), :]
        pi = p_scr[pl.ds(n1 + k1, n1, stride=2 * n1), :]
        z = _dot(g_ref[k1].astype(BF16), jnp.concatenate([pr, pi], axis=0).astype(BF16))
        zr_scr[pl.ds(k1, n1, stride=n1), :] = z[:n1]
        zi_scr[pl.ds(k1, n1, stride=n1), :] = z[n1:]
        return carry

    lax.fori_loop(0, n1, stage2, 0)

    cc = cc_ref[...].astype(BF16)
    sc = sc_ref[...].astype(BF16)
    for gi in range(o_ref.shape[2] // FN_GROUP_CH):
        cols = slice(gi * FN_GROUP_CH, (gi + 1) * FN_GROUP_CH)
        out = _dot(zr_scr[:, cols].astype(BF16), cc) + _dot(zi_scr[:, cols].astype(BF16), sc)
        o_ref[0, :, cols] = out.astype(BF16)


def _fourier(f, tables):
    b, t, w = f.shape
    cw = FN_GROUP_CH
    cs, g, cc, sc = tables
    blk = pl.BlockSpec((1, t, cw), lambda i, j: (i, 0, j))
    return pl.pallas_call(
        _fft_kernel,
        out_shape=jax.ShapeDtypeStruct((b, t, w), BF16),
        grid=(b, w // cw),
        in_specs=[blk, _resident(cs.shape), _resident(g.shape), _resident(cc.shape), _resident(sc.shape)],
        out_specs=blk,
        scratch_shapes=[pltpu.VMEM((2 * t, cw), F32), pltpu.VMEM((t, cw), F32), pltpu.VMEM((t, cw), F32)],
        compiler_params=_cparams("parallel", "parallel"),
        name="fourier_mix",
    )(f, cs, g, cc, sc)


def _gla_masks():
    c = GLA_CHUNK
    r = lax.broadcasted_iota(jnp.int32, (GLA_HEADS * c, c), 0) % c
    m = lax.broadcasted_iota(jnp.int32, (GLA_HEADS * c, c), 1)
    lower, upper = r >= m, r <= m
    sr = lax.broadcasted_iota(jnp.int32, (GLA_VAL_WIDTH, GLA_KEY_WIDTH), 0) // GLA_DV
    sc = lax.broadcasted_iota(jnp.int32, (GLA_VAL_WIDTH, GLA_KEY_WIDTH), 1) // GLA_DK
    tl = lax.broadcasted_iota(jnp.int32, (c, c), 0)
    tm = lax.broadcasted_iota(jnp.int32, (c, c), 1)
    return lower, upper, sr == sc, (tl >= tm).astype(F32), (tl <= tm).astype(F32)


def _gla_chunk(q, k, v, la, state, tri, att_mask, state_mask, last_row, want_out):
    c = GLA_CHUNK
    b = jnp.dot(tri, la, preferred_element_type=F32, precision=lax.Precision.HIGHEST)
    b_last = b[last_row:last_row + 1, :]
    kd = (k * jnp.exp(b_last - b)).astype(BF16)
    kv_t = _dot_tn(v, kd)
    new_state = state * jnp.exp(b_last) + jnp.where(state_mask, kv_t, 0.0)
    if not want_out:
        return None, new_state
    qe = q * jnp.exp(b)
    ke = (k * jnp.exp(-b)).astype(BF16)
    lane_head = lax.broadcasted_iota(jnp.int32, (1, GLA_KEY_WIDTH), 1) // GLA_DK
    qs = jnp.concatenate([jnp.where(lane_head == h, qe, 0.0) for h in range(GLA_HEADS)], axis=0).astype(BF16)
    att = jnp.where(att_mask, _dot_nt(qs, ke), 0.0).astype(BF16)
    r = _dot(att, v)
    intra = jnp.concatenate([r[h * c:(h + 1) * c, h * GLA_DV:(h + 1) * GLA_DV] for h in range(GLA_HEADS)], axis=1)
    inter = _dot_nt(qe.astype(BF16), state.astype(BF16))
    return intra + inter, new_state


def _gla_kernel(qkf_ref, vf_ref, laf_ref, gf_ref, qkb_ref, vb_ref, lab_ref, gb_ref,
                kc_ref, vc_ref, lac_ref, hw_ref, o_ref, sf_scr, sb_scr, acc_scr, *, n_tiles, tile):
    c = GLA_CHUNK
    i = pl.program_id(1)
    lower, upper, state_mask, tri_lo, tri_up = _gla_masks()
    kw = GLA_KEY_WIDTH

    @pl.when(i == 0)
    def _():
        n_ctx = kc_ref.shape[1] // c
        sf = jnp.zeros((GLA_VAL_WIDTH, kw), F32)
        sb = jnp.zeros((GLA_VAL_WIDTH, kw), F32)
        for n in range(n_ctx):
            rf = slice(n * c, (n + 1) * c)
            rb = slice((n_ctx - 1 - n) * c, (n_ctx - n) * c)
            _, sf = _gla_chunk(None, kc_ref[0, rf, :], vc_ref[0, rf, :], lac_ref[0, rf, :kw], sf,
                               tri_lo, lower, state_mask, c - 1, False)
            _, sb = _gla_chunk(None, kc_ref[0, rb, :], vc_ref[0, rb, :], lac_ref[0, rb, kw:], sb,
                               tri_up, upper, state_mask, 0, False)
        sf_scr[...] = sf
        sb_scr[...] = sb

    first_half = i < n_tiles // 2
    hw = hw_ref[...]

    def finish(o, g):
        parts = []
        for h in range(GLA_HEADS):
            oh = o[:, h * GLA_DV:(h + 1) * GLA_DV]
            parts.append(oh * lax.rsqrt(jnp.mean(oh * oh, axis=-1, keepdims=True) + RMS_EPS))
        return (jnp.concatenate(parts, axis=1) * hw * (g * jax.nn.sigmoid(g))).astype(BF16)

    def emit(o, g, row0):
        rows = pl.ds(pl.multiple_of(row0, c), c)

        @pl.when(first_half)
        def _():
            acc_scr[rows, :] = o

        @pl.when(jnp.logical_not(first_half))
        def _():
            o_ref[0, rows, :] = finish(o + acc_scr[rows, :], g)

    def body(n, carry):
        rf = pl.ds(pl.multiple_of(n * c, c), c)
        qk = qkf_ref[0, rf, :]
        o, s = _gla_chunk(qk[:, :kw], qk[:, kw:], vf_ref[0, rf, :], laf_ref[0, rf, :], sf_scr[...],
                          tri_lo, lower, state_mask, c - 1, True)
        sf_scr[...] = s
        emit(o, gf_ref[0, rf, :], i * tile + n * c)
        nb = tile // c - 1 - n
        rb = pl.ds(pl.multiple_of(nb * c, c), c)
        qk = qkb_ref[0, rb, :]
        o, s = _gla_chunk(qk[:, :kw], qk[:, kw:], vb_ref[0, rb, :], lab_ref[0, rb, :], sb_scr[...],
                          tri_up, upper, state_mask, 0, True)
        sb_scr[...] = s
        emit(o, gb_ref[0, rb, :], (n_tiles - 1 - i) * tile + nb * c)
        return carry

    lax.fori_loop(0, tile // c, body, 0)


def _gla(qk, v, la, g, qk_c, v_c, la_c, head_w):
    b, t, _ = qk.shape
    tile = GLA_TILE
    n_tiles = t // tile
    kw, vw = GLA_KEY_WIDTH, GLA_VAL_WIDTH
    fwd = lambda w, col: pl.BlockSpec((1, tile, w), lambda bi, i: (bi, i, col))
    bwd = lambda w, col: pl.BlockSpec((1, tile, w), lambda bi, i: (bi, n_tiles - 1 - i, col))
    ctx = lambda w: pl.BlockSpec((1, CTX_LEN, w), lambda bi, i: (bi, 0, 0))
    return pl.pallas_call(
        functools.partial(_gla_kernel, n_tiles=n_tiles, tile=tile),
        out_shape=jax.ShapeDtypeStruct((b, t, vw), BF16),
        grid=(b, n_tiles),
        in_specs=[fwd(2 * kw, 0), fwd(vw, 0), fwd(kw, 0), fwd(vw, 0),
                  bwd(2 * kw, 0), bwd(vw, 0), bwd(kw, 1), bwd(vw, 0),
                  pl.BlockSpec((1, CTX_LEN, kw), lambda bi, i: (bi, 0, 1)), ctx(vw), ctx(2 * kw),
                  pl.BlockSpec((1, vw), lambda bi, i: (0, 0))],
        out_specs=pl.BlockSpec((1, t, vw), lambda bi, i: (bi, 0, 0)),
        scratch_shapes=[pltpu.VMEM((vw, kw), F32), pltpu.VMEM((vw, kw), F32), pltpu.VMEM((t, vw), F32)],
        compiler_params=_cparams("parallel", "arbitrary"),
        name="gla",
    )(qk, v, la, g, qk, v, la, g, qk_c, v_c, la_c, head_w)


def kernel(x, c, ctx, c_ctx, ada_w, ada_b, norm_mix_w, norm_ffn_w, ffn_w_gate, ffn_w_up, ffn_w_down,
           ab_w_in, ab_w_out, ab_sgu_norm_w, ab_sgu_w, ab_sgu_b, ab_rel_bias,
           cd_w_in, cd_w_out, cd_decay_w_fwd, cd_decay_b_fwd, cd_decay_w_bwd, cd_decay_b_bwd, cd_head_norm_w,
           final_norm_w):
    bsz, t, d = x.shape
    n_ctx = ctx.shape[1]
    tpb = t // ROW_TILE
    x2 = x.reshape(bsz * t, d)
    ctx2 = ctx.reshape(bsz * n_ctx, d)
    row = lambda w: w.reshape(1, -1)

    cc = jnp.concatenate([c, c_ctx[None, :], jnp.zeros((16 - bsz - 1, d), F32)], axis=0)
    mod = _ada(cc, ada_w, ada_b)
    mod_x = [mod[i, :bsz].reshape(bsz, 6, d) for i in range(2)]
    mod_c = [mod[i, bsz:bsz + 1].reshape(1, 6, d) for i in range(2)]
    ffn = [(ffn_w_gate[i].astype(BF16), ffn_w_up[i].astype(BF16), ffn_w_down[i].astype(BF16)) for i in range(2)]

    w_in = ab_w_in[0].astype(BF16)
    sgu = (row(ab_sgu_norm_w[0]), ab_sgu_w[0].astype(BF16),
           jnp.broadcast_to(ab_sgu_b[0][:, :, None], (A_GROUPS, A_CHUNK, LANES)))
    a_l, q_l, k_l, v_l = _ab_in(x2, mod_x[0], tpb, ROW_TILE, row(norm_mix_w[0]), w_in, *sgu)
    a_c, q_c, k_c, v_c = _ab_in(ctx2, mod_c[0], None, n_ctx, row(norm_mix_w[0]), w_in, *sgu)
    seq = lambda z, n: z.reshape(bsz, n, z.shape[-1])
    b_l = _na(seq(q_l, t), seq(k_l, t), seq(v_l, t), seq(k_c, n_ctx), seq(v_c, n_ctx), _na_bias_table(ab_rel_bias[0]))
    b_c = _ctx_attn(seq(q_c, n_ctx), seq(k_c, n_ctx), seq(v_c, n_ctx))
    wo = ab_w_out[0].astype(BF16)
    x2 = _out_ffn(a_l, b_l.reshape(bsz * t, -1), x2, mod_x[0], tpb, ROW_TILE, wo, row(norm_ffn_w[0]), *ffn[0],
                  row(final_norm_w), False)
    ctx2 = _out_ffn(a_c, b_c.reshape(bsz * n_ctx, -1), ctx2, mod_c[0], None, n_ctx, wo, row(norm_ffn_w[0]),
                    *ffn[0], row(final_norm_w), False)

    w_in = cd_w_in[0]
    w_main = w_in[:, :O_A].astype(BF16)
    w_a = jnp.pad(w_in[:, O_A:], ((0, 0), (0, LANES - 2 * GLA_LOW_RANK))).astype(BF16)
    w_dec = jnp.zeros((LANES, 2 * GLA_KEY_WIDTH), F32)
    w_dec = w_dec.at[:GLA_LOW_RANK, :GLA_KEY_WIDTH].set(cd_decay_w_fwd[0])
    w_dec = w_dec.at[GLA_LOW_RANK:2 * GLA_LOW_RANK, GLA_KEY_WIDTH:].set(cd_decay_w_bwd[0]).astype(BF16)
    b_dec = jnp.concatenate([cd_decay_b_fwd[0], cd_decay_b_bwd[0]]).reshape(1, -1)
    cos, sin = _rope_tables(t)
    f_l, qk_l, v_l, g_l, la_l = _cd_in(x2, mod_x[1], tpb, ROW_TILE, row(norm_mix_w[1]), w_main, w_a, w_dec, b_dec,
                                      cos, sin, True)
    _, qk_c, v_c, _, la_c = _cd_in(ctx2, mod_c[1], None, n_ctx, row(norm_mix_w[1]), w_main, w_a, w_dec, b_dec,
                                   cos, sin, False)
    fm = _fourier(seq(f_l, t), _fft_tables(t))
    head_w = jnp.tile(cd_head_norm_w[0], GLA_HEADS).reshape(1, -1)
    go = _gla(seq(qk_l, t), seq(v_l, t), seq(la_l, t), seq(g_l, t),
              seq(qk_c, n_ctx), seq(v_c, n_ctx), seq(la_c, n_ctx), head_w)
    out = _out_ffn(fm.reshape(bsz * t, -1), go.reshape(bsz * t, -1), x2, mod_x[1], tpb, ROW_TILE,
                   cd_w_out[0].astype(BF16), row(norm_ffn_w[1]), *ffn[1], row(final_norm_w), True)
    return out.reshape(bsz, t, d)
```

```python
import functools

import numpy as np
import jax
import jax.numpy as jnp
from jax import lax
from jax.experimental import pallas as pl
from jax.experimental.pallas import tpu as pltpu

F32 = jnp.float32
BF16 = jnp.bfloat16

D_MODEL = 1024
CTX_LEN = 256
GRID_W = 64
A_WIDTH = 512
A_GROUPS = 4
A_CHUNK = 128
NA_HEADS = 8
NA_HEAD_DIM = 64
NA_WIDTH = 512
NA_WIN_ROWS = 8
NA_WIN_COLS = 16
O_NA_Q = 2 * A_WIDTH
FN_WIDTH = 512
FN_GROUP_CH = 128
GLA_HEADS = 4
GLA_VAL_WIDTH = 512
GLA_KEY_WIDTH = 256
GLA_DK = 64
GLA_DV = 128
GLA_LOW_RANK = 16
GLA_GATE_TEMP = 16.0
GLA_CHUNK = 64
O_Q = FN_WIDTH
O_V = O_Q + 2 * GLA_KEY_WIDTH
O_G = O_V + GLA_VAL_WIDTH
O_A = O_G + GLA_VAL_WIDTH
FFN_HIDDEN = 2816
ROPE_BASE = 10000.0
RMS_EPS = 1e-6
NEG_INF = -1e30

LANES = 128
VMEM_LIMIT_BYTES = 56 * 2 ** 20

ROW_TILE = 512
FFN_SPLIT = 2
GLA_TILE = 512
NA_UNROLL = 4
FFT_UNROLL = 8
GLA_UNROLL = 2


def _cparams(*sem):
    return pltpu.CompilerParams(dimension_semantics=sem, vmem_limit_bytes=VMEM_LIMIT_BYTES)


def _resident(shape):
    nd = len(shape)
    return pl.BlockSpec(shape, lambda *_: (0,) * nd, pipeline_mode=pl.Buffered(1))


def _dot(a, b):
    return jnp.dot(a, b, preferred_element_type=F32)


def _dot_nt(a, b):
    return lax.dot_general(a, b, (((1,), (1,)), ((), ())), preferred_element_type=F32)


def _dot_tn(a, b):
    return lax.dot_general(a, b, (((0,), (0,)), ((), ())), preferred_element_type=F32)


def _rms(x, w):
    return x * lax.rsqrt(jnp.mean(x * x, axis=-1, keepdims=True) + RMS_EPS) * w


def _rms_mod(x, w, shift, scale):
    return _rms(x, w) * (1.0 + scale) + shift


def _ada_kernel(c_ref, w_ref, b_ref, o_ref):
    c = c_ref[...]
    s = c * jax.nn.sigmoid(c)
    o_ref[0] = jnp.dot(s, w_ref[0], preferred_element_type=F32, precision=lax.Precision.HIGHEST) + b_ref[0]


def _ada(cc, ada_w, ada_b):
    depth, d, n = ada_w.shape
    r = cc.shape[0]
    tn = 1536
    return pl.pallas_call(
        _ada_kernel,
        out_shape=jax.ShapeDtypeStruct((depth, r, n), F32),
        grid=(depth, n // tn),
        in_specs=[pl.BlockSpec((r, d), lambda i, j: (0, 0)),
                  pl.BlockSpec((1, d, tn), lambda i, j: (i, 0, j)),
                  pl.BlockSpec((1, 1, tn), lambda i, j: (i, 0, j))],
        out_specs=pl.BlockSpec((1, r, tn), lambda i, j: (i, 0, j)),
        compiler_params=_cparams("arbitrary", "arbitrary"),
        name="ada_mod",
    )(cc, ada_w, ada_b.reshape(depth, 1, n))


def _mod_spec(tiles_per_batch):
    if tiles_per_batch is None:
        return pl.BlockSpec((1, 6, D_MODEL), lambda i: (0, 0, 0))
    return pl.BlockSpec((1, 6, D_MODEL), lambda i: (i // tiles_per_batch, 0, 0))


def _ab_in_kernel(x_ref, mod_ref, nw_ref, w_ref, snw_ref, sguw_ref, sgub_ref,
                  a_ref, q_ref, k_ref, v_ref, *, tm):
    h = _rms_mod(x_ref[...], nw_ref[...], mod_ref[0, 0:1, :], mod_ref[0, 1:2, :]).astype(BF16)
    uv = jax.nn.gelu(_dot(h, w_ref[:, :O_NA_Q]))
    u = uv[:, :A_WIDTH]
    v = _rms(uv[:, A_WIDTH:], snw_ref[...]).astype(BF16)
    for ci in range(tm // A_CHUNK):
        rs = slice(ci * A_CHUNK, (ci + 1) * A_CHUNK)
        for g in range(A_GROUPS):
            cs = slice(g * LANES, (g + 1) * LANES)
            gate = _dot(sguw_ref[g], v[rs, cs]) + sgub_ref[g]
            a_ref[rs, cs] = (u[rs, cs] * gate).astype(BF16)
    qkv = _dot(h, w_ref[:, O_NA_Q:])
    q_ref[...] = (qkv[:, :NA_WIDTH] * NA_HEAD_DIM ** -0.5).astype(BF16)
    k_ref[...] = qkv[:, NA_WIDTH:2 * NA_WIDTH].astype(BF16)
    v_ref[...] = qkv[:, 2 * NA_WIDTH:].astype(BF16)


def _ab_in(x2, mod, tiles_per_batch, tm, nw, w_bf, snw, sguw_bf, sgub_exp):
    m = x2.shape[0]
    n_in = w_bf.shape[1]
    row = lambda w: pl.BlockSpec((tm, w), lambda i: (i, 0))
    out = jax.ShapeDtypeStruct((m, A_WIDTH), BF16)
    return pl.pallas_call(
        functools.partial(_ab_in_kernel, tm=tm),
        out_shape=(out, out, out, out),
        grid=(m // tm,),
        in_specs=[row(D_MODEL), _mod_spec(tiles_per_batch), _resident((1, D_MODEL)),
                  _resident((D_MODEL, n_in)), _resident((1, A_WIDTH)),
                  _resident((A_GROUPS, A_CHUNK, A_CHUNK)), _resident((A_GROUPS, A_CHUNK, LANES))],
        out_specs=(row(A_WIDTH),) * 4,
        compiler_params=_cparams("parallel"),
        name="ab_in",
    )(x2, mod, nw, w_bf, snw, sguw_bf, sgub_exp)


def _na_bias_table(rel_bias):
    w, nrow = GRID_W, 2 * NA_WIN_ROWS - 1
    col = np.arange(w)
    col_start = np.clip(col - NA_WIN_COLS // 2, 0, w - NA_WIN_COLS)
    col_mask = (col[None, :] >= col_start[:, None]) & (col[None, :] < col_start[:, None] + NA_WIN_COLS)
    lo = w - NA_WIN_COLS
    r_ext = jnp.pad(rel_bias, ((0, 0), (0, 0), (lo, 2 * w - lo - (2 * NA_WIN_COLS - 1))))
    flat = jnp.tile(r_ext, (1, 1, w))[:, :, :w * (2 * w - 1)]
    toe = flat.reshape(NA_HEADS, nrow, w, 2 * w - 1)[:, :, :, w - 1:]
    toe = jnp.where(col_mask, toe, NEG_INF).reshape(NA_HEADS // 2, 2, nrow, w, w)
    strips = [jnp.concatenate([toe[:, :, j - o + NA_WIN_ROWS - 1] for j in range(NA_WIN_ROWS)], axis=-1)
              for o in range(NA_WIN_ROWS)]
    return jnp.stack([s.reshape(NA_HEADS // 2, 2 * w, NA_WIN_ROWS * w) for s in strips], axis=1)


def _stack_heads(qb):
    first = lax.broadcasted_iota(jnp.int32, (1, LANES), 1) < NA_HEAD_DIM
    zero = jnp.zeros_like(qb)
    return jnp.concatenate([jnp.where(first, qb, zero), jnp.where(first, zero, qb)], axis=0)


def _unstack_heads(r):
    n = r.shape[0] // 2
    first = lax.broadcasted_iota(jnp.int32, (1, LANES), 1) < NA_HEAD_DIM
    return jnp.where(first, r[:n], r[n:])


def _na_kernel(q_ref, k_ref, v_ref, kc_ref, vc_ref, tab_ref, o_ref, *, rows):
    kc = kc_ref[0]
    vc = vc_ref[0]
    win = NA_WIN_ROWS * GRID_W

    def body(r, carry):
        start = jnp.clip(r - NA_WIN_ROWS // 2, 0, rows - NA_WIN_ROWS)
        q0 = pl.multiple_of(r * GRID_W, GRID_W)
        k0 = pl.multiple_of(start * GRID_W, GRID_W)
        qs = _stack_heads(q_ref[0, pl.ds(q0, GRID_W), :])
        kb = k_ref[0, pl.ds(k0, win), :]
        vb = v_ref[0, pl.ds(k0, win), :]
        s_nb = _dot_nt(qs, kb) + tab_ref[0, r - start]
        s_cx = _dot_nt(qs, kc)
        m = jnp.maximum(jnp.max(s_nb, axis=-1, keepdims=True), jnp.max(s_cx, axis=-1, keepdims=True))
        e_nb = jnp.exp(s_nb - m)
        e_cx = jnp.exp(s_cx - m)
        l = jnp.sum(e_nb, axis=-1, keepdims=True) + jnp.sum(e_cx, axis=-1, keepdims=True)
        acc = _dot(e_nb.astype(BF16), vb) + _dot(e_cx.astype(BF16), vc)
        o_ref[0, pl.ds(q0, GRID_W), :] = _unstack_heads(acc / l).astype(BF16)
        return carry

    lax.fori_loop(0, rows, body, 0, unroll=NA_UNROLL)


def _na(q, k, v, kc, vc, table):
    b, t, _ = q.shape
    rows = t // GRID_W
    lat = pl.BlockSpec((1, t, LANES), lambda i, j: (i, 0, j))
    ctx = pl.BlockSpec((1, CTX_LEN, LANES), lambda i, j: (i, 0, j))
    return pl.pallas_call(
        functools.partial(_na_kernel, rows=rows),
        out_shape=jax.ShapeDtypeStruct((b, t, NA_WIDTH), BF16),
        grid=(b, NA_HEADS // 2),
        in_specs=[lat, lat, lat, ctx, ctx,
                  pl.BlockSpec((1, NA_WIN_ROWS, 2 * GRID_W, NA_WIN_ROWS * GRID_W), lambda i, j: (j, 0, 0, 0))],
        out_specs=lat,
        compiler_params=_cparams("parallel", "parallel"),
        name="na_attn",
    )(q, k, v, kc, vc, table)


def _ctx_attn_kernel(q_ref, k_ref, v_ref, o_ref):
    s = _dot_nt(_stack_heads(q_ref[0]), k_ref[0])
    e = jnp.exp(s - jnp.max(s, axis=-1, keepdims=True))
    acc = _dot(e.astype(BF16), v_ref[0])
    o_ref[0] = _unstack_heads(acc / jnp.sum(e, axis=-1, keepdims=True)).astype(BF16)


def _ctx_attn(q, k, v):
    b = q.shape[0]
    spec = pl.BlockSpec((1, CTX_LEN, LANES), lambda i, j: (i, 0, j))
    return pl.pallas_call(
        _ctx_attn_kernel,
        out_shape=jax.ShapeDtypeStruct((b, CTX_LEN, NA_WIDTH), BF16),
        grid=(b, NA_HEADS // 2),
        in_specs=[spec, spec, spec],
        out_specs=spec,
        compiler_params=_cparams("parallel", "parallel"),
        name="ctx_attn",
    )(q, k, v)


def _out_ffn_kernel(a_ref, b_ref, x_ref, mod_ref, wo_ref, nw_ref, wg_ref, wu_ref, wd_ref, fnw_ref, o_ref,
                    *, final):
    half = wo_ref.shape[0] // 2
    y = _dot(a_ref[...], wo_ref[:half, :]) + _dot(b_ref[...], wo_ref[half:, :])
    x1 = x_ref[...] + mod_ref[0, 2:3, :] * y
    h = _rms_mod(x1, nw_ref[...], mod_ref[0, 3:4, :], mod_ref[0, 4:5, :]).astype(BF16)
    piece = FFN_HIDDEN // FFN_SPLIT
    acc = None
    for j in range(FFN_SPLIT):
        cs = slice(j * piece, (j + 1) * piece)
        g = _dot(h, wg_ref[:, cs])
        act = (g * jax.nn.sigmoid(g) * _dot(h, wu_ref[:, cs])).astype(BF16)
        part = _dot(act, wd_ref[cs, :])
        acc = part if acc is None else acc + part
    x2 = x1 + mod_ref[0, 5:6, :] * acc
    o_ref[...] = _rms(x2, fnw_ref[...]) if final else x2


def _out_ffn(a, b, x2, mod, tiles_per_batch, tm, wo_bf, nw, wg_bf, wu_bf, wd_bf, fnw, final):
    m = x2.shape[0]
    row = lambda w: pl.BlockSpec((tm, w), lambda i: (i, 0))
    return pl.pallas_call(
        functools.partial(_out_ffn_kernel, final=final),
        out_shape=jax.ShapeDtypeStruct((m, D_MODEL), F32),
        grid=(m // tm,),
        in_specs=[row(a.shape[1]), row(b.shape[1]), row(D_MODEL), _mod_spec(tiles_per_batch),
                  _resident(wo_bf.shape), _resident((1, D_MODEL)), _resident(wg_bf.shape),
                  _resident(wu_bf.shape), _resident(wd_bf.shape), _resident((1, D_MODEL))],
        out_specs=row(D_MODEL),
        compiler_params=_cparams("parallel"),
        name="out_ffn_final" if final else "out_ffn",
    )(a, b, x2, mod, wo_bf, nw, wg_bf, wu_bf, wd_bf, fnw)


def _rope_tables(t):
    half = GLA_DK // 4
    inv_freq = ROPE_BASE ** (-np.arange(half, dtype=np.float64) / half)
    pos = np.arange(t)
    ang_r = (pos // GRID_W)[:, None] * inv_freq[None, :]
    ang_c = (pos % GRID_W)[:, None] * inv_freq[None, :]
    cos = np.concatenate([np.cos(ang_r)] * 2 + [np.cos(ang_c)] * 2, axis=1)
    sin = np.concatenate([-np.sin(ang_r), np.sin(ang_r), -np.sin(ang_c), np.sin(ang_c)], axis=1)
    return (jnp.asarray(np.tile(cos, (1, 2)), F32), jnp.asarray(np.tile(sin, (1, 2)), F32))


def _log_sigmoid(z):
    return jnp.minimum(z, 0.0) - jnp.log1p(jnp.exp(-jnp.abs(z)))


def _cd_in_kernel(x_ref, mod_ref, nw_ref, w_ref, wa_ref, wdec_ref, bdec_ref, cos_ref, sin_ref,
                  f_ref, qk_ref, v_ref, g_ref, la_ref, *, rope):
    h = _rms_mod(x_ref[...], nw_ref[...], mod_ref[0, 0:1, :], mod_ref[0, 1:2, :]).astype(BF16)
    f_ref[...] = _dot(h, w_ref[:, :O_Q])
    qk = _dot(h, w_ref[:, O_Q:O_V])
    if rope:
        reps = qk.shape[1] // LANES
        cos = jnp.concatenate([cos_ref[...]] * reps, axis=1)
        sin = jnp.concatenate([sin_ref[...]] * reps, axis=1)
        w = qk.shape[1]
        q16 = GLA_DK // 4
        first = (lax.broadcasted_iota(jnp.int32, (1, w), 1) % (2 * q16)) < q16
        partner = jnp.where(first, pltpu.roll(qk, w - q16, 1), pltpu.roll(qk, q16, 1))
        qk = qk * cos + partner * sin
    is_q = lax.broadcasted_iota(jnp.int32, (1, qk.shape[1]), 1) < GLA_KEY_WIDTH
    qk_ref[...] = jnp.where(is_q, qk * GLA_DK ** -0.5, qk)
    v_ref[...] = _dot(h, w_ref[:, O_V:O_G]).astype(BF16)
    g_ref[...] = _dot(h, w_ref[:, O_G:O_A])
    a = _dot(h, wa_ref[...]).astype(BF16)
    la_ref[...] = _log_sigmoid(_dot(a, wdec_ref[...]) + bdec_ref[...]) * (1.0 / GLA_GATE_TEMP)


def _cd_in(x2, mod, tiles_per_batch, tm, nw, w_bf, wa_bf, wdec_bf, bdec, cos, sin, rope):
    m = x2.shape[0]
    row = lambda w: pl.BlockSpec((tm, w), lambda i: (i, 0))
    if rope:
        tab = pl.BlockSpec((tm, LANES), lambda i: (i % tiles_per_batch, 0))
    else:
        tab = pl.BlockSpec((tm, LANES), lambda i: (0, 0))
    w512 = 2 * GLA_KEY_WIDTH
    outs = (jax.ShapeDtypeStruct((m, FN_WIDTH), F32), jax.ShapeDtypeStruct((m, w512), F32),
            jax.ShapeDtypeStruct((m, GLA_VAL_WIDTH), BF16), jax.ShapeDtypeStruct((m, GLA_VAL_WIDTH), F32),
            jax.ShapeDtypeStruct((m, w512), F32))
    return pl.pallas_call(
        functools.partial(_cd_in_kernel, rope=rope),
        out_shape=outs,
        grid=(m // tm,),
        in_specs=[row(D_MODEL), _mod_spec(tiles_per_batch), _resident((1, D_MODEL)),
                  _resident(w_bf.shape), _resident(wa_bf.shape), _resident(wdec_bf.shape),
                  _resident((1, w512)), tab, tab],
        out_specs=(row(FN_WIDTH), row(w512), row(GLA_VAL_WIDTH), row(GLA_VAL_WIDTH), row(w512)),
        compiler_params=_cparams("parallel"),
        name="cd_in_rope" if rope else "cd_in_ctx",
    )(x2, mod, nw, w_bf, wa_bf, wdec_bf, bdec, cos, sin)


def _fft_tables(t):
    n1 = GRID_W
    assert t == n1 * n1
    a = np.arange(n1)
    ang1 = 2 * np.pi * np.outer(a, a) / n1
    cs = np.concatenate([np.cos(ang1), -np.sin(ang1)], axis=0)
    kap = a[:, None, None] + n1 * a[None, :, None]
    ang2 = 2 * np.pi * kap * a[None, None, :] / t
    gr, gi = np.cos(ang2), -np.sin(ang2)
    g = np.concatenate([np.concatenate([gr, -gi], axis=2), np.concatenate([gi, gr], axis=2)], axis=1)
    c = np.arange(FN_GROUP_CH)
    ang3 = 2 * np.pi * np.outer(c, c) / FN_GROUP_CH
    norm = 1.0 / np.sqrt(t * FN_GROUP_CH)
    return (jnp.asarray(cs, F32), jnp.asarray(g, F32),
            jnp.asarray(np.cos(ang3) * norm, F32), jnp.asarray(np.sin(ang3) * norm, F32))


def _fft_kernel(x_ref, cs_ref, g_ref, cc_ref, sc_ref, o_ref, p_scr, zr_scr, zi_scr):
    n1 = GRID_W
    cs = cs_ref[...].astype(BF16)

    def stage1(b, carry):
        xb = x_ref[0, pl.ds(b, n1, stride=n1), :].astype(BF16)
        p_scr[pl.ds(pl.multiple_of(b * 2 * n1, 2 * n1), 2 * n1), :] = _dot(cs, xb)
        return carry

    lax.fori_loop(0, n1, stage1, 0, unroll=FFT_UNROLL)

    def stage2(k1, carry):
        pr = p_scr[pl.ds(k1, n1, stride=2 * n1), :]
        pi = p_scr[pl.ds(n1 + k1, n1, stride=2 * n1), :]
        z = _dot(g_ref[k1].astype(BF16), jnp.concatenate([pr, pi], axis=0).astype(BF16))
        zr_scr[pl.ds(k1, n1, stride=n1), :] = z[:n1]
        zi_scr[pl.ds(k1, n1, stride=n1), :] = z[n1:]
        return carry

    lax.fori_loop(0, n1, stage2, 0, unroll=FFT_UNROLL)

    cc = cc_ref[...].astype(BF16)
    sc = sc_ref[...].astype(BF16)
    for gi in range(o_ref.shape[2] // FN_GROUP_CH):
        cols = slice(gi * FN_GROUP_CH, (gi + 1) * FN_GROUP_CH)
        out = _dot(zr_scr[:, cols].astype(BF16), cc) + _dot(zi_scr[:, cols].astype(BF16), sc)
        o_ref[0, :, cols] = out.astype(BF16)


def _fourier(f, tables):
    b, t, w = f.shape
    cw = FN_GROUP_CH
    cs, g, cc, sc = tables
    blk = pl.BlockSpec((1, t, cw), lambda i, j: (i, 0, j))
    return pl.pallas_call(
        _fft_kernel,
        out_shape=jax.ShapeDtypeStruct((b, t, w), BF16),
        grid=(b, w // cw),
        in_specs=[blk, _resident(cs.shape), _resident(g.shape), _resident(cc.shape), _resident(sc.shape)],
        out_specs=blk,
        scratch_shapes=[pltpu.VMEM((2 * t, cw), F32), pltpu.VMEM((t, cw), F32), pltpu.VMEM((t, cw), F32)],
        compiler_params=_cparams("parallel", "parallel"),
        name="fourier_mix",
    )(f, cs, g, cc, sc)


def _gla_masks():
    c = GLA_CHUNK
    tl = lax.broadcasted_iota(jnp.int32, (c, 3 * c), 0)
    tm = lax.broadcasted_iota(jnp.int32, (c, 3 * c), 1) % c
    al = lax.broadcasted_iota(jnp.int32, (c, GLA_HEADS * c), 0)
    am = lax.broadcasted_iota(jnp.int32, (c, GLA_HEADS * c), 1) % c
    return ((tl >= tm).astype(BF16), al >= am), ((tl <= tm).astype(BF16), al <= am)


def _block_diag(x, block):
    head = lax.broadcasted_iota(jnp.int32, (1, x.shape[1]), 1) // block
    return jnp.concatenate([jnp.where(head == h, x, jnp.zeros_like(x)) for h in range(GLA_HEADS)], axis=0)


def _cum_decay(tri3, la):
    hi = la.astype(BF16)
    rest = la - hi.astype(F32)
    mid = rest.astype(BF16)
    lo = (rest - mid.astype(F32)).astype(BF16)
    return _dot(tri3, jnp.concatenate([hi, mid, lo], axis=0))


def _gla_chunk(q, k, v, la, state, masks, last_row, want_out):
    tri3, att_mask = masks
    b = _cum_decay(tri3, la)
    b_last = b[last_row:last_row + 1, :]
    kd = _block_diag(k * jnp.exp(b_last - b), GLA_DK).astype(BF16)
    v_rows = jnp.concatenate([v[:, h * GLA_DV:(h + 1) * GLA_DV] for h in range(GLA_HEADS)], axis=0)
    new_state = state * jnp.exp(b_last) + _dot_tn(v_rows, kd)
    if not want_out:
        return None, new_state
    qe = (q * jnp.exp(b)).astype(BF16)
    ke = _block_diag(k * jnp.exp(-b), GLA_DK).astype(BF16)
    att = jnp.where(att_mask, _dot_nt(qe, ke), 0.0).astype(BF16)
    s_bd = _block_diag(state, GLA_DK).astype(BF16)
    return _dot(att, _block_diag(v, GLA_DV)) + _dot_nt(qe, s_bd), new_state


def _gla_kernel(qkf_ref, vf_ref, laf_ref, gf_ref, qkb_ref, vb_ref, lab_ref, gb_ref,
                kc_ref, vc_ref, lac_ref, hw_ref, o_ref, sf_scr, sb_scr, acc_scr, *, n_tiles, tile):
    c = GLA_CHUNK
    i = pl.program_id(1)
    fwd_masks, rev_masks = _gla_masks()
    kw = GLA_KEY_WIDTH

    @pl.when(i == 0)
    def _():
        n_ctx = kc_ref.shape[1] // c
        sf = jnp.zeros((GLA_DV, kw), F32)
        sb = jnp.zeros((GLA_DV, kw), F32)
        for n in range(n_ctx):
            rf = slice(n * c, (n + 1) * c)
            rb = slice((n_ctx - 1 - n) * c, (n_ctx - n) * c)
            _, sf = _gla_chunk(None, kc_ref[0, rf, :], vc_ref[0, rf, :], lac_ref[0, rf, :kw], sf,
                               fwd_masks, c - 1, False)
            _, sb = _gla_chunk(None, kc_ref[0, rb, :], vc_ref[0, rb, :], lac_ref[0, rb, kw:], sb,
                               rev_masks, 0, False)
        sf_scr[...] = sf
        sb_scr[...] = sb

    hw = hw_ref[...]

    def finish(o, g):
        parts = []
        for h in range(GLA_HEADS):
            oh = o[:, h * GLA_DV:(h + 1) * GLA_DV]
            parts.append(oh * lax.rsqrt(jnp.mean(oh * oh, axis=-1, keepdims=True) + RMS_EPS))
        return (jnp.concatenate(parts, axis=1) * hw * (g * jax.nn.sigmoid(g))).astype(BF16)

    def make_body(second_pass):
        def emit(o, g_ref, rloc, row0):
            rows = pl.ds(pl.multiple_of(row0, c), c)
            if second_pass:
                o_ref[0, rows, :] = finish(o + acc_scr[rows, :], g_ref[0, rloc, :])
            else:
                acc_scr[rows, :] = o

        def body(n, carry):
            rf = pl.ds(pl.multiple_of(n * c, c), c)
            qk = qkf_ref[0, rf, :]
            o, s = _gla_chunk(qk[:, :kw], qk[:, kw:], vf_ref[0, rf, :], laf_ref[0, rf, :], sf_scr[...],
                              fwd_masks, c - 1, True)
            sf_scr[...] = s
            emit(o, gf_ref, rf, i * tile + n * c)
            nb = tile // c - 1 - n
            rb = pl.ds(pl.multiple_of(nb * c, c), c)
            qk = qkb_ref[0, rb, :]
            o, s = _gla_chunk(qk[:, :kw], qk[:, kw:], vb_ref[0, rb, :], lab_ref[0, rb, :], sb_scr[...],
                              rev_masks, 0, True)
            sb_scr[...] = s
            emit(o, gb_ref, rb, (n_tiles - 1 - i) * tile + nb * c)
            return carry

        return body

    @pl.when(i < n_tiles // 2)
    def _():
        lax.fori_loop(0, tile // c, make_body(False), 0, unroll=GLA_UNROLL)

    @pl.when(i >= n_tiles // 2)
    def _():
        lax.fori_loop(0, tile // c, make_body(True), 0, unroll=GLA_UNROLL)


def _gla(qk, v, la, g, qk_c, v_c, la_c, head_w):
    b, t, _ = qk.shape
    tile = GLA_TILE
    n_tiles = t // tile
    kw, vw = GLA_KEY_WIDTH, GLA_VAL_WIDTH
    fwd = lambda w, col: pl.BlockSpec((1, tile, w), lambda bi, i: (bi, i, col))
    bwd = lambda w, col: pl.BlockSpec((1, tile, w), lambda bi, i: (bi, n_tiles - 1 - i, col))
    ctx = lambda w: pl.BlockSpec((1, CTX_LEN, w), lambda bi, i: (bi, 0, 0))
    return pl.pallas_call(
        functools.partial(_gla_kernel, n_tiles=n_tiles, tile=tile),
        out_shape=jax.ShapeDtypeStruct((b, t, vw), BF16),
        grid=(b, n_tiles),
        in_specs=[fwd(2 * kw, 0), fwd(vw, 0), fwd(kw, 0), fwd(vw, 0),
                  bwd(2 * kw, 0), bwd(vw, 0), bwd(kw, 1), bwd(vw, 0),
                  pl.BlockSpec((1, CTX_LEN, kw), lambda bi, i: (bi, 0, 1)), ctx(vw), ctx(2 * kw),
                  pl.BlockSpec((1, vw), lambda bi, i: (0, 0))],
        out_specs=pl.BlockSpec((1, t, vw), lambda bi, i: (bi, 0, 0)),
        scratch_shapes=[pltpu.VMEM((GLA_DV, kw), F32), pltpu.VMEM((GLA_DV, kw), F32), pltpu.VMEM((t, vw), F32)],
        compiler_params=_cparams("parallel", "arbitrary"),
        name="gla",
    )(qk, v, la, g, qk, v, la, g, qk_c, v_c, la_c, head_w)


def kernel(x, c, ctx, c_ctx, ada_w, ada_b, norm_mix_w, norm_ffn_w, ffn_w_gate, ffn_w_up, ffn_w_down,
           ab_w_in, ab_w_out, ab_sgu_norm_w, ab_sgu_w, ab_sgu_b, ab_rel_bias,
           cd_w_in, cd_w_out, cd_decay_w_fwd, cd_decay_b_fwd, cd_decay_w_bwd, cd_decay_b_bwd, cd_head_norm_w,
           final_norm_w):
    bsz, t, d = x.shape
    n_ctx = ctx.shape[1]
    tpb = t // ROW_TILE
    x2 = x.reshape(bsz * t, d)
    ctx2 = ctx.reshape(bsz * n_ctx, d)
    row = lambda w: w.reshape(1, -1)

    cc = jnp.concatenate([c, c_ctx[None, :], jnp.zeros((16 - bsz - 1, d), F32)], axis=0)
    mod = _ada(cc, ada_w, ada_b)
    mod_x = [mod[i, :bsz].reshape(bsz, 6, d) for i in range(2)]
    mod_c = [mod[i, bsz:bsz + 1].reshape(1, 6, d) for i in range(2)]
    ffn = [(ffn_w_gate[i].astype(BF16), ffn_w_up[i].astype(BF16), ffn_w_down[i].astype(BF16)) for i in range(2)]

    w_in = ab_w_in[0].astype(BF16)
    sgu = (row(ab_sgu_norm_w[0]), ab_sgu_w[0].astype(BF16),
           jnp.broadcast_to(ab_sgu_b[0][:, :, None], (A_GROUPS, A_CHUNK, LANES)))
    a_l, q_l, k_l, v_l = _ab_in(x2, mod_x[0], tpb, ROW_TILE, row(norm_mix_w[0]), w_in, *sgu)
    a_c, q_c, k_c, v_c = _ab_in(ctx2, mod_c[0], None, n_ctx, row(norm_mix_w[0]), w_in, *sgu)
    seq = lambda z, n: z.reshape(bsz, n, z.shape[-1])
    b_l = _na(seq(q_l, t), seq(k_l, t), seq(v_l, t), seq(k_c, n_ctx), seq(v_c, n_ctx), _na_bias_table(ab_rel_bias[0]))
    b_c = _ctx_attn(seq(q_c, n_ctx), seq(k_c, n_ctx), seq(v_c, n_ctx))
    wo = ab_w_out[0].astype(BF16)
    x2 = _out_ffn(a_l, b_l.reshape(bsz * t, -1), x2, mod_x[0], tpb, ROW_TILE, wo, row(norm_ffn_w[0]), *ffn[0],
                  row(final_norm_w), False)
    ctx2 = _out_ffn(a_c, b_c.reshape(bsz * n_ctx, -1), ctx2, mod_c[0], None, n_ctx, wo, row(norm_ffn_w[0]),
                    *ffn[0], row(final_norm_w), False)

    w_in = cd_w_in[0]
    w_main = w_in[:, :O_A].astype(BF16)
    w_a = jnp.pad(w_in[:, O_A:], ((0, 0), (0, LANES - 2 * GLA_LOW_RANK))).astype(BF16)
    w_dec = jnp.zeros((LANES, 2 * GLA_KEY_WIDTH), F32)
    w_dec = w_dec.at[:GLA_LOW_RANK, :GLA_KEY_WIDTH].set(cd_decay_w_fwd[0])
    w_dec = w_dec.at[GLA_LOW_RANK:2 * GLA_LOW_RANK, GLA_KEY_WIDTH:].set(cd_decay_w_bwd[0]).astype(BF16)
    b_dec = jnp.concatenate([cd_decay_b_fwd[0], cd_decay_b_bwd[0]]).reshape(1, -1)
    cos, sin = _rope_tables(t)
    f_l, qk_l, v_l, g_l, la_l = _cd_in(x2, mod_x[1], tpb, ROW_TILE, row(norm_mix_w[1]), w_main, w_a, w_dec, b_dec,
                                      cos, sin, True)
    _, qk_c, v_c, _, la_c = _cd_in(ctx2, mod_c[1], None, n_ctx, row(norm_mix_w[1]), w_main, w_a, w_dec, b_dec,
                                   cos, sin, False)
    fm = _fourier(seq(f_l, t), _fft_tables(t))
    head_w = jnp.tile(cd_head_norm_w[0], GLA_HEADS).reshape(1, -1)
    go = _gla(seq(qk_l, t), seq(v_l, t), seq(la_l, t), seq(g_l, t),
              seq(qk_c, n_ctx), seq(v_c, n_ctx), seq(la_c, n_ctx), head_w)
    out = _out_ffn(fm.reshape(bsz * t, -1), go.reshape(bsz * t, -1), x2, mod_x[1], tpb, ROW_TILE,
                   cd_w_out[0].astype(BF16), row(norm_ffn_w[1]), *ffn[1], row(final_norm_w), True)
    return out.reshape(bsz, t, d)
```

```python
import functools

import numpy as np
import jax
import jax.numpy as jnp
from jax import lax
from jax.experimental import pallas as pl
from jax.experimental.pallas import tpu as pltpu

F32 = jnp.float32
BF16 = jnp.bfloat16

D_MODEL = 1024
CTX_LEN = 256
GRID_W = 64
A_WIDTH = 512
A_GROUPS = 4
A_CHUNK = 128
NA_HEADS = 8
NA_HEAD_DIM = 64
NA_WIDTH = 512
NA_WIN_ROWS = 8
NA_WIN_COLS = 16
O_NA_Q = 2 * A_WIDTH
FN_WIDTH = 512
FN_GROUP_CH = 128
GLA_HEADS = 4
GLA_VAL_WIDTH = 512
GLA_KEY_WIDTH = 256
GLA_DK = 64
GLA_DV = 128
GLA_LOW_RANK = 16
GLA_GATE_TEMP = 16.0
GLA_CHUNK = 64
O_Q = FN_WIDTH
O_V = O_Q + 2 * GLA_KEY_WIDTH
O_G = O_V + GLA_VAL_WIDTH
O_A = O_G + GLA_VAL_WIDTH
FFN_HIDDEN = 2816
ROPE_BASE = 10000.0
RMS_EPS = 1e-6
NEG_INF = -1e30

LANES = 128
VMEM_LIMIT_BYTES = 56 * 2 ** 20

ROW_TILE = 512
FFN_SPLIT = 2
GLA_TILE = 512
NA_PIPE_SLOTS = 4
FFT_UNROLL = 16
FFT_PITCH = GRID_W + 8
FFT_PITCH2 = 2 * GRID_W + 8
GLA_UNROLL = 2


def _cparams(*sem):
    return pltpu.CompilerParams(dimension_semantics=sem, vmem_limit_bytes=VMEM_LIMIT_BYTES)


def _resident(shape):
    nd = len(shape)
    return pl.BlockSpec(shape, lambda *_: (0,) * nd, pipeline_mode=pl.Buffered(1))


def _dot(a, b):
    return jnp.dot(a, b, preferred_element_type=F32)


def _dot_nt(a, b):
    return lax.dot_general(a, b, (((1,), (1,)), ((), ())), preferred_element_type=F32)


def _dot_tn(a, b):
    return lax.dot_general(a, b, (((0,), (0,)), ((), ())), preferred_element_type=F32)


def _rms(x, w):
    return x * lax.rsqrt(jnp.mean(x * x, axis=-1, keepdims=True) + RMS_EPS) * w


def _rms_mod(x, w, shift, scale):
    return _rms(x, w) * (1.0 + scale) + shift


def _ada_kernel(c_ref, w_ref, b_ref, o_ref):
    c = c_ref[...]
    s = c * jax.nn.sigmoid(c)
    o_ref[0] = jnp.dot(s, w_ref[0], preferred_element_type=F32, precision=lax.Precision.HIGHEST) + b_ref[0]


def _ada(cc, ada_w, ada_b):
    depth, d, n = ada_w.shape
    r = cc.shape[0]
    tn = 1536
    return pl.pallas_call(
        _ada_kernel,
        out_shape=jax.ShapeDtypeStruct((depth, r, n), F32),
        grid=(depth, n // tn),
        in_specs=[pl.BlockSpec((r, d), lambda i, j: (0, 0)),
                  pl.BlockSpec((1, d, tn), lambda i, j: (i, 0, j)),
                  pl.BlockSpec((1, 1, tn), lambda i, j: (i, 0, j))],
        out_specs=pl.BlockSpec((1, r, tn), lambda i, j: (i, 0, j)),
        compiler_params=_cparams("arbitrary", "arbitrary"),
        name="ada_mod",
    )(cc, ada_w, ada_b.reshape(depth, 1, n))


def _mod_spec(tiles_per_batch):
    if tiles_per_batch is None:
        return pl.BlockSpec((1, 6, D_MODEL), lambda i: (0, 0, 0))
    return pl.BlockSpec((1, 6, D_MODEL), lambda i: (i // tiles_per_batch, 0, 0))


def _ab_in_kernel(x_ref, mod_ref, nw_ref, w_ref, snw_ref, sguw_ref, sgub_ref,
                  a_ref, q_ref, k_ref, v_ref, *, tm):
    h = _rms_mod(x_ref[...], nw_ref[...], mod_ref[0, 0:1, :], mod_ref[0, 1:2, :]).astype(BF16)
    uv = jax.nn.gelu(_dot(h, w_ref[:, :O_NA_Q]))
    u = uv[:, :A_WIDTH]
    v = _rms(uv[:, A_WIDTH:], snw_ref[...]).astype(BF16)
    for ci in range(tm // A_CHUNK):
        rs = slice(ci * A_CHUNK, (ci + 1) * A_CHUNK)
        for g in range(A_GROUPS):
            cs = slice(g * LANES, (g + 1) * LANES)
            gate = _dot(sguw_ref[g], v[rs, cs]) + sgub_ref[g]
            a_ref[rs, cs] = (u[rs, cs] * gate).astype(BF16)
    qkv = _dot(h, w_ref[:, O_NA_Q:])
    q_ref[...] = (qkv[:, :NA_WIDTH] * NA_HEAD_DIM ** -0.5).astype(BF16)
    k_ref[...] = qkv[:, NA_WIDTH:2 * NA_WIDTH].astype(BF16)
    v_ref[...] = qkv[:, 2 * NA_WIDTH:].astype(BF16)


def _ab_in(x2, mod, tiles_per_batch, tm, nw, w_bf, snw, sguw_bf, sgub_exp):
    m = x2.shape[0]
    n_in = w_bf.shape[1]
    row = lambda w: pl.BlockSpec((tm, w), lambda i: (i, 0))
    out = jax.ShapeDtypeStruct((m, A_WIDTH), BF16)
    return pl.pallas_call(
        functools.partial(_ab_in_kernel, tm=tm),
        out_shape=(out, out, out, out),
        grid=(m // tm,),
        in_specs=[row(D_MODEL), _mod_spec(tiles_per_batch), _resident((1, D_MODEL)),
                  _resident((D_MODEL, n_in)), _resident((1, A_WIDTH)),
                  _resident((A_GROUPS, A_CHUNK, A_CHUNK)), _resident((A_GROUPS, A_CHUNK, LANES))],
        out_specs=(row(A_WIDTH),) * 4,
        compiler_params=_cparams("parallel"),
        name="ab_in",
    )(x2, mod, nw, w_bf, snw, sguw_bf, sgub_exp)


def _na_bias_table(rel_bias):
    w, nrow = GRID_W, 2 * NA_WIN_ROWS - 1
    col = np.arange(w)
    col_start = np.clip(col - NA_WIN_COLS // 2, 0, w - NA_WIN_COLS)
    col_mask = (col[None, :] >= col_start[:, None]) & (col[None, :] < col_start[:, None] + NA_WIN_COLS)
    lo = w - NA_WIN_COLS
    r_ext = jnp.pad(rel_bias, ((0, 0), (0, 0), (lo, 2 * w - lo - (2 * NA_WIN_COLS - 1))))
    flat = jnp.tile(r_ext, (1, 1, w))[:, :, :w * (2 * w - 1)]
    toe = flat.reshape(NA_HEADS, nrow, w, 2 * w - 1)[:, :, :, w - 1:]
    toe = jnp.where(col_mask, toe, NEG_INF).reshape(NA_HEADS // 2, 2, nrow, w, w)
    strips = [jnp.concatenate([toe[:, :, j - o + NA_WIN_ROWS - 1] for j in range(NA_WIN_ROWS)], axis=-1)
              for o in range(NA_WIN_ROWS)]
    return jnp.stack([s.reshape(NA_HEADS // 2, 2 * w, NA_WIN_ROWS * w) for s in strips], axis=1)


def _stack_heads(qb):
    first = lax.broadcasted_iota(jnp.int32, (1, LANES), 1) < NA_HEAD_DIM
    zero = jnp.zeros_like(qb)
    return jnp.concatenate([jnp.where(first, qb, zero), jnp.where(first, zero, qb)], axis=0)


def _unstack_heads(r):
    n = r.shape[0] // 2
    first = lax.broadcasted_iota(jnp.int32, (1, LANES), 1) < NA_HEAD_DIM
    return jnp.where(first, r[:n], r[n:])


def _na_kernel(q_ref, k_ref, v_ref, kc_ref, vc_ref, tab_ref, o_ref, kt_scr, kct_scr, *scr, rows):
    t = rows * GRID_W
    kt_scr[0] = k_ref[0].T
    kt_scr[1, :, :t - GRID_W] = k_ref[0, GRID_W:, :].T
    kct_scr[...] = kc_ref[0].T
    kct = kct_scr[...]
    vc = vc_ref[0]
    win = NA_WIN_ROWS * GRID_W

    def scores(r, s_scr):
        start = jnp.clip(r - NA_WIN_ROWS // 2, 0, rows - NA_WIN_ROWS)
        odd = start % 2
        q0 = pl.multiple_of(r * GRID_W, GRID_W)
        kt0 = pl.multiple_of((start - odd) * GRID_W, 2 * GRID_W)
        qs = _stack_heads(q_ref[0, pl.ds(q0, GRID_W), :])
        s_scr[:, :win] = _dot(qs, kt_scr[odd, :, pl.ds(kt0, win)]) + tab_ref[0, r - start]
        s_scr[:, win:] = _dot(qs, kct)

    def softmax(s_scr, p_scr, l_scr):
        s = s_scr[...]
        e = jnp.exp(s - jnp.max(s, axis=-1, keepdims=True))
        l_scr[...] = jnp.sum(e, axis=-1, keepdims=True)
        p_scr[...] = e.astype(BF16)

    def values(r, p_scr, l_scr):
        start = jnp.clip(r - NA_WIN_ROWS // 2, 0, rows - NA_WIN_ROWS)
        q0 = pl.multiple_of(r * GRID_W, GRID_W)
        k0 = pl.multiple_of(start * GRID_W, GRID_W)
        acc = _dot(p_scr[:, :win], v_ref[0, pl.ds(k0, win), :]) + _dot(p_scr[:, win:], vc)
        o_ref[0, pl.ds(q0, GRID_W), :] = _unstack_heads(acc / l_scr[...]).astype(BF16)

    depth = NA_PIPE_SLOTS
    slots = [scr[3 * k:3 * k + 3] for k in range(depth)]

    def step(r, k, live=lambda row: True):
        if live(r - depth):
            values(r - depth, *slots[k][1:])
        if live(r - depth // 2):
            softmax(*slots[(k - depth // 2) % depth])
        if live(r):
            scores(r, slots[k][0])

    in_range = lambda row: 0 <= row < rows
    for r in range(depth):
        step(r, r, in_range)

    def body(i, carry):
        for k in range(depth):
            step(depth * i + k, k)
        return carry

    lax.fori_loop(1, rows // depth, body, 0)
    for r in range(rows, rows + depth):
        step(r, r % depth, in_range)


def _na(q, k, v, kc, vc, table):
    b, t, _ = q.shape
    rows = t // GRID_W
    n_keys = NA_WIN_ROWS * GRID_W + CTX_LEN
    lat = pl.BlockSpec((1, t, LANES), lambda i, j: (i, 0, j))
    ctx = pl.BlockSpec((1, CTX_LEN, LANES), lambda i, j: (i, 0, j))
    return pl.pallas_call(
        functools.partial(_na_kernel, rows=rows),
        out_shape=jax.ShapeDtypeStruct((b, t, NA_WIDTH), BF16),
        grid=(b, NA_HEADS // 2),
        in_specs=[lat, lat, lat, ctx, ctx,
                  pl.BlockSpec((1, NA_WIN_ROWS, 2 * GRID_W, NA_WIN_ROWS * GRID_W), lambda i, j: (j, 0, 0, 0))],
        out_specs=lat,
        scratch_shapes=[pltpu.VMEM((2, LANES, t), BF16), pltpu.VMEM((LANES, CTX_LEN), BF16)] + NA_PIPE_SLOTS * [
            pltpu.VMEM((2 * GRID_W, n_keys), F32), pltpu.VMEM((2 * GRID_W, n_keys), BF16),
            pltpu.VMEM((2 * GRID_W, 1), F32)],
        compiler_params=_cparams("parallel", "parallel"),
        name="na_attn",
    )(q, k, v, kc, vc, table)


def _ctx_attn_kernel(q_ref, k_ref, v_ref, o_ref):
    s = _dot_nt(_stack_heads(q_ref[0]), k_ref[0])
    e = jnp.exp(s - jnp.max(s, axis=-1, keepdims=True))
    acc = _dot(e.astype(BF16), v_ref[0])
    o_ref[0] = _unstack_heads(acc / jnp.sum(e, axis=-1, keepdims=True)).astype(BF16)


def _ctx_attn(q, k, v):
    b = q.shape[0]
    spec = pl.BlockSpec((1, CTX_LEN, LANES), lambda i, j: (i, 0, j))
    return pl.pallas_call(
        _ctx_attn_kernel,
        out_shape=jax.ShapeDtypeStruct((b, CTX_LEN, NA_WIDTH), BF16),
        grid=(b, NA_HEADS // 2),
        in_specs=[spec, spec, spec],
        out_specs=spec,
        compiler_params=_cparams("parallel", "parallel"),
        name="ctx_attn",
    )(q, k, v)


def _out_ffn_kernel(a_ref, b_ref, x_ref, mod_ref, wo_ref, nw_ref, wg_ref, wu_ref, wd_ref, fnw_ref, o_ref,
                    *, final):
    half = wo_ref.shape[0] // 2
    y = _dot(a_ref[...], wo_ref[:half, :]) + _dot(b_ref[...], wo_ref[half:, :])
    x1 = x_ref[...] + mod_ref[0, 2:3, :] * y
    h = _rms_mod(x1, nw_ref[...], mod_ref[0, 3:4, :], mod_ref[0, 4:5, :]).astype(BF16)
    piece = FFN_HIDDEN // FFN_SPLIT
    acc = None
    for j in range(FFN_SPLIT):
        cs = slice(j * piece, (j + 1) * piece)
        g = _dot(h, wg_ref[:, cs])
        act = (g * jax.nn.sigmoid(g) * _dot(h, wu_ref[:, cs])).astype(BF16)
        part = _dot(act, wd_ref[cs, :])
        acc = part if acc is None else acc + part
    x2 = x1 + mod_ref[0, 5:6, :] * acc
    o_ref[...] = _rms(x2, fnw_ref[...]) if final else x2


def _out_ffn(a, b, x2, mod, tiles_per_batch, tm, wo_bf, nw, wg_bf, wu_bf, wd_bf, fnw, final):
    m = x2.shape[0]
    row = lambda w: pl.BlockSpec((tm, w), lambda i: (i, 0))
    return pl.pallas_call(
        functools.partial(_out_ffn_kernel, final=final),
        out_shape=jax.ShapeDtypeStruct((m, D_MODEL), F32),
        grid=(m // tm,),
        in_specs=[row(a.shape[1]), row(b.shape[1]), row(D_MODEL), _mod_spec(tiles_per_batch),
                  _resident(wo_bf.shape), _resident((1, D_MODEL)), _resident(wg_bf.shape),
                  _resident(wu_bf.shape), _resident(wd_bf.shape), _resident((1, D_MODEL))],
        out_specs=row(D_MODEL),
        compiler_params=_cparams("parallel"),
        name="out_ffn_final" if final else "out_ffn",
    )(a, b, x2, mod, wo_bf, nw, wg_bf, wu_bf, wd_bf, fnw)


def _rope_tables(t):
    half = GLA_DK // 4
    inv_freq = ROPE_BASE ** (-np.arange(half, dtype=np.float64) / half)
    pos = np.arange(t)
    ang_r = (pos // GRID_W)[:, None] * inv_freq[None, :]
    ang_c = (pos % GRID_W)[:, None] * inv_freq[None, :]
    cos = np.concatenate([np.cos(ang_r)] * 2 + [np.cos(ang_c)] * 2, axis=1)
    sin = np.concatenate([-np.sin(ang_r), np.sin(ang_r), -np.sin(ang_c), np.sin(ang_c)], axis=1)
    return (jnp.asarray(np.tile(cos, (1, 2)), F32), jnp.asarray(np.tile(sin, (1, 2)), F32))


def _log_sigmoid(z):
    return jnp.minimum(z, 0.0) - jnp.log1p(jnp.exp(-jnp.abs(z)))


def _cd_in_kernel(x_ref, mod_ref, nw_ref, w_ref, wa_ref, wdec_ref, bdec_ref, cos_ref, sin_ref,
                  f_ref, qk_ref, v_ref, g_ref, la_ref, *, rope):
    h = _rms_mod(x_ref[...], nw_ref[...], mod_ref[0, 0:1, :], mod_ref[0, 1:2, :]).astype(BF16)
    f = _dot(h, w_ref[:, :O_Q])
    for a in range(f.shape[0] // GRID_W):
        f_ref[a * FFT_PITCH:a * FFT_PITCH + GRID_W, :] = f[a * GRID_W:(a + 1) * GRID_W]
        f_ref[a * FFT_PITCH + GRID_W:(a + 1) * FFT_PITCH, :] = jnp.zeros((FFT_PITCH - GRID_W, f.shape[1]), F32)
    qk = _dot(h, w_ref[:, O_Q:O_V])
    if rope:
        reps = qk.shape[1] // LANES
        cos = jnp.concatenate([cos_ref[...]] * reps, axis=1)
        sin = jnp.concatenate([sin_ref[...]] * reps, axis=1)
        w = qk.shape[1]
        q16 = GLA_DK // 4
        first = (lax.broadcasted_iota(jnp.int32, (1, w), 1) % (2 * q16)) < q16
        partner = jnp.where(first, pltpu.roll(qk, w - q16, 1), pltpu.roll(qk, q16, 1))
        qk = qk * cos + partner * sin
    is_q = lax.broadcasted_iota(jnp.int32, (1, qk.shape[1]), 1) < GLA_KEY_WIDTH
    qk_ref[...] = jnp.where(is_q, qk * GLA_DK ** -0.5, qk)
    v_ref[...] = _dot(h, w_ref[:, O_V:O_G]).astype(BF16)
    g_ref[...] = _dot(h, w_ref[:, O_G:O_A])
    a = _dot(h, wa_ref[...]).astype(BF16)
    la_ref[...] = _log_sigmoid(_dot(a, wdec_ref[...]) + bdec_ref[...]) * (1.0 / GLA_GATE_TEMP)


def _cd_in(x2, mod, tiles_per_batch, tm, nw, w_bf, wa_bf, wdec_bf, bdec, cos, sin, rope):
    m = x2.shape[0]
    row = lambda w: pl.BlockSpec((tm, w), lambda i: (i, 0))
    if rope:
        tab = pl.BlockSpec((tm, LANES), lambda i: (i % tiles_per_batch, 0))
    else:
        tab = pl.BlockSpec((tm, LANES), lambda i: (0, 0))
    w512 = 2 * GLA_KEY_WIDTH
    pad = lambda n: n // GRID_W * FFT_PITCH
    outs = (jax.ShapeDtypeStruct((pad(m), FN_WIDTH), F32), jax.ShapeDtypeStruct((m, w512), F32),
            jax.ShapeDtypeStruct((m, GLA_VAL_WIDTH), BF16), jax.ShapeDtypeStruct((m, GLA_VAL_WIDTH), F32),
            jax.ShapeDtypeStruct((m, w512), F32))
    return pl.pallas_call(
        functools.partial(_cd_in_kernel, rope=rope),
        out_shape=outs,
        grid=(m // tm,),
        in_specs=[row(D_MODEL), _mod_spec(tiles_per_batch), _resident((1, D_MODEL)),
                  _resident(w_bf.shape), _resident(wa_bf.shape), _resident(wdec_bf.shape),
                  _resident((1, w512)), tab, tab],
        out_specs=(pl.BlockSpec((pad(tm), FN_WIDTH), lambda i: (i, 0)),
                   row(w512), row(GLA_VAL_WIDTH), row(GLA_VAL_WIDTH), row(w512)),
        compiler_params=_cparams("parallel"),
        name="cd_in_rope" if rope else "cd_in_ctx",
    )(x2, mod, nw, w_bf, wa_bf, wdec_bf, bdec, cos, sin)


def _fft_tables(t):
    n1 = GRID_W
    assert t == n1 * n1
    a = np.arange(n1)
    ang1 = 2 * np.pi * np.outer(a, a) / n1
    cs = np.concatenate([np.cos(ang1), -np.sin(ang1)], axis=0)
    kap = a[:, None, None] + n1 * a[None, :, None]
    ang2 = 2 * np.pi * kap * a[None, None, :] / t
    gr, gi = np.cos(ang2), -np.sin(ang2)
    g = np.concatenate([np.concatenate([gr, -gi], axis=2), np.concatenate([gi, gr], axis=2)], axis=1)
    c = np.arange(FN_GROUP_CH)
    ang3 = 2 * np.pi * np.outer(c, c) / FN_GROUP_CH
    norm = 1.0 / np.sqrt(t * FN_GROUP_CH)
    return (jnp.asarray(cs, F32), jnp.asarray(g, F32),
            jnp.asarray(np.cos(ang3) * norm, F32), jnp.asarray(np.sin(ang3) * norm, F32))


def _fft_kernel(x_ref, cs_ref, g_ref, cc_ref, sc_ref, o_ref, p_scr, zr_scr, zi_scr):
    n1 = GRID_W
    cs = cs_ref[...].astype(BF16)

    def stage1(b, carry):
        xb = x_ref[0, pl.ds(b, n1, stride=FFT_PITCH), :].astype(BF16)
        p_scr[pl.ds(pl.multiple_of(b * FFT_PITCH2, 8), 2 * n1), :] = _dot(cs, xb)
        return carry

    lax.fori_loop(0, n1, stage1, 0, unroll=FFT_UNROLL)

    def stage2(k1, carry):
        pr = p_scr[pl.ds(k1, n1, stride=FFT_PITCH2), :]
        pi = p_scr[pl.ds(n1 + k1, n1, stride=FFT_PITCH2), :]
        z = _dot(g_ref[k1].astype(BF16), jnp.concatenate([pr, pi], axis=0).astype(BF16))
        rows = pl.ds(pl.multiple_of(k1 * FFT_PITCH, 8), n1)
        zr_scr[rows, :] = z[:n1]
        zi_scr[rows, :] = z[n1:]
        return carry

    lax.fori_loop(0, n1, stage2, 0, unroll=FFT_UNROLL)

    ccsc = jnp.concatenate([cc_ref[...], sc_ref[...]], axis=0).astype(BF16)

    def stage3(j, carry):
        parts = []
        for u in range(FFT_UNROLL):
            k2 = j * FFT_UNROLL + u
            parts.append(jnp.concatenate([zr_scr[pl.ds(k2, n1, stride=FFT_PITCH), :],
                                          zi_scr[pl.ds(k2, n1, stride=FFT_PITCH), :]], axis=1))
        out = _dot(jnp.concatenate(parts, axis=0).astype(BF16), ccsc)
        o_ref[0, pl.ds(pl.multiple_of(j * FFT_UNROLL * n1, FFT_UNROLL * n1), FFT_UNROLL * n1), :] = out.astype(BF16)
        return carry

    lax.fori_loop(0, n1 // FFT_UNROLL, stage3, 0)


def _fourier(f_padded, tables):
    b, tp, w = f_padded.shape
    t = tp // FFT_PITCH * GRID_W
    cw = FN_GROUP_CH
    cs, g, cc, sc = tables
    return pl.pallas_call(
        _fft_kernel,
        out_shape=jax.ShapeDtypeStruct((b, t, w), BF16),
        grid=(b, w // cw),
        in_specs=[pl.BlockSpec((1, tp, cw), lambda i, j: (i, 0, j)),
                  _resident(cs.shape), _resident(g.shape), _resident(cc.shape), _resident(sc.shape)],
        out_specs=pl.BlockSpec((1, t, cw), lambda i, j: (i, 0, j)),
        scratch_shapes=[pltpu.VMEM((GRID_W * FFT_PITCH2, cw), F32), pltpu.VMEM((tp, cw), F32),
                        pltpu.VMEM((tp, cw), F32)],
        compiler_params=_cparams("parallel", "parallel"),
        name="fourier_mix",
    )(f_padded, cs, g, cc, sc)


def _gla_masks():
    c = GLA_CHUNK
    tl = lax.broadcasted_iota(jnp.int32, (c, 3 * c), 0)
    tm = lax.broadcasted_iota(jnp.int32, (c, 3 * c), 1) % c
    al = lax.broadcasted_iota(jnp.int32, (c, GLA_HEADS * c), 0)
    am = lax.broadcasted_iota(jnp.int32, (c, GLA_HEADS * c), 1) % c
    return ((tl >= tm).astype(BF16), al >= am), ((tl <= tm).astype(BF16), al <= am)


def _block_diag(x, block):
    head = lax.broadcasted_iota(jnp.int32, (1, x.shape[1]), 1) // block
    return jnp.concatenate([jnp.where(head == h, x, jnp.zeros_like(x)) for h in range(GLA_HEADS)], axis=0)


def _cum_decay(tri3, la):
    hi = la.astype(BF16)
    rest = la - hi.astype(F32)
    mid = rest.astype(BF16)
    lo = (rest - mid.astype(F32)).astype(BF16)
    return _dot(tri3, jnp.concatenate([hi, mid, lo], axis=0))


def _gla_chunk(q, k, v, la, state, masks, last_row, want_out):
    tri3, att_mask = masks
    b = _cum_decay(tri3, la)
    b_last = b[last_row:last_row + 1, :]
    kd = _block_diag(k * jnp.exp(b_last - b), GLA_DK).astype(BF16)
    v_rows = jnp.concatenate([v[:, h * GLA_DV:(h + 1) * GLA_DV] for h in range(GLA_HEADS)], axis=0)
    new_state = state * jnp.exp(b_last) + _dot_tn(v_rows, kd)
    if not want_out:
        return None, new_state
    qe = (q * jnp.exp(b)).astype(BF16)
    ke = _block_diag(k * jnp.exp(-b), GLA_DK).astype(BF16)
    att = jnp.where(att_mask, _dot_nt(qe, ke), 0.0).astype(BF16)
    s_bd = _block_diag(state, GLA_DK).astype(BF16)
    return _dot(att, _block_diag(v, GLA_DV)) + _dot_nt(qe, s_bd), new_state


def _gla_kernel(qkf_ref, vf_ref, laf_ref, gf_ref, qkb_ref, vb_ref, lab_ref, gb_ref,
                kc_ref, vc_ref, lac_ref, hw_ref, o_ref, sf_scr, sb_scr, acc_scr, *, n_tiles, tile):
    c = GLA_CHUNK
    i = pl.program_id(1)
    fwd_masks, rev_masks = _gla_masks()
    kw = GLA_KEY_WIDTH

    @pl.when(i == 0)
    def _():
        n_ctx = kc_ref.shape[1] // c
        sf = jnp.zeros((GLA_DV, kw), F32)
        sb = jnp.zeros((GLA_DV, kw), F32)
        for n in range(n_ctx):
            rf = slice(n * c, (n + 1) * c)
            rb = slice((n_ctx - 1 - n) * c, (n_ctx - n) * c)
            _, sf = _gla_chunk(None, kc_ref[0, rf, :], vc_ref[0, rf, :], lac_ref[0, rf, :kw], sf,
                               fwd_masks, c - 1, False)
            _, sb = _gla_chunk(None, kc_ref[0, rb, :], vc_ref[0, rb, :], lac_ref[0, rb, kw:], sb,
                               rev_masks, 0, False)
        sf_scr[...] = sf
        sb_scr[...] = sb

    hw = hw_ref[...]

    def finish(o, g):
        parts = []
        for h in range(GLA_HEADS):
            oh = o[:, h * GLA_DV:(h + 1) * GLA_DV]
            parts.append(oh * lax.rsqrt(jnp.mean(oh * oh, axis=-1, keepdims=True) + RMS_EPS))
        return (jnp.concatenate(parts, axis=1) * hw * (g * jax.nn.sigmoid(g))).astype(BF16)

    def make_body(second_pass):
        def emit(o, g_ref, rloc, row0):
            rows = pl.ds(pl.multiple_of(row0, c), c)
            if second_pass:
                o_ref[0, rows, :] = finish(o + acc_scr[rows, :], g_ref[0, rloc, :])
            else:
                acc_scr[rows, :] = o

        def body(n, carry):
            rf = pl.ds(pl.multiple_of(n * c, c), c)
            qk = qkf_ref[0, rf, :]
            o, s = _gla_chunk(qk[:, :kw], qk[:, kw:], vf_ref[0, rf, :], laf_ref[0, rf, :], sf_scr[...],
                              fwd_masks, c - 1, True)
            sf_scr[...] = s
            emit(o, gf_ref, rf, i * tile + n * c)
            nb = tile // c - 1 - n
            rb = pl.ds(pl.multiple_of(nb * c, c), c)
            qk = qkb_ref[0, rb, :]
            o, s = _gla_chunk(qk[:, :kw], qk[:, kw:], vb_ref[0, rb, :], lab_ref[0, rb, :], sb_scr[...],
                              rev_masks, 0, True)
            sb_scr[...] = s
            emit(o, gb_ref, rb, (n_tiles - 1 - i) * tile + nb * c)
            return carry

        return body

    @pl.when(i < n_tiles // 2)
    def _():
        lax.fori_loop(0, tile // c, make_body(False), 0, unroll=GLA_UNROLL)

    @pl.when(i >= n_tiles // 2)
    def _():
        lax.fori_loop(0, tile // c, make_body(True), 0, unroll=GLA_UNROLL)


def _gla(qk, v, la, g, qk_c, v_c, la_c, head_w):
    b, t, _ = qk.shape
    tile = GLA_TILE
    n_tiles = t // tile
    kw, vw = GLA_KEY_WIDTH, GLA_VAL_WIDTH
    fwd = lambda w, col: pl.BlockSpec((1, tile, w), lambda bi, i: (bi, i, col))
    bwd = lambda w, col: pl.BlockSpec((1, tile, w), lambda bi, i: (bi, n_tiles - 1 - i, col))
    ctx = lambda w: pl.BlockSpec((1, CTX_LEN, w), lambda bi, i: (bi, 0, 0))
    return pl.pallas_call(
        functools.partial(_gla_kernel, n_tiles=n_tiles, tile=tile),
        out_shape=jax.ShapeDtypeStruct((b, t, vw), BF16),
        grid=(b, n_tiles),
        in_specs=[fwd(2 * kw, 0), fwd(vw, 0), fwd(kw, 0), fwd(vw, 0),
                  bwd(2 * kw, 0), bwd(vw, 0), bwd(kw, 1), bwd(vw, 0),
                  pl.BlockSpec((1, CTX_LEN, kw), lambda bi, i: (bi, 0, 1)), ctx(vw), ctx(2 * kw),
                  pl.BlockSpec((1, vw), lambda bi, i: (0, 0))],
        out_specs=pl.BlockSpec((1, t, vw), lambda bi, i: (bi, 0, 0)),
        scratch_shapes=[pltpu.VMEM((GLA_DV, kw), F32), pltpu.VMEM((GLA_DV, kw), F32), pltpu.VMEM((t, vw), F32)],
        compiler_params=_cparams("parallel", "arbitrary"),
        name="gla",
    )(qk, v, la, g, qk, v, la, g, qk_c, v_c, la_c, head_w)


def kernel(x, c, ctx, c_ctx, ada_w, ada_b, norm_mix_w, norm_ffn_w, ffn_w_gate, ffn_w_up, ffn_w_down,
           ab_w_in, ab_w_out, ab_sgu_norm_w, ab_sgu_w, ab_sgu_b, ab_rel_bias,
           cd_w_in, cd_w_out, cd_decay_w_fwd, cd_decay_b_fwd, cd_decay_w_bwd, cd_decay_b_bwd, cd_head_norm_w,
           final_norm_w):
    bsz, t, d = x.shape
    n_ctx = ctx.shape[1]
    tpb = t // ROW_TILE
    x2 = x.reshape(bsz * t, d)
    ctx2 = ctx.reshape(bsz * n_ctx, d)
    row = lambda w: w.reshape(1, -1)

    cc = jnp.concatenate([c, c_ctx[None, :], jnp.zeros((16 - bsz - 1, d), F32)], axis=0)
    mod = _ada(cc, ada_w, ada_b)
    mod_x = [mod[i, :bsz].reshape(bsz, 6, d) for i in range(2)]
    mod_c = [mod[i, bsz:bsz + 1].reshape(1, 6, d) for i in range(2)]
    ffn = [(ffn_w_gate[i].astype(BF16), ffn_w_up[i].astype(BF16), ffn_w_down[i].astype(BF16)) for i in range(2)]

    w_in = ab_w_in[0].astype(BF16)
    sgu = (row(ab_sgu_norm_w[0]), ab_sgu_w[0].astype(BF16),
           jnp.broadcast_to(ab_sgu_b[0][:, :, None], (A_GROUPS, A_CHUNK, LANES)))
    a_l, q_l, k_l, v_l = _ab_in(x2, mod_x[0], tpb, ROW_TILE, row(norm_mix_w[0]), w_in, *sgu)
    a_c, q_c, k_c, v_c = _ab_in(ctx2, mod_c[0], None, n_ctx, row(norm_mix_w[0]), w_in, *sgu)
    seq = lambda z, n: z.reshape(bsz, n, z.shape[-1])
    b_l = _na(seq(q_l, t), seq(k_l, t), seq(v_l, t), seq(k_c, n_ctx), seq(v_c, n_ctx), _na_bias_table(ab_rel_bias[0]))
    b_c = _ctx_attn(seq(q_c, n_ctx), seq(k_c, n_ctx), seq(v_c, n_ctx))
    wo = ab_w_out[0].astype(BF16)
    x2 = _out_ffn(a_l, b_l.reshape(bsz * t, -1), x2, mod_x[0], tpb, ROW_TILE, wo, row(norm_ffn_w[0]), *ffn[0],
                  row(final_norm_w), False)
    ctx2 = _out_ffn(a_c, b_c.reshape(bsz * n_ctx, -1), ctx2, mod_c[0], None, n_ctx, wo, row(norm_ffn_w[0]),
                    *ffn[0], row(final_norm_w), False)

    w_in = cd_w_in[0]
    w_main = w_in[:, :O_A].astype(BF16)
    w_a = jnp.pad(w_in[:, O_A:], ((0, 0), (0, LANES - 2 * GLA_LOW_RANK))).astype(BF16)
    w_dec = jnp.zeros((LANES, 2 * GLA_KEY_WIDTH), F32)
    w_dec = w_dec.at[:GLA_LOW_RANK, :GLA_KEY_WIDTH].set(cd_decay_w_fwd[0])
    w_dec = w_dec.at[GLA_LOW_RANK:2 * GLA_LOW_RANK, GLA_KEY_WIDTH:].set(cd_decay_w_bwd[0]).astype(BF16)
    b_dec = jnp.concatenate([cd_decay_b_fwd[0], cd_decay_b_bwd[0]]).reshape(1, -1)
    cos, sin = _rope_tables(t)
    f_l, qk_l, v_l, g_l, la_l = _cd_in(x2, mod_x[1], tpb, ROW_TILE, row(norm_mix_w[1]), w_main, w_a, w_dec, b_dec,
                                      cos, sin, True)
    _, qk_c, v_c, _, la_c = _cd_in(ctx2, mod_c[1], None, n_ctx, row(norm_mix_w[1]), w_main, w_a, w_dec, b_dec,
                                   cos, sin, False)
    fm = _fourier(f_l.reshape(bsz, -1, FN_WIDTH), _fft_tables(t))
    head_w = jnp.tile(cd_head_norm_w[0], GLA_HEADS).reshape(1, -1)
    go = _gla(seq(qk_l, t), seq(v_l, t), seq(la_l, t), seq(g_l, t),
              seq(qk_c, n_ctx), seq(v_c, n_ctx), seq(la_c, n_ctx), head_w)
    out = _out_ffn(fm.reshape(bsz * t, -1), go.reshape(bsz * t, -1), x2, mod_x[1], tpb, ROW_TILE,
                   cd_w_out[0].astype(BF16), row(norm_ffn_w[1]), *ffn[1], row(final_norm_w), True)
    return out.reshape(bsz, t, d)
```

```python
import functools

import numpy as np
import jax
import jax.numpy as jnp
from jax import lax
from jax.experimental import pallas as pl
from jax.experimental.pallas import tpu as pltpu

F32 = jnp.float32
BF16 = jnp.bfloat16

D_MODEL = 1024
CTX_LEN = 256
GRID_W = 64
A_WIDTH = 512
A_GROUPS = 4
A_CHUNK = 128
NA_HEADS = 8
NA_HEAD_DIM = 64
NA_WIDTH = 512
NA_WIN_ROWS = 8
NA_WIN_COLS = 16
O_NA_Q = 2 * A_WIDTH
FN_WIDTH = 512
FN_GROUP_CH = 128
GLA_HEADS = 4
GLA_VAL_WIDTH = 512
GLA_KEY_WIDTH = 256
GLA_DK = 64
GLA_DV = 128
GLA_LOW_RANK = 16
GLA_GATE_TEMP = 16.0
GLA_CHUNK = 64
O_Q = FN_WIDTH
O_V = O_Q + 2 * GLA_KEY_WIDTH
O_G = O_V + GLA_VAL_WIDTH
O_A = O_G + GLA_VAL_WIDTH
FFN_HIDDEN = 2816
ROPE_BASE = 10000.0
RMS_EPS = 1e-6
NEG_INF = -1e30

LANES = 128
VMEM_LIMIT_BYTES = 56 * 2 ** 20

ROW_TILE = 512
FFN_SPLIT = 2
GLA_TILE = 512
NA_PIPE_SLOTS = 4
FFT_UNROLL = 16
FFT_PITCH = GRID_W + 8
FFT_PITCH2 = 2 * GRID_W + 8
GLA_PREP_AHEAD = 2


def _cparams(*sem):
    return pltpu.CompilerParams(dimension_semantics=sem, vmem_limit_bytes=VMEM_LIMIT_BYTES)


def _resident(shape):
    nd = len(shape)
    return pl.BlockSpec(shape, lambda *_: (0,) * nd, pipeline_mode=pl.Buffered(1))


def _dot(a, b):
    return jnp.dot(a, b, preferred_element_type=F32)


def _dot_nt(a, b):
    return lax.dot_general(a, b, (((1,), (1,)), ((), ())), preferred_element_type=F32)


def _dot_tn(a, b):
    return lax.dot_general(a, b, (((0,), (0,)), ((), ())), preferred_element_type=F32)


def _rms(x, w):
    return x * lax.rsqrt(jnp.mean(x * x, axis=-1, keepdims=True) + RMS_EPS) * w


def _rms_mod(x, w, shift, scale):
    return _rms(x, w) * (1.0 + scale) + shift


def _ada_kernel(c_ref, w_ref, b_ref, o_ref):
    c = c_ref[...]
    s = c * jax.nn.sigmoid(c)
    o_ref[0] = jnp.dot(s, w_ref[0], preferred_element_type=F32, precision=lax.Precision.HIGHEST) + b_ref[0]


def _ada(cc, ada_w, ada_b):
    depth, d, n = ada_w.shape
    r = cc.shape[0]
    tn = 1536
    return pl.pallas_call(
        _ada_kernel,
        out_shape=jax.ShapeDtypeStruct((depth, r, n), F32),
        grid=(depth, n // tn),
        in_specs=[pl.BlockSpec((r, d), lambda i, j: (0, 0)),
                  pl.BlockSpec((1, d, tn), lambda i, j: (i, 0, j)),
                  pl.BlockSpec((1, 1, tn), lambda i, j: (i, 0, j))],
        out_specs=pl.BlockSpec((1, r, tn), lambda i, j: (i, 0, j)),
        compiler_params=_cparams("arbitrary", "arbitrary"),
        name="ada_mod",
    )(cc, ada_w, ada_b.reshape(depth, 1, n))


def _mod_spec(tiles_per_batch):
    if tiles_per_batch is None:
        return pl.BlockSpec((1, 6, D_MODEL), lambda i: (0, 0, 0))
    return pl.BlockSpec((1, 6, D_MODEL), lambda i: (i // tiles_per_batch, 0, 0))


def _ab_in_kernel(x_ref, mod_ref, nw_ref, w_ref, snw_ref, sguw_ref, sgub_ref,
                  a_ref, q_ref, k_ref, v_ref, *, tm):
    h = _rms_mod(x_ref[...], nw_ref[...], mod_ref[0, 0:1, :], mod_ref[0, 1:2, :]).astype(BF16)
    uv = jax.nn.gelu(_dot(h, w_ref[:, :O_NA_Q]))
    u = uv[:, :A_WIDTH]
    v = _rms(uv[:, A_WIDTH:], snw_ref[...]).astype(BF16)
    for ci in range(tm // A_CHUNK):
        rs = slice(ci * A_CHUNK, (ci + 1) * A_CHUNK)
        for g in range(A_GROUPS):
            cs = slice(g * LANES, (g + 1) * LANES)
            gate = _dot(sguw_ref[g], v[rs, cs]) + sgub_ref[g]
            a_ref[rs, cs] = (u[rs, cs] * gate).astype(BF16)
    qkv = _dot(h, w_ref[:, O_NA_Q:])
    q_ref[...] = (qkv[:, :NA_WIDTH] * NA_HEAD_DIM ** -0.5).astype(BF16)
    k_ref[...] = qkv[:, NA_WIDTH:2 * NA_WIDTH].astype(BF16)
    v_ref[...] = qkv[:, 2 * NA_WIDTH:].astype(BF16)


def _ab_in(x2, mod, tiles_per_batch, tm, nw, w_bf, snw, sguw_bf, sgub_exp):
    m = x2.shape[0]
    n_in = w_bf.shape[1]
    row = lambda w: pl.BlockSpec((tm, w), lambda i: (i, 0))
    out = jax.ShapeDtypeStruct((m, A_WIDTH), BF16)
    return pl.pallas_call(
        functools.partial(_ab_in_kernel, tm=tm),
        out_shape=(out, out, out, out),
        grid=(m // tm,),
        in_specs=[row(D_MODEL), _mod_spec(tiles_per_batch), _resident((1, D_MODEL)),
                  _resident((D_MODEL, n_in)), _resident((1, A_WIDTH)),
                  _resident((A_GROUPS, A_CHUNK, A_CHUNK)), _resident((A_GROUPS, A_CHUNK, LANES))],
        out_specs=(row(A_WIDTH),) * 4,
        compiler_params=_cparams("parallel"),
        name="ab_in",
    )(x2, mod, nw, w_bf, snw, sguw_bf, sgub_exp)


def _na_bias_table(rel_bias):
    w, nrow = GRID_W, 2 * NA_WIN_ROWS - 1
    col = np.arange(w)
    col_start = np.clip(col - NA_WIN_COLS // 2, 0, w - NA_WIN_COLS)
    col_mask = (col[None, :] >= col_start[:, None]) & (col[None, :] < col_start[:, None] + NA_WIN_COLS)
    lo = w - NA_WIN_COLS
    r_ext = jnp.pad(rel_bias, ((0, 0), (0, 0), (lo, 2 * w - lo - (2 * NA_WIN_COLS - 1))))
    flat = jnp.tile(r_ext, (1, 1, w))[:, :, :w * (2 * w - 1)]
    toe = flat.reshape(NA_HEADS, nrow, w, 2 * w - 1)[:, :, :, w - 1:]
    toe = jnp.where(col_mask, toe, NEG_INF).reshape(NA_HEADS // 2, 2, nrow, w, w)
    strips = [jnp.concatenate([toe[:, :, j - o + NA_WIN_ROWS - 1] for j in range(NA_WIN_ROWS)], axis=-1)
              for o in range(NA_WIN_ROWS)]
    return jnp.stack([s.reshape(NA_HEADS // 2, 2 * w, NA_WIN_ROWS * w) for s in strips], axis=1)


def _stack_heads(qb):
    first = lax.broadcasted_iota(jnp.int32, (1, LANES), 1) < NA_HEAD_DIM
    zero = jnp.zeros_like(qb)
    return jnp.concatenate([jnp.where(first, qb, zero), jnp.where(first, zero, qb)], axis=0)


def _unstack_heads(r):
    n = r.shape[0] // 2
    first = lax.broadcasted_iota(jnp.int32, (1, LANES), 1) < NA_HEAD_DIM
    return jnp.where(first, r[:n], r[n:])


def _na_kernel(q_ref, k_ref, v_ref, kc_ref, vc_ref, tab_ref, o_ref, kt_scr, kct_scr, *scr, rows):
    t = rows * GRID_W
    kt_scr[0] = k_ref[0].T
    kt_scr[1, :, :t - GRID_W] = k_ref[0, GRID_W:, :].T
    kct_scr[...] = kc_ref[0].T
    kct = kct_scr[...]
    vc = vc_ref[0]
    win = NA_WIN_ROWS * GRID_W

    def scores(r, s_scr):
        start = jnp.clip(r - NA_WIN_ROWS // 2, 0, rows - NA_WIN_ROWS)
        odd = start % 2
        q0 = pl.multiple_of(r * GRID_W, GRID_W)
        kt0 = pl.multiple_of((start - odd) * GRID_W, 2 * GRID_W)
        qs = _stack_heads(q_ref[0, pl.ds(q0, GRID_W), :])
        s_scr[:, :win] = _dot(qs, kt_scr[odd, :, pl.ds(kt0, win)]) + tab_ref[0, r - start]
        s_scr[:, win:] = _dot(qs, kct)

    def softmax(s_scr, p_scr, l_scr):
        s = s_scr[...]
        e = jnp.exp(s - jnp.max(s, axis=-1, keepdims=True))
        l_scr[...] = jnp.sum(e, axis=-1, keepdims=True)
        p_scr[...] = e.astype(BF16)

    def values(r, p_scr, l_scr):
        start = jnp.clip(r - NA_WIN_ROWS // 2, 0, rows - NA_WIN_ROWS)
        q0 = pl.multiple_of(r * GRID_W, GRID_W)
        k0 = pl.multiple_of(start * GRID_W, GRID_W)
        acc = _dot(p_scr[:, :win], v_ref[0, pl.ds(k0, win), :]) + _dot(p_scr[:, win:], vc)
        o_ref[0, pl.ds(q0, GRID_W), :] = _unstack_heads(acc / l_scr[...]).astype(BF16)

    depth = NA_PIPE_SLOTS
    slots = [scr[3 * k:3 * k + 3] for k in range(depth)]

    def step(r, k, live=lambda row: True):
        if live(r - depth):
            values(r - depth, *slots[k][1:])
        if live(r - depth // 2):
            softmax(*slots[(k - depth // 2) % depth])
        if live(r):
            scores(r, slots[k][0])

    in_range = lambda row: 0 <= row < rows
    for r in range(depth):
        step(r, r, in_range)

    def body(i, carry):
        for k in range(depth):
            step(depth * i + k, k)
        return carry

    lax.fori_loop(1, rows // depth, body, 0)
    for r in range(rows, rows + depth):
        step(r, r % depth, in_range)


def _na(q, k, v, kc, vc, table):
    b, t, _ = q.shape
    rows = t // GRID_W
    n_keys = NA_WIN_ROWS * GRID_W + CTX_LEN
    lat = pl.BlockSpec((1, t, LANES), lambda i, j: (i, 0, j))
    ctx = pl.BlockSpec((1, CTX_LEN, LANES), lambda i, j: (i, 0, j))
    return pl.pallas_call(
        functools.partial(_na_kernel, rows=rows),
        out_shape=jax.ShapeDtypeStruct((b, t, NA_WIDTH), BF16),
        grid=(b, NA_HEADS // 2),
        in_specs=[lat, lat, lat, ctx, ctx,
                  pl.BlockSpec((1, NA_WIN_ROWS, 2 * GRID_W, NA_WIN_ROWS * GRID_W), lambda i, j: (j, 0, 0, 0))],
        out_specs=lat,
        scratch_shapes=[pltpu.VMEM((2, LANES, t), BF16), pltpu.VMEM((LANES, CTX_LEN), BF16)] + NA_PIPE_SLOTS * [
            pltpu.VMEM((2 * GRID_W, n_keys), F32), pltpu.VMEM((2 * GRID_W, n_keys), BF16),
            pltpu.VMEM((2 * GRID_W, 1), F32)],
        compiler_params=_cparams("parallel", "parallel"),
        name="na_attn",
    )(q, k, v, kc, vc, table)


def _ctx_attn_kernel(q_ref, k_ref, v_ref, o_ref):
    s = _dot_nt(_stack_heads(q_ref[0]), k_ref[0])
    e = jnp.exp(s - jnp.max(s, axis=-1, keepdims=True))
    acc = _dot(e.astype(BF16), v_ref[0])
    o_ref[0] = _unstack_heads(acc / jnp.sum(e, axis=-1, keepdims=True)).astype(BF16)


def _ctx_attn(q, k, v):
    b = q.shape[0]
    spec = pl.BlockSpec((1, CTX_LEN, LANES), lambda i, j: (i, 0, j))
    return pl.pallas_call(
        _ctx_attn_kernel,
        out_shape=jax.ShapeDtypeStruct((b, CTX_LEN, NA_WIDTH), BF16),
        grid=(b, NA_HEADS // 2),
        in_specs=[spec, spec, spec],
        out_specs=spec,
        compiler_params=_cparams("parallel", "parallel"),
        name="ctx_attn",
    )(q, k, v)


def _out_ffn_kernel(a_ref, b_ref, x_ref, mod_ref, wo_ref, nw_ref, wg_ref, wu_ref, wd_ref, fnw_ref, o_ref,
                    *, final):
    half = wo_ref.shape[0] // 2
    y = _dot(a_ref[...], wo_ref[:half, :]) + _dot(b_ref[...], wo_ref[half:, :])
    x1 = x_ref[...] + mod_ref[0, 2:3, :] * y
    h = _rms_mod(x1, nw_ref[...], mod_ref[0, 3:4, :], mod_ref[0, 4:5, :]).astype(BF16)
    piece = FFN_HIDDEN // FFN_SPLIT
    acc = None
    for j in range(FFN_SPLIT):
        cs = slice(j * piece, (j + 1) * piece)
        g = _dot(h, wg_ref[:, cs])
        act = (g * jax.nn.sigmoid(g) * _dot(h, wu_ref[:, cs])).astype(BF16)
        part = _dot(act, wd_ref[cs, :])
        acc = part if acc is None else acc + part
    x2 = x1 + mod_ref[0, 5:6, :] * acc
    o_ref[...] = _rms(x2, fnw_ref[...]) if final else x2


def _out_ffn(a, b, x2, mod, tiles_per_batch, tm, wo_bf, nw, wg_bf, wu_bf, wd_bf, fnw, final):
    m = x2.shape[0]
    row = lambda w: pl.BlockSpec((tm, w), lambda i: (i, 0))
    return pl.pallas_call(
        functools.partial(_out_ffn_kernel, final=final),
        out_shape=jax.ShapeDtypeStruct((m, D_MODEL), F32),
        grid=(m // tm,),
        in_specs=[row(a.shape[1]), row(b.shape[1]), row(D_MODEL), _mod_spec(tiles_per_batch),
                  _resident(wo_bf.shape), _resident((1, D_MODEL)), _resident(wg_bf.shape),
                  _resident(wu_bf.shape), _resident(wd_bf.shape), _resident((1, D_MODEL))],
        out_specs=row(D_MODEL),
        compiler_params=_cparams("parallel"),
        name="out_ffn_final" if final else "out_ffn",
    )(a, b, x2, mod, wo_bf, nw, wg_bf, wu_bf, wd_bf, fnw)


def _rope_tables(t):
    half = GLA_DK // 4
    inv_freq = ROPE_BASE ** (-np.arange(half, dtype=np.float64) / half)
    pos = np.arange(t)
    ang_r = (pos // GRID_W)[:, None] * inv_freq[None, :]
    ang_c = (pos % GRID_W)[:, None] * inv_freq[None, :]
    cos = np.concatenate([np.cos(ang_r)] * 2 + [np.cos(ang_c)] * 2, axis=1)
    sin = np.concatenate([-np.sin(ang_r), np.sin(ang_r), -np.sin(ang_c), np.sin(ang_c)], axis=1)
    return (jnp.asarray(np.tile(cos, (1, 2)), F32), jnp.asarray(np.tile(sin, (1, 2)), F32))


def _log_sigmoid(z):
    return jnp.minimum(z, 0.0) - jnp.log1p(jnp.exp(-jnp.abs(z)))


def _cd_in_kernel(x_ref, mod_ref, nw_ref, w_ref, wa_ref, wdec_ref, bdec_ref, cos_ref, sin_ref,
                  f_ref, qk_ref, v_ref, g_ref, la_ref, *, rope):
    h = _rms_mod(x_ref[...], nw_ref[...], mod_ref[0, 0:1, :], mod_ref[0, 1:2, :]).astype(BF16)
    f = _dot(h, w_ref[:, :O_Q])
    for a in range(f.shape[0] // GRID_W):
        f_ref[a * FFT_PITCH:a * FFT_PITCH + GRID_W, :] = f[a * GRID_W:(a + 1) * GRID_W]
        f_ref[a * FFT_PITCH + GRID_W:(a + 1) * FFT_PITCH, :] = jnp.zeros((FFT_PITCH - GRID_W, f.shape[1]), F32)
    qk = _dot(h, w_ref[:, O_Q:O_V])
    if rope:
        reps = qk.shape[1] // LANES
        cos = jnp.concatenate([cos_ref[...]] * reps, axis=1)
        sin = jnp.concatenate([sin_ref[...]] * reps, axis=1)
        w = qk.shape[1]
        q16 = GLA_DK // 4
        first = (lax.broadcasted_iota(jnp.int32, (1, w), 1) % (2 * q16)) < q16
        partner = jnp.where(first, pltpu.roll(qk, w - q16, 1), pltpu.roll(qk, q16, 1))
        qk = qk * cos + partner * sin
    is_q = lax.broadcasted_iota(jnp.int32, (1, qk.shape[1]), 1) < GLA_KEY_WIDTH
    qk_ref[...] = jnp.where(is_q, qk * GLA_DK ** -0.5, qk)
    v_ref[...] = _dot(h, w_ref[:, O_V:O_G]).astype(BF16)
    g_ref[...] = _dot(h, w_ref[:, O_G:O_A])
    a = _dot(h, wa_ref[...]).astype(BF16)
    la_ref[...] = _log_sigmoid(_dot(a, wdec_ref[...]) + bdec_ref[...]) * (1.0 / GLA_GATE_TEMP)


def _cd_in(x2, mod, tiles_per_batch, tm, nw, w_bf, wa_bf, wdec_bf, bdec, cos, sin, rope):
    m = x2.shape[0]
    row = lambda w: pl.BlockSpec((tm, w), lambda i: (i, 0))
    if rope:
        tab = pl.BlockSpec((tm, LANES), lambda i: (i % tiles_per_batch, 0))
    else:
        tab = pl.BlockSpec((tm, LANES), lambda i: (0, 0))
    w512 = 2 * GLA_KEY_WIDTH
    pad = lambda n: n // GRID_W * FFT_PITCH
    outs = (jax.ShapeDtypeStruct((pad(m), FN_WIDTH), F32), jax.ShapeDtypeStruct((m, w512), F32),
            jax.ShapeDtypeStruct((m, GLA_VAL_WIDTH), BF16), jax.ShapeDtypeStruct((m, GLA_VAL_WIDTH), F32),
            jax.ShapeDtypeStruct((m, w512), F32))
    return pl.pallas_call(
        functools.partial(_cd_in_kernel, rope=rope),
        out_shape=outs,
        grid=(m // tm,),
        in_specs=[row(D_MODEL), _mod_spec(tiles_per_batch), _resident((1, D_MODEL)),
                  _resident(w_bf.shape), _resident(wa_bf.shape), _resident(wdec_bf.shape),
                  _resident((1, w512)), tab, tab],
        out_specs=(pl.BlockSpec((pad(tm), FN_WIDTH), lambda i: (i, 0)),
                   row(w512), row(GLA_VAL_WIDTH), row(GLA_VAL_WIDTH), row(w512)),
        compiler_params=_cparams("parallel"),
        name="cd_in_rope" if rope else "cd_in_ctx",
    )(x2, mod, nw, w_bf, wa_bf, wdec_bf, bdec, cos, sin)


def _fft_tables(t):
    n1 = GRID_W
    assert t == n1 * n1
    a = np.arange(n1)
    ang1 = 2 * np.pi * np.outer(a, a) / n1
    cs = np.concatenate([np.cos(ang1), -np.sin(ang1)], axis=0)
    kap = a[:, None, None] + n1 * a[None, :, None]
    ang2 = 2 * np.pi * kap * a[None, None, :] / t
    gr, gi = np.cos(ang2), -np.sin(ang2)
    g = np.concatenate([np.concatenate([gr, -gi], axis=2), np.concatenate([gi, gr], axis=2)], axis=1)
    c = np.arange(FN_GROUP_CH)
    ang3 = 2 * np.pi * np.outer(c, c) / FN_GROUP_CH
    norm = 1.0 / np.sqrt(t * FN_GROUP_CH)
    return (jnp.asarray(cs, F32), jnp.asarray(g, F32),
            jnp.asarray(np.cos(ang3) * norm, F32), jnp.asarray(np.sin(ang3) * norm, F32))


def _fft_kernel(x_ref, cs_ref, g_ref, cc_ref, sc_ref, o_ref, p_scr, zr_scr, zi_scr):
    n1 = GRID_W
    cs = cs_ref[...].astype(BF16)

    def stage1(b, carry):
        xb = x_ref[0, pl.ds(b, n1, stride=FFT_PITCH), :].astype(BF16)
        p_scr[pl.ds(pl.multiple_of(b * FFT_PITCH2, 8), 2 * n1), :] = _dot(cs, xb)
        return carry

    lax.fori_loop(0, n1, stage1, 0, unroll=FFT_UNROLL)

    def stage2(k1, carry):
        pr = p_scr[pl.ds(k1, n1, stride=FFT_PITCH2), :]
        pi = p_scr[pl.ds(n1 + k1, n1, stride=FFT_PITCH2), :]
        z = _dot(g_ref[k1].astype(BF16), jnp.concatenate([pr, pi], axis=0).astype(BF16))
        rows = pl.ds(pl.multiple_of(k1 * FFT_PITCH, 8), n1)
        zr_scr[rows, :] = z[:n1]
        zi_scr[rows, :] = z[n1:]
        return carry

    lax.fori_loop(0, n1, stage2, 0, unroll=FFT_UNROLL)

    ccsc = jnp.concatenate([cc_ref[...], sc_ref[...]], axis=0).astype(BF16)

    def stage3(j, carry):
        parts = []
        for u in range(FFT_UNROLL):
            k2 = j * FFT_UNROLL + u
            parts.append(jnp.concatenate([zr_scr[pl.ds(k2, n1, stride=FFT_PITCH), :],
                                          zi_scr[pl.ds(k2, n1, stride=FFT_PITCH), :]], axis=1))
        out = _dot(jnp.concatenate(parts, axis=0).astype(BF16), ccsc)
        o_ref[0, pl.ds(pl.multiple_of(j * FFT_UNROLL * n1, FFT_UNROLL * n1), FFT_UNROLL * n1), :] = out.astype(BF16)
        return carry

    lax.fori_loop(0, n1 // FFT_UNROLL, stage3, 0)


def _fourier(f_padded, tables):
    b, tp, w = f_padded.shape
    t = tp // FFT_PITCH * GRID_W
    cw = FN_GROUP_CH
    cs, g, cc, sc = tables
    return pl.pallas_call(
        _fft_kernel,
        out_shape=jax.ShapeDtypeStruct((b, t, w), BF16),
        grid=(b, w // cw),
        in_specs=[pl.BlockSpec((1, tp, cw), lambda i, j: (i, 0, j)),
                  _resident(cs.shape), _resident(g.shape), _resident(cc.shape), _resident(sc.shape)],
        out_specs=pl.BlockSpec((1, t, cw), lambda i, j: (i, 0, j)),
        scratch_shapes=[pltpu.VMEM((GRID_W * FFT_PITCH2, cw), F32), pltpu.VMEM((tp, cw), F32),
                        pltpu.VMEM((tp, cw), F32)],
        compiler_params=_cparams("parallel", "parallel"),
        name="fourier_mix",
    )(f_padded, cs, g, cc, sc)


def _block_diag(x, block):
    head = lax.broadcasted_iota(jnp.int32, (1, x.shape[1]), 1) // block
    return jnp.concatenate([jnp.where(head == h, x, jnp.zeros_like(x)) for h in range(GLA_HEADS)], axis=0)


def _cum_rows(x, reverse):
    n, w = x.shape
    row = lax.broadcasted_iota(jnp.int32, (n, 1), 0)
    s = 1
    while s < n:
        if s < 8:
            if reverse:
                shifted = jnp.where(row < n - s, pltpu.roll(x, n - s, 0), 0.0)
            else:
                shifted = jnp.where(row >= s, pltpu.roll(x, s, 0), 0.0)
        else:
            pad = jnp.zeros((s, w), F32)
            shifted = jnp.concatenate([x[s:], pad] if reverse else [pad, x[:n - s]], axis=0)
        x = x + shifted
        s *= 2
    return x


def _gla_prep(q, k, la, reverse, want_out):
    c = GLA_CHUNK
    b = _cum_rows(la, reverse)
    b_last = b[0:1, :] if reverse else b[c - 1:c, :]
    kd = (k * jnp.exp(b_last - b)).astype(BF16)
    decay = jnp.exp(b_last)
    if not want_out:
        return kd, decay
    qe = (q * jnp.exp(b)).astype(BF16)
    ke = _block_diag(k * jnp.exp(-b), GLA_DK).astype(BF16)
    l = lax.broadcasted_iota(jnp.int32, (c, GLA_HEADS * c), 0)
    m = lax.broadcasted_iota(jnp.int32, (c, GLA_HEADS * c), 1) % c
    att = jnp.where((l <= m) if reverse else (l >= m), _dot_nt(qe, ke), 0.0).astype(BF16)
    return kd, decay, qe, att


def _gla_state_step(kd, decay, v, state):
    v_rows = jnp.concatenate([v[:, h * GLA_DV:(h + 1) * GLA_DV] for h in range(GLA_HEADS)], axis=0)
    return state * decay + _dot_tn(v_rows, _block_diag(kd, GLA_DK))


def _gla_apply(prep, v, state):
    c = GLA_CHUNK
    kd, decay, qe, att = prep
    inter = _dot_nt(_block_diag(qe, GLA_DK), state.astype(BF16))
    out = _dot(att, _block_diag(v, GLA_DV)) + jnp.concatenate(
        [inter[h * c:(h + 1) * c] for h in range(GLA_HEADS)], axis=1)
    return out, _gla_state_step(kd, decay, v, state)


def _gla_kernel(qkf_ref, vf_ref, laf_ref, gf_ref, qkb_ref, vb_ref, lab_ref, gb_ref,
                kc_ref, vc_ref, lac_ref, hw_ref, o_ref, sf_scr, sb_scr, acc_scr, *, n_tiles, tile):
    c = GLA_CHUNK
    i = pl.program_id(1)
    kw = GLA_KEY_WIDTH
    n_chunks = tile // c

    @pl.when(i == 0)
    def _():
        n_ctx = kc_ref.shape[1] // c
        sf = jnp.zeros((GLA_DV, kw), F32)
        sb = jnp.zeros((GLA_DV, kw), F32)
        for n in range(n_ctx):
            rf = slice(n * c, (n + 1) * c)
            rb = slice((n_ctx - 1 - n) * c, (n_ctx - n) * c)
            sf = _gla_state_step(*_gla_prep(None, kc_ref[0, rf, :], lac_ref[0, rf, :kw], False, False),
                                 vc_ref[0, rf, :], sf)
            sb = _gla_state_step(*_gla_prep(None, kc_ref[0, rb, :], lac_ref[0, rb, kw:], True, False),
                                 vc_ref[0, rb, :], sb)
        sf_scr[...] = sf
        sb_scr[...] = sb

    hw = hw_ref[...]

    def finish(o, g):
        parts = []
        for h in range(GLA_HEADS):
            oh = o[:, h * GLA_DV:(h + 1) * GLA_DV]
            parts.append(oh * lax.rsqrt(jnp.mean(oh * oh, axis=-1, keepdims=True) + RMS_EPS))
        return (jnp.concatenate(parts, axis=1) * hw * (g * jax.nn.sigmoid(g))).astype(BF16)

    def sweep(second_pass):
        def emit(o, g_ref, rloc, row0):
            rows = pl.ds(pl.multiple_of(row0, c), c)
            if second_pass:
                o_ref[0, rows, :] = finish(o + acc_scr[rows, :], g_ref[0, rloc, :])
            else:
                acc_scr[rows, :] = o

        def rows_of(n):
            nb = n_chunks - 1 - n
            return slice(n * c, (n + 1) * c), slice(nb * c, (nb + 1) * c)

        def prep(n):
            rf, rb = rows_of(n)
            qkf, qkb = qkf_ref[0, rf, :], qkb_ref[0, rb, :]
            return (_gla_prep(qkf[:, :kw], qkf[:, kw:], laf_ref[0, rf, :], False, True),
                    _gla_prep(qkb[:, :kw], qkb[:, kw:], lab_ref[0, rb, :], True, True))

        sf, sb = sf_scr[...], sb_scr[...]
        ready = {n: prep(n) for n in range(min(GLA_PREP_AHEAD, n_chunks))}
        for n in range(n_chunks):
            rf, rb = rows_of(n)
            pf, pb = ready.pop(n)
            of, sf = _gla_apply(pf, vf_ref[0, rf, :], sf)
            emit(of, gf_ref, rf, i * tile + rf.start)
            ob, sb = _gla_apply(pb, vb_ref[0, rb, :], sb)
            emit(ob, gb_ref, rb, (n_tiles - 1 - i) * tile + rb.start)
            if n + GLA_PREP_AHEAD < n_chunks:
                ready[n + GLA_PREP_AHEAD] = prep(n + GLA_PREP_AHEAD)
        sf_scr[...] = sf
        sb_scr[...] = sb

    @pl.when(i < n_tiles // 2)
    def _():
        sweep(False)

    @pl.when(i >= n_tiles // 2)
    def _():
        sweep(True)


def _gla(qk, v, la, g, qk_c, v_c, la_c, head_w):
    b, t, _ = qk.shape
    tile = GLA_TILE
    n_tiles = t // tile
    kw, vw = GLA_KEY_WIDTH, GLA_VAL_WIDTH
    fwd = lambda w, col: pl.BlockSpec((1, tile, w), lambda bi, i: (bi, i, col))
    bwd = lambda w, col: pl.BlockSpec((1, tile, w), lambda bi, i: (bi, n_tiles - 1 - i, col))
    ctx = lambda w: pl.BlockSpec((1, CTX_LEN, w), lambda bi, i: (bi, 0, 0))
    return pl.pallas_call(
        functools.partial(_gla_kernel, n_tiles=n_tiles, tile=tile),
        out_shape=jax.ShapeDtypeStruct((b, t, vw), BF16),
        grid=(b, n_tiles),
        in_specs=[fwd(2 * kw, 0), fwd(vw, 0), fwd(kw, 0), fwd(vw, 0),
                  bwd(2 * kw, 0), bwd(vw, 0), bwd(kw, 1), bwd(vw, 0),
                  pl.BlockSpec((1, CTX_LEN, kw), lambda bi, i: (bi, 0, 1)), ctx(vw), ctx(2 * kw),
                  pl.BlockSpec((1, vw), lambda bi, i: (0, 0))],
        out_specs=pl.BlockSpec((1, t, vw), lambda bi, i: (bi, 0, 0)),
        scratch_shapes=[pltpu.VMEM((GLA_DV, kw), F32), pltpu.VMEM((GLA_DV, kw), F32), pltpu.VMEM((t, vw), F32)],
        compiler_params=_cparams("parallel", "arbitrary"),
        name="gla",
    )(qk, v, la, g, qk, v, la, g, qk_c, v_c, la_c, head_w)


def kernel(x, c, ctx, c_ctx, ada_w, ada_b, norm_mix_w, norm_ffn_w, ffn_w_gate, ffn_w_up, ffn_w_down,
           ab_w_in, ab_w_out, ab_sgu_norm_w, ab_sgu_w, ab_sgu_b, ab_rel_bias,
           cd_w_in, cd_w_out, cd_decay_w_fwd, cd_decay_b_fwd, cd_decay_w_bwd, cd_decay_b_bwd, cd_head_norm_w,
           final_norm_w):
    bsz, t, d = x.shape
    n_ctx = ctx.shape[1]
    tpb = t // ROW_TILE
    x2 = x.reshape(bsz * t, d)
    ctx2 = ctx.reshape(bsz * n_ctx, d)
    row = lambda w: w.reshape(1, -1)

    cc = jnp.concatenate([c, c_ctx[None, :], jnp.zeros((16 - bsz - 1, d), F32)], axis=0)
    mod = _ada(cc, ada_w, ada_b)
    mod_x = [mod[i, :bsz].reshape(bsz, 6, d) for i in range(2)]
    mod_c = [mod[i, bsz:bsz + 1].reshape(1, 6, d) for i in range(2)]
    ffn = [(ffn_w_gate[i].astype(BF16), ffn_w_up[i].astype(BF16), ffn_w_down[i].astype(BF16)) for i in range(2)]

    w_in = ab_w_in[0].astype(BF16)
    sgu = (row(ab_sgu_norm_w[0]), ab_sgu_w[0].astype(BF16),
           jnp.broadcast_to(ab_sgu_b[0][:, :, None], (A_GROUPS, A_CHUNK, LANES)))
    a_l, q_l, k_l, v_l = _ab_in(x2, mod_x[0], tpb, ROW_TILE, row(norm_mix_w[0]), w_in, *sgu)
    a_c, q_c, k_c, v_c = _ab_in(ctx2, mod_c[0], None, n_ctx, row(norm_mix_w[0]), w_in, *sgu)
    seq = lambda z, n: z.reshape(bsz, n, z.shape[-1])
    b_l = _na(seq(q_l, t), seq(k_l, t), seq(v_l, t), seq(k_c, n_ctx), seq(v_c, n_ctx), _na_bias_table(ab_rel_bias[0]))
    b_c = _ctx_attn(seq(q_c, n_ctx), seq(k_c, n_ctx), seq(v_c, n_ctx))
    wo = ab_w_out[0].astype(BF16)
    x2 = _out_ffn(a_l, b_l.reshape(bsz * t, -1), x2, mod_x[0], tpb, ROW_TILE, wo, row(norm_ffn_w[0]), *ffn[0],
                  row(final_norm_w), False)
    ctx2 = _out_ffn(a_c, b_c.reshape(bsz * n_ctx, -1), ctx2, mod_c[0], None, n_ctx, wo, row(norm_ffn_w[0]),
                    *ffn[0], row(final_norm_w), False)

    w_in = cd_w_in[0]
    w_main = w_in[:, :O_A].astype(BF16)
    w_a = jnp.pad(w_in[:, O_A:], ((0, 0), (0, LANES - 2 * GLA_LOW_RANK))).astype(BF16)
    w_dec = jnp.zeros((LANES, 2 * GLA_KEY_WIDTH), F32)
    w_dec = w_dec.at[:GLA_LOW_RANK, :GLA_KEY_WIDTH].set(cd_decay_w_fwd[0])
    w_dec = w_dec.at[GLA_LOW_RANK:2 * GLA_LOW_RANK, GLA_KEY_WIDTH:].set(cd_decay_w_bwd[0]).astype(BF16)
    b_dec = jnp.concatenate([cd_decay_b_fwd[0], cd_decay_b_bwd[0]]).reshape(1, -1)
    cos, sin = _rope_tables(t)
    f_l, qk_l, v_l, g_l, la_l = _cd_in(x2, mod_x[1], tpb, ROW_TILE, row(norm_mix_w[1]), w_main, w_a, w_dec, b_dec,
                                      cos, sin, True)
    _, qk_c, v_c, _, la_c = _cd_in(ctx2, mod_c[1], None, n_ctx, row(norm_mix_w[1]), w_main, w_a, w_dec, b_dec,
                                   cos, sin, False)
    fm = _fourier(f_l.reshape(bsz, -1, FN_WIDTH), _fft_tables(t))
    head_w = jnp.tile(cd_head_norm_w[0], GLA_HEADS).reshape(1, -1)
    go = _gla(seq(qk_l, t), seq(v_l, t), seq(la_l, t), seq(g_l, t),
              seq(qk_c, n_ctx), seq(v_c, n_ctx), seq(la_c, n_ctx), head_w)
    out = _out_ffn(fm.reshape(bsz * t, -1), go.reshape(bsz * t, -1), x2, mod_x[1], tpb, ROW_TILE,
                   cd_w_out[0].astype(BF16), row(norm_ffn_w[1]), *ffn[1], row(final_norm_w), True)
    return out.reshape(bsz, t, d)
```

```python
import functools

import numpy as np
import jax
import jax.numpy as jnp
from jax import lax
from jax.experimental import pallas as pl
from jax.experimental.pallas import tpu as pltpu

F32 = jnp.float32
BF16 = jnp.bfloat16

D_MODEL = 1024
CTX_LEN = 256
GRID_W = 64
A_WIDTH = 512
A_GROUPS = 4
A_CHUNK = 128
NA_HEADS = 8
NA_HEAD_DIM = 64
NA_WIDTH = 512
NA_WIN_ROWS = 8
NA_WIN_COLS = 16
O_NA_Q = 2 * A_WIDTH
FN_WIDTH = 512
FN_GROUP_CH = 128
GLA_HEADS = 4
GLA_VAL_WIDTH = 512
GLA_KEY_WIDTH = 256
GLA_DK = 64
GLA_DV = 128
GLA_LOW_RANK = 16
GLA_GATE_TEMP = 16.0
GLA_CHUNK = 64
O_Q = FN_WIDTH
O_V = O_Q + 2 * GLA_KEY_WIDTH
O_G = O_V + GLA_VAL_WIDTH
O_A = O_G + GLA_VAL_WIDTH
FFN_HIDDEN = 2816
ROPE_BASE = 10000.0
RMS_EPS = 1e-6
NEG_INF = -1e30

LANES = 128
VMEM_LIMIT_BYTES = 56 * 2 ** 20

ROW_TILE = 512
ROW_SUBTILE = 512
FFN_SPLIT = 2
GLA_TILE = 512
NA_STAGE_LAG = 2
FFT_UNROLL = 32
FFT_PITCH = GRID_W + 8
FFT_PITCH2 = 2 * GRID_W + 8
GLA_PREP_AHEAD = 2


def _cparams(*sem):
    return pltpu.CompilerParams(dimension_semantics=sem, vmem_limit_bytes=VMEM_LIMIT_BYTES)


def _resident(shape):
    nd = len(shape)
    return pl.BlockSpec(shape, lambda *_: (0,) * nd, pipeline_mode=pl.Buffered(1))


def _dot(a, b):
    return jnp.dot(a, b, preferred_element_type=F32)


def _dot_nt(a, b):
    return lax.dot_general(a, b, (((1,), (1,)), ((), ())), preferred_element_type=F32)


def _dot_tn(a, b):
    return lax.dot_general(a, b, (((0,), (0,)), ((), ())), preferred_element_type=F32)


def _subtiles(n_rows):
    step = min(ROW_SUBTILE, n_rows)
    return [slice(r, r + step) for r in range(0, n_rows, step)]


def _rms(x, w):
    return x * lax.rsqrt(jnp.mean(x * x, axis=-1, keepdims=True) + RMS_EPS) * w


def _rms_mod(x, w, shift, scale):
    return _rms(x, w) * (1.0 + scale) + shift


def _ada_kernel(c_ref, w_ref, b_ref, o_ref):
    c = c_ref[...]
    s = c * jax.nn.sigmoid(c)
    o_ref[0] = jnp.dot(s, w_ref[0], preferred_element_type=F32, precision=lax.Precision.HIGHEST) + b_ref[0]


def _ada(cc, ada_w, ada_b):
    depth, d, n = ada_w.shape
    r = cc.shape[0]
    tn = 1536
    return pl.pallas_call(
        _ada_kernel,
        out_shape=jax.ShapeDtypeStruct((depth, r, n), F32),
        grid=(depth, n // tn),
        in_specs=[pl.BlockSpec((r, d), lambda i, j: (0, 0)),
                  pl.BlockSpec((1, d, tn), lambda i, j: (i, 0, j)),
                  pl.BlockSpec((1, 1, tn), lambda i, j: (i, 0, j))],
        out_specs=pl.BlockSpec((1, r, tn), lambda i, j: (i, 0, j)),
        compiler_params=_cparams("arbitrary", "arbitrary"),
        name="ada_mod",
    )(cc, ada_w, ada_b.reshape(depth, 1, n))


def _mod_spec(tiles_per_batch):
    if tiles_per_batch is None:
        return pl.BlockSpec((1, 6, D_MODEL), lambda i: (0, 0, 0))
    return pl.BlockSpec((1, 6, D_MODEL), lambda i: (i // tiles_per_batch, 0, 0))


def _ab_in_kernel(x_ref, mod_ref, nw_ref, w_ref, snw_ref, sguw_ref, sgub_ref,
                  a_ref, q_ref, k_ref, v_ref, *, tm):
    for rows in _subtiles(tm):
        h = _rms_mod(x_ref[rows, :], nw_ref[...], mod_ref[0, 0:1, :], mod_ref[0, 1:2, :]).astype(BF16)
        uv = jax.nn.gelu(_dot(h, w_ref[:, :O_NA_Q]))
        u = uv[:, :A_WIDTH]
        v = _rms(uv[:, A_WIDTH:], snw_ref[...]).astype(BF16)
        for ci in range(h.shape[0] // A_CHUNK):
            rs = slice(ci * A_CHUNK, (ci + 1) * A_CHUNK)
            out_rows = slice(rows.start + rs.start, rows.start + rs.stop)
            for g in range(A_GROUPS):
                cs = slice(g * LANES, (g + 1) * LANES)
                gate = _dot(sguw_ref[g], v[rs, cs]) + sgub_ref[g]
                a_ref[out_rows, cs] = (u[rs, cs] * gate).astype(BF16)
        qkv = _dot(h, w_ref[:, O_NA_Q:])
        q_ref[rows, :] = (qkv[:, :NA_WIDTH] * NA_HEAD_DIM ** -0.5).astype(BF16)
        k_ref[rows, :] = qkv[:, NA_WIDTH:2 * NA_WIDTH].astype(BF16)
        v_ref[rows, :] = qkv[:, 2 * NA_WIDTH:].astype(BF16)


def _ab_in(x2, mod, tiles_per_batch, tm, nw, w_bf, snw, sguw_bf, sgub_exp):
    m = x2.shape[0]
    n_in = w_bf.shape[1]
    row = lambda w: pl.BlockSpec((tm, w), lambda i: (i, 0))
    out = jax.ShapeDtypeStruct((m, A_WIDTH), BF16)
    return pl.pallas_call(
        functools.partial(_ab_in_kernel, tm=tm),
        out_shape=(out, out, out, out),
        grid=(m // tm,),
        in_specs=[row(D_MODEL), _mod_spec(tiles_per_batch), _resident((1, D_MODEL)),
                  _resident((D_MODEL, n_in)), _resident((1, A_WIDTH)),
                  _resident((A_GROUPS, A_CHUNK, A_CHUNK)), _resident((A_GROUPS, A_CHUNK, LANES))],
        out_specs=(row(A_WIDTH),) * 4,
        compiler_params=_cparams("parallel"),
        name="ab_in",
    )(x2, mod, nw, w_bf, snw, sguw_bf, sgub_exp)


def _na_bias_table(rel_bias):
    w, nrow = GRID_W, 2 * NA_WIN_ROWS - 1
    col = np.arange(w)
    col_start = np.clip(col - NA_WIN_COLS // 2, 0, w - NA_WIN_COLS)
    col_mask = (col[None, :] >= col_start[:, None]) & (col[None, :] < col_start[:, None] + NA_WIN_COLS)
    lo = w - NA_WIN_COLS
    r_ext = jnp.pad(rel_bias, ((0, 0), (0, 0), (lo, 2 * w - lo - (2 * NA_WIN_COLS - 1))))
    flat = jnp.tile(r_ext, (1, 1, w))[:, :, :w * (2 * w - 1)]
    toe = flat.reshape(NA_HEADS, nrow, w, 2 * w - 1)[:, :, :, w - 1:]
    toe = jnp.where(col_mask, toe, NEG_INF).reshape(NA_HEADS // 2, 2, nrow, w, w)
    strips = [jnp.concatenate([toe[:, :, j - o + NA_WIN_ROWS - 1] for j in range(NA_WIN_ROWS)], axis=-1)
              for o in range(NA_WIN_ROWS)]
    return jnp.stack([s.reshape(NA_HEADS // 2, 2 * w, NA_WIN_ROWS * w) for s in strips], axis=1)


def _stack_heads(qb):
    first = lax.broadcasted_iota(jnp.int32, (1, LANES), 1) < NA_HEAD_DIM
    zero = jnp.zeros_like(qb)
    return jnp.concatenate([jnp.where(first, qb, zero), jnp.where(first, zero, qb)], axis=0)


def _unstack_heads(r):
    n = r.shape[0] // 2
    first = lax.broadcasted_iota(jnp.int32, (1, LANES), 1) < NA_HEAD_DIM
    return jnp.where(first, r[:n], r[n:])


def _na_kernel(q_ref, k_ref, v_ref, kc_ref, vc_ref, tab_ref, o_ref, kt_scr, kct_scr, *, rows):
    t = rows * GRID_W
    kt_scr[0] = k_ref[0].T
    kt_scr[1, :, :t - GRID_W] = k_ref[0, GRID_W:, :].T
    kct_scr[...] = kc_ref[0].T
    kct = kct_scr[...]
    vc = vc_ref[0]
    win = NA_WIN_ROWS * GRID_W

    def window(r):
        return min(max(r - NA_WIN_ROWS // 2, 0), rows - NA_WIN_ROWS)

    def scores(r):
        start = window(r)
        odd = start % 2
        kt0 = (start - odd) * GRID_W
        qs = _stack_heads(q_ref[0, r * GRID_W:(r + 1) * GRID_W, :])
        return _dot(qs, kt_scr[odd, :, kt0:kt0 + win]) + tab_ref[0, r - start], _dot(qs, kct)

    def softmax(s):
        s_nb, s_cx = s
        m = jnp.maximum(jnp.max(s_nb, axis=-1, keepdims=True), jnp.max(s_cx, axis=-1, keepdims=True))
        e_nb = jnp.exp(s_nb - m)
        e_cx = jnp.exp(s_cx - m)
        l = jnp.sum(e_nb, axis=-1, keepdims=True) + jnp.sum(e_cx, axis=-1, keepdims=True)
        return e_nb.astype(BF16), e_cx.astype(BF16), l

    def values(r, p):
        e_nb, e_cx, l = p
        k0 = window(r) * GRID_W
        acc = _dot(e_nb, v_ref[0, k0:k0 + win, :]) + _dot(e_cx, vc)
        o_ref[0, r * GRID_W:(r + 1) * GRID_W, :] = _unstack_heads(acc / l).astype(BF16)

    lag = NA_STAGE_LAG
    s_vals, p_vals = {}, {}
    for r in range(rows + 2 * lag):
        if 0 <= r - 2 * lag < rows:
            values(r - 2 * lag, p_vals.pop(r - 2 * lag))
        if 0 <= r - lag < rows:
            p_vals[r - lag] = softmax(s_vals.pop(r - lag))
        if r < rows:
            s_vals[r] = scores(r)


def _na(q, k, v, kc, vc, table):
    b, t, _ = q.shape
    rows = t // GRID_W
    n_keys = NA_WIN_ROWS * GRID_W + CTX_LEN
    lat = pl.BlockSpec((1, t, LANES), lambda i, j: (i, 0, j))
    ctx = pl.BlockSpec((1, CTX_LEN, LANES), lambda i, j: (i, 0, j))
    return pl.pallas_call(
        functools.partial(_na_kernel, rows=rows),
        out_shape=jax.ShapeDtypeStruct((b, t, NA_WIDTH), BF16),
        grid=(b, NA_HEADS // 2),
        in_specs=[lat, lat, lat, ctx, ctx,
                  pl.BlockSpec((1, NA_WIN_ROWS, 2 * GRID_W, NA_WIN_ROWS * GRID_W), lambda i, j: (j, 0, 0, 0))],
        out_specs=lat,
        scratch_shapes=[pltpu.VMEM((2, LANES, t), BF16), pltpu.VMEM((LANES, CTX_LEN), BF16)],
        compiler_params=_cparams("parallel", "parallel"),
        name="na_attn",
    )(q, k, v, kc, vc, table)


def _ctx_attn_kernel(q_ref, k_ref, v_ref, o_ref):
    s = _dot_nt(_stack_heads(q_ref[0]), k_ref[0])
    e = jnp.exp(s - jnp.max(s, axis=-1, keepdims=True))
    acc = _dot(e.astype(BF16), v_ref[0])
    o_ref[0] = _unstack_heads(acc / jnp.sum(e, axis=-1, keepdims=True)).astype(BF16)


def _ctx_attn(q, k, v):
    b = q.shape[0]
    spec = pl.BlockSpec((1, CTX_LEN, LANES), lambda i, j: (i, 0, j))
    return pl.pallas_call(
        _ctx_attn_kernel,
        out_shape=jax.ShapeDtypeStruct((b, CTX_LEN, NA_WIDTH), BF16),
        grid=(b, NA_HEADS // 2),
        in_specs=[spec, spec, spec],
        out_specs=spec,
        compiler_params=_cparams("parallel", "parallel"),
        name="ctx_attn",
    )(q, k, v)


def _out_ffn_kernel(a_ref, b_ref, x_ref, mod_ref, wo_ref, nw_ref, wg_ref, wu_ref, wd_ref, fnw_ref, o_ref,
                    *, final):
    half = wo_ref.shape[0] // 2
    piece = FFN_HIDDEN // FFN_SPLIT
    for rows in _subtiles(x_ref.shape[0]):
        y = _dot(a_ref[rows, :], wo_ref[:half, :]) + _dot(b_ref[rows, :], wo_ref[half:, :])
        x1 = x_ref[rows, :] + mod_ref[0, 2:3, :] * y
        h = _rms_mod(x1, nw_ref[...], mod_ref[0, 3:4, :], mod_ref[0, 4:5, :]).astype(BF16)
        acc = None
        for j in range(FFN_SPLIT):
            cs = slice(j * piece, (j + 1) * piece)
            g = _dot(h, wg_ref[:, cs])
            act = (g * jax.nn.sigmoid(g) * _dot(h, wu_ref[:, cs])).astype(BF16)
            part = _dot(act, wd_ref[cs, :])
            acc = part if acc is None else acc + part
        x2 = x1 + mod_ref[0, 5:6, :] * acc
        o_ref[rows, :] = _rms(x2, fnw_ref[...]) if final else x2


def _out_ffn(a, b, x2, mod, tiles_per_batch, tm, wo_bf, nw, wg_bf, wu_bf, wd_bf, fnw, final):
    m = x2.shape[0]
    row = lambda w: pl.BlockSpec((tm, w), lambda i: (i, 0))
    return pl.pallas_call(
        functools.partial(_out_ffn_kernel, final=final),
        out_shape=jax.ShapeDtypeStruct((m, D_MODEL), F32),
        grid=(m // tm,),
        in_specs=[row(a.shape[1]), row(b.shape[1]), row(D_MODEL), _mod_spec(tiles_per_batch),
                  _resident(wo_bf.shape), _resident((1, D_MODEL)), _resident(wg_bf.shape),
                  _resident(wu_bf.shape), _resident(wd_bf.shape), _resident((1, D_MODEL))],
        out_specs=row(D_MODEL),
        compiler_params=_cparams("parallel"),
        name="out_ffn_final" if final else "out_ffn",
    )(a, b, x2, mod, wo_bf, nw, wg_bf, wu_bf, wd_bf, fnw)


def _rope_tables(t):
    half = GLA_DK // 4
    inv_freq = ROPE_BASE ** (-np.arange(half, dtype=np.float64) / half)
    pos = np.arange(t)
    ang_r = (pos // GRID_W)[:, None] * inv_freq[None, :]
    ang_c = (pos % GRID_W)[:, None] * inv_freq[None, :]
    cos = np.concatenate([np.cos(ang_r)] * 2 + [np.cos(ang_c)] * 2, axis=1)
    sin = np.concatenate([-np.sin(ang_r), np.sin(ang_r), -np.sin(ang_c), np.sin(ang_c)], axis=1)
    return (jnp.asarray(np.tile(cos, (1, 2)), F32), jnp.asarray(np.tile(sin, (1, 2)), F32))


def _log_sigmoid(z):
    return jnp.minimum(z, 0.0) - jnp.log1p(jnp.exp(-jnp.abs(z)))


def _cd_in_kernel(x_ref, mod_ref, nw_ref, w_ref, wa_ref, wdec_ref, bdec_ref, cos_ref, sin_ref,
                  f_ref, qk_ref, v_ref, g_ref, la_ref, *, rope):
    for rows in _subtiles(x_ref.shape[0]):
        h = _rms_mod(x_ref[rows, :], nw_ref[...], mod_ref[0, 0:1, :], mod_ref[0, 1:2, :]).astype(BF16)
        f = _dot(h, w_ref[:, :O_Q])
        for a in range(f.shape[0] // GRID_W):
            r0 = (rows.start // GRID_W + a) * FFT_PITCH
            f_ref[r0:r0 + GRID_W, :] = f[a * GRID_W:(a + 1) * GRID_W]
            f_ref[r0 + GRID_W:r0 + FFT_PITCH, :] = jnp.zeros((FFT_PITCH - GRID_W, f.shape[1]), F32)
        qk = _dot(h, w_ref[:, O_Q:O_V])
        if rope:
            reps = qk.shape[1] // LANES
            cos = jnp.concatenate([cos_ref[rows, :]] * reps, axis=1)
            sin = jnp.concatenate([sin_ref[rows, :]] * reps, axis=1)
            w = qk.shape[1]
            q16 = GLA_DK // 4
            first = (lax.broadcasted_iota(jnp.int32, (1, w), 1) % (2 * q16)) < q16
            partner = jnp.where(first, pltpu.roll(qk, w - q16, 1), pltpu.roll(qk, q16, 1))
            qk = qk * cos + partner * sin
        is_q = lax.broadcasted_iota(jnp.int32, (1, qk.shape[1]), 1) < GLA_KEY_WIDTH
        qk_ref[rows, :] = jnp.where(is_q, qk * GLA_DK ** -0.5, qk)
        v_ref[rows, :] = _dot(h, w_ref[:, O_V:O_G]).astype(BF16)
        g_ref[rows, :] = _dot(h, w_ref[:, O_G:O_A])
        a = _dot(h, wa_ref[...]).astype(BF16)
        la_ref[rows, :] = _log_sigmoid(_dot(a, wdec_ref[...]) + bdec_ref[...]) * (1.0 / GLA_GATE_TEMP)


def _cd_in(x2, mod, tiles_per_batch, tm, nw, w_bf, wa_bf, wdec_bf, bdec, cos, sin, rope):
    m = x2.shape[0]
    row = lambda w: pl.BlockSpec((tm, w), lambda i: (i, 0))
    if rope:
        tab = pl.BlockSpec((tm, LANES), lambda i: (i % tiles_per_batch, 0))
    else:
        tab = pl.BlockSpec((tm, LANES), lambda i: (0, 0))
    w512 = 2 * GLA_KEY_WIDTH
    pad = lambda n: n // GRID_W * FFT_PITCH
    outs = (jax.ShapeDtypeStruct((pad(m), FN_WIDTH), F32), jax.ShapeDtypeStruct((m, w512), F32),
            jax.ShapeDtypeStruct((m, GLA_VAL_WIDTH), BF16), jax.ShapeDtypeStruct((m, GLA_VAL_WIDTH), F32),
            jax.ShapeDtypeStruct((m, w512), F32))
    return pl.pallas_call(
        functools.partial(_cd_in_kernel, rope=rope),
        out_shape=outs,
        grid=(m // tm,),
        in_specs=[row(D_MODEL), _mod_spec(tiles_per_batch), _resident((1, D_MODEL)),
                  _resident(w_bf.shape), _resident(wa_bf.shape), _resident(wdec_bf.shape),
                  _resident((1, w512)), tab, tab],
        out_specs=(pl.BlockSpec((pad(tm), FN_WIDTH), lambda i: (i, 0)),
                   row(w512), row(GLA_VAL_WIDTH), row(GLA_VAL_WIDTH), row(w512)),
        compiler_params=_cparams("parallel"),
        name="cd_in_rope" if rope else "cd_in_ctx",
    )(x2, mod, nw, w_bf, wa_bf, wdec_bf, bdec, cos, sin)


def _fft_tables(t):
    n1 = GRID_W
    assert t == n1 * n1
    a = np.arange(n1)
    ang1 = 2 * np.pi * np.outer(a, a) / n1
    cs = np.concatenate([np.cos(ang1), -np.sin(ang1)], axis=0)
    kap = a[:, None, None] + n1 * a[None, :, None]
    ang2 = 2 * np.pi * kap * a[None, None, :] / t
    gr, gi = np.cos(ang2), -np.sin(ang2)
    g = np.concatenate([np.concatenate([gr, -gi], axis=2), np.concatenate([gi, gr], axis=2)], axis=1)
    c = np.arange(FN_GROUP_CH)
    ang3 = 2 * np.pi * np.outer(c, c) / FN_GROUP_CH
    norm = 1.0 / np.sqrt(t * FN_GROUP_CH)
    return (jnp.asarray(cs, F32), jnp.asarray(g, F32),
            jnp.asarray(np.cos(ang3) * norm, F32), jnp.asarray(np.sin(ang3) * norm, F32))


def _fft_kernel(x_ref, cs_ref, g_ref, cc_ref, sc_ref, o_ref, p_scr, zr_scr, zi_scr):
    n1 = GRID_W
    cs = cs_ref[...].astype(BF16)

    def stage1(b, carry):
        xb = x_ref[0, pl.ds(b, n1, stride=FFT_PITCH), :].astype(BF16)
        p_scr[pl.ds(pl.multiple_of(b * FFT_PITCH2, 8), 2 * n1), :] = _dot(cs, xb)
        return carry

    lax.fori_loop(0, n1, stage1, 0, unroll=FFT_UNROLL)

    def stage2(k1, carry):
        pr = p_scr[pl.ds(k1, n1, stride=FFT_PITCH2), :]
        pi = p_scr[pl.ds(n1 + k1, n1, stride=FFT_PITCH2), :]
        z = _dot(g_ref[k1].astype(BF16), jnp.concatenate([pr, pi], axis=0).astype(BF16))
        rows = pl.ds(pl.multiple_of(k1 * FFT_PITCH, 8), n1)
        zr_scr[rows, :] = z[:n1]
        zi_scr[rows, :] = z[n1:]
        return carry

    lax.fori_loop(0, n1, stage2, 0, unroll=FFT_UNROLL)

    ccsc = jnp.concatenate([cc_ref[...], sc_ref[...]], axis=0).astype(BF16)

    def stage3(j, carry):
        parts = []
        for u in range(FFT_UNROLL):
            k2 = j * FFT_UNROLL + u
            parts.append(jnp.concatenate([zr_scr[pl.ds(k2, n1, stride=FFT_PITCH), :],
                                          zi_scr[pl.ds(k2, n1, stride=FFT_PITCH), :]], axis=1))
        out = _dot(jnp.concatenate(parts, axis=0).astype(BF16), ccsc)
        o_ref[0, pl.ds(pl.multiple_of(j * FFT_UNROLL * n1, FFT_UNROLL * n1), FFT_UNROLL * n1), :] = out.astype(BF16)
        return carry

    lax.fori_loop(0, n1 // FFT_UNROLL, stage3, 0)


def _fourier(f_padded, tables):
    b, tp, w = f_padded.shape
    t = tp // FFT_PITCH * GRID_W
    cw = FN_GROUP_CH
    cs, g, cc, sc = tables
    return pl.pallas_call(
        _fft_kernel,
        out_shape=jax.ShapeDtypeStruct((b, t, w), BF16),
        grid=(b, w // cw),
        in_specs=[pl.BlockSpec((1, tp, cw), lambda i, j: (i, 0, j)),
                  _resident(cs.shape), _resident(g.shape), _resident(cc.shape), _resident(sc.shape)],
        out_specs=pl.BlockSpec((1, t, cw), lambda i, j: (i, 0, j)),
        scratch_shapes=[pltpu.VMEM((GRID_W * FFT_PITCH2, cw), F32), pltpu.VMEM((tp, cw), F32),
                        pltpu.VMEM((tp, cw), F32)],
        compiler_params=_cparams("parallel", "parallel"),
        name="fourier_mix",
    )(f_padded, cs, g, cc, sc)


def _block_diag(x, block):
    head = lax.broadcasted_iota(jnp.int32, (1, x.shape[1]), 1) // block
    return jnp.concatenate([jnp.where(head == h, x, jnp.zeros_like(x)) for h in range(GLA_HEADS)], axis=0)


def _cum_rows(x, reverse):
    n, w = x.shape
    row = lax.broadcasted_iota(jnp.int32, (n, 1), 0)
    s = 1
    while s < n:
        if s < 8:
            if reverse:
                shifted = jnp.where(row < n - s, pltpu.roll(x, n - s, 0), 0.0)
            else:
                shifted = jnp.where(row >= s, pltpu.roll(x, s, 0), 0.0)
        else:
            pad = jnp.zeros((s, w), F32)
            shifted = jnp.concatenate([x[s:], pad] if reverse else [pad, x[:n - s]], axis=0)
        x = x + shifted
        s *= 2
    return x


def _gla_prep(q, k, la, reverse, want_out):
    c = GLA_CHUNK
    b = _cum_rows(la, reverse)
    b_last = b[0:1, :] if reverse else b[c - 1:c, :]
    kd = (k * jnp.exp(b_last - b)).astype(BF16)
    decay = jnp.exp(b_last)
    if not want_out:
        return kd, decay
    qe = (q * jnp.exp(b)).astype(BF16)
    ke = _block_diag(k * jnp.exp(-b), GLA_DK).astype(BF16)
    l = lax.broadcasted_iota(jnp.int32, (c, GLA_HEADS * c), 0)
    m = lax.broadcasted_iota(jnp.int32, (c, GLA_HEADS * c), 1) % c
    att = jnp.where((l <= m) if reverse else (l >= m), _dot_nt(qe, ke), 0.0).astype(BF16)
    return kd, decay, qe, att


def _gla_state_step(kd, decay, v, state):
    v_rows = jnp.concatenate([v[:, h * GLA_DV:(h + 1) * GLA_DV] for h in range(GLA_HEADS)], axis=0)
    return state * decay + _dot_tn(v_rows, _block_diag(kd, GLA_DK))


def _gla_apply(prep, v, state):
    c = GLA_CHUNK
    kd, decay, qe, att = prep
    inter = _dot_nt(_block_diag(qe, GLA_DK), state.astype(BF16))
    out = _dot(att, _block_diag(v, GLA_DV)) + jnp.concatenate(
        [inter[h * c:(h + 1) * c] for h in range(GLA_HEADS)], axis=1)
    return out, _gla_state_step(kd, decay, v, state)


def _gla_kernel(qkf_ref, vf_ref, laf_ref, gf_ref, qkb_ref, vb_ref, lab_ref, gb_ref,
                kc_ref, vc_ref, lac_ref, hw_ref, o_ref, sf_scr, sb_scr, acc_scr, *, n_tiles, tile):
    c = GLA_CHUNK
    i = pl.program_id(1)
    kw = GLA_KEY_WIDTH
    n_chunks = tile // c

    @pl.when(i == 0)
    def _():
        n_ctx = kc_ref.shape[1] // c
        sf = jnp.zeros((GLA_DV, kw), F32)
        sb = jnp.zeros((GLA_DV, kw), F32)
        for n in range(n_ctx):
            rf = slice(n * c, (n + 1) * c)
            rb = slice((n_ctx - 1 - n) * c, (n_ctx - n) * c)
            sf = _gla_state_step(*_gla_prep(None, kc_ref[0, rf, :], lac_ref[0, rf, :kw], False, False),
                                 vc_ref[0, rf, :], sf)
            sb = _gla_state_step(*_gla_prep(None, kc_ref[0, rb, :], lac_ref[0, rb, kw:], True, False),
                                 vc_ref[0, rb, :], sb)
        sf_scr[...] = sf
        sb_scr[...] = sb

    hw = hw_ref[...]

    def finish(o, g):
        parts = []
        for h in range(GLA_HEADS):
            oh = o[:, h * GLA_DV:(h + 1) * GLA_DV]
            parts.append(oh * lax.rsqrt(jnp.mean(oh * oh, axis=-1, keepdims=True) + RMS_EPS))
        return (jnp.concatenate(parts, axis=1) * hw * (g * jax.nn.sigmoid(g))).astype(BF16)

    def sweep(second_pass):
        def emit(o, g_ref, rloc, row0):
            rows = pl.ds(pl.multiple_of(row0, c), c)
            if second_pass:
                o_ref[0, rows, :] = finish(o + acc_scr[rows, :], g_ref[0, rloc, :])
            else:
                acc_scr[rows, :] = o

        def rows_of(n):
            nb = n_chunks - 1 - n
            return slice(n * c, (n + 1) * c), slice(nb * c, (nb + 1) * c)

        def prep(n):
            rf, rb = rows_of(n)
            qkf, qkb = qkf_ref[0, rf, :], qkb_ref[0, rb, :]
            return (_gla_prep(qkf[:, :kw], qkf[:, kw:], laf_ref[0, rf, :], False, True),
                    _gla_prep(qkb[:, :kw], qkb[:, kw:], lab_ref[0, rb, :], True, True))

        sf, sb = sf_scr[...], sb_scr[...]
        ready = {n: prep(n) for n in range(min(GLA_PREP_AHEAD, n_chunks))}
        for n in range(n_chunks):
            rf, rb = rows_of(n)
            pf, pb = ready.pop(n)
            of, sf = _gla_apply(pf, vf_ref[0, rf, :], sf)
            emit(of, gf_ref, rf, i * tile + rf.start)
            ob, sb = _gla_apply(pb, vb_ref[0, rb, :], sb)
            emit(ob, gb_ref, rb, (n_tiles - 1 - i) * tile + rb.start)
            if n + GLA_PREP_AHEAD < n_chunks:
                ready[n + GLA_PREP_AHEAD] = prep(n + GLA_PREP_AHEAD)
        sf_scr[...] = sf
        sb_scr[...] = sb

    @pl.when(i < n_tiles // 2)
    def _():
        sweep(False)

    @pl.when(i >= n_tiles // 2)
    def _():
        sweep(True)


def _gla(qk, v, la, g, qk_c, v_c, la_c, head_w):
    b, t, _ = qk.shape
    tile = GLA_TILE
    n_tiles = t // tile
    kw, vw = GLA_KEY_WIDTH, GLA_VAL_WIDTH
    fwd = lambda w, col: pl.BlockSpec((1, tile, w), lambda bi, i: (bi, i, col))
    bwd = lambda w, col: pl.BlockSpec((1, tile, w), lambda bi, i: (bi, n_tiles - 1 - i, col))
    ctx = lambda w: pl.BlockSpec((1, CTX_LEN, w), lambda bi, i: (bi, 0, 0))
    return pl.pallas_call(
        functools.partial(_gla_kernel, n_tiles=n_tiles, tile=tile),
        out_shape=jax.ShapeDtypeStruct((b, t, vw), BF16),
        grid=(b, n_tiles),
        in_specs=[fwd(2 * kw, 0), fwd(vw, 0), fwd(kw, 0), fwd(vw, 0),
                  bwd(2 * kw, 0), bwd(vw, 0), bwd(kw, 1), bwd(vw, 0),
                  pl.BlockSpec((1, CTX_LEN, kw), lambda bi, i: (bi, 0, 1)), ctx(vw), ctx(2 * kw),
                  pl.BlockSpec((1, vw), lambda bi, i: (0, 0))],
        out_specs=pl.BlockSpec((1, t, vw), lambda bi, i: (bi, 0, 0)),
        scratch_shapes=[pltpu.VMEM((GLA_DV, kw), F32), pltpu.VMEM((GLA_DV, kw), F32), pltpu.VMEM((t, vw), F32)],
        compiler_params=_cparams("parallel", "arbitrary"),
        name="gla",
    )(qk, v, la, g, qk, v, la, g, qk_c, v_c, la_c, head_w)


def kernel(x, c, ctx, c_ctx, ada_w, ada_b, norm_mix_w, norm_ffn_w, ffn_w_gate, ffn_w_up, ffn_w_down,
           ab_w_in, ab_w_out, ab_sgu_norm_w, ab_sgu_w, ab_sgu_b, ab_rel_bias,
           cd_w_in, cd_w_out, cd_decay_w_fwd, cd_decay_b_fwd, cd_decay_w_bwd, cd_decay_b_bwd, cd_head_norm_w,
           final_norm_w):
    bsz, t, d = x.shape
    n_ctx = ctx.shape[1]
    tpb = t // ROW_TILE
    x2 = x.reshape(bsz * t, d)
    ctx2 = ctx.reshape(bsz * n_ctx, d)
    row = lambda w: w.reshape(1, -1)

    cc = jnp.concatenate([c, c_ctx[None, :], jnp.zeros((16 - bsz - 1, d), F32)], axis=0)
    mod = _ada(cc, ada_w, ada_b)
    mod_x = [mod[i, :bsz].reshape(bsz, 6, d) for i in range(2)]
    mod_c = [mod[i, bsz:bsz + 1].reshape(1, 6, d) for i in range(2)]
    ffn = [(ffn_w_gate[i].astype(BF16), ffn_w_up[i].astype(BF16), ffn_w_down[i].astype(BF16)) for i in range(2)]

    w_in = ab_w_in[0].astype(BF16)
    sgu = (row(ab_sgu_norm_w[0]), ab_sgu_w[0].astype(BF16),
           jnp.broadcast_to(ab_sgu_b[0][:, :, None], (A_GROUPS, A_CHUNK, LANES)))
    a_l, q_l, k_l, v_l = _ab_in(x2, mod_x[0], tpb, ROW_TILE, row(norm_mix_w[0]), w_in, *sgu)
    a_c, q_c, k_c, v_c = _ab_in(ctx2, mod_c[0], None, n_ctx, row(norm_mix_w[0]), w_in, *sgu)
    seq = lambda z, n: z.reshape(bsz, n, z.shape[-1])
    b_l = _na(seq(q_l, t), seq(k_l, t), seq(v_l, t), seq(k_c, n_ctx), seq(v_c, n_ctx), _na_bias_table(ab_rel_bias[0]))
    b_c = _ctx_attn(seq(q_c, n_ctx), seq(k_c, n_ctx), seq(v_c, n_ctx))
    wo = ab_w_out[0].astype(BF16)
    x2 = _out_ffn(a_l, b_l.reshape(bsz * t, -1), x2, mod_x[0], tpb, ROW_TILE, wo, row(norm_ffn_w[0]), *ffn[0],
                  row(final_norm_w), False)
    ctx2 = _out_ffn(a_c, b_c.reshape(bsz * n_ctx, -1), ctx2, mod_c[0], None, n_ctx, wo, row(norm_ffn_w[0]),
                    *ffn[0], row(final_norm_w), False)

    w_in = cd_w_in[0]
    w_main = w_in[:, :O_A].astype(BF16)
    w_a = jnp.pad(w_in[:, O_A:], ((0, 0), (0, LANES - 2 * GLA_LOW_RANK))).astype(BF16)
    w_dec = jnp.zeros((LANES, 2 * GLA_KEY_WIDTH), F32)
    w_dec = w_dec.at[:GLA_LOW_RANK, :GLA_KEY_WIDTH].set(cd_decay_w_fwd[0])
    w_dec = w_dec.at[GLA_LOW_RANK:2 * GLA_LOW_RANK, GLA_KEY_WIDTH:].set(cd_decay_w_bwd[0]).astype(BF16)
    b_dec = jnp.concatenate([cd_decay_b_fwd[0], cd_decay_b_bwd[0]]).reshape(1, -1)
    cos, sin = _rope_tables(t)
    f_l, qk_l, v_l, g_l, la_l = _cd_in(x2, mod_x[1], tpb, ROW_TILE, row(norm_mix_w[1]), w_main, w_a, w_dec, b_dec,
                                      cos, sin, True)
    _, qk_c, v_c, _, la_c = _cd_in(ctx2, mod_c[1], None, n_ctx, row(norm_mix_w[1]), w_main, w_a, w_dec, b_dec,
                                   cos, sin, False)
    fm = _fourier(f_l.reshape(bsz, -1, FN_WIDTH), _fft_tables(t))
    head_w = jnp.tile(cd_head_norm_w[0], GLA_HEADS).reshape(1, -1)
    go = _gla(seq(qk_l, t), seq(v_l, t), seq(la_l, t), seq(g_l, t),
              seq(qk_c, n_ctx), seq(v_c, n_ctx), seq(la_c, n_ctx), head_w)
    out = _out_ffn(fm.reshape(bsz * t, -1), go.reshape(bsz * t, -1), x2, mod_x[1], tpb, ROW_TILE,
                   cd_w_out[0].astype(BF16), row(norm_ffn_w[1]), *ffn[1], row(final_norm_w), True)
    return out.reshape(bsz, t, d)
```

```python
import functools

import numpy as np
import jax
import jax.numpy as jnp
from jax import lax
from jax.experimental import pallas as pl
from jax.experimental.pallas import tpu as pltpu

F32 = jnp.float32
BF16 = jnp.bfloat16

D_MODEL = 1024
CTX_LEN = 256
GRID_W = 64
A_WIDTH = 512
A_GROUPS = 4
A_CHUNK = 128
NA_HEADS = 8
NA_HEAD_DIM = 64
NA_WIDTH = 512
NA_WIN_ROWS = 8
NA_WIN_COLS = 16
O_NA_Q = 2 * A_WIDTH
FN_WIDTH = 512
FN_GROUP_CH = 128
GLA_HEADS = 4
GLA_VAL_WIDTH = 512
GLA_KEY_WIDTH = 256
GLA_DK = 64
GLA_DV = 128
GLA_LOW_RANK = 16
GLA_GATE_TEMP = 16.0
GLA_CHUNK = 64
O_Q = FN_WIDTH
O_V = O_Q + 2 * GLA_KEY_WIDTH
O_G = O_V + GLA_VAL_WIDTH
O_A = O_G + GLA_VAL_WIDTH
FFN_HIDDEN = 2816
ROPE_BASE = 10000.0
RMS_EPS = 1e-6
NEG_INF = -1e30

LANES = 128
VMEM_LIMIT_BYTES = 56 * 2 ** 20

ROW_TILE = 512
ROW_SUBTILE = 256
FFN_SPLIT = 2
GLA_TILE = 512
NA_STAGE_LAG = 2
FFT_UNROLL = 32
FFT_PITCH = GRID_W + 8
FFT_PITCH2 = 2 * GRID_W + 8
GLA_PREP_AHEAD = 2


def _cparams(*sem):
    return pltpu.CompilerParams(dimension_semantics=sem, vmem_limit_bytes=VMEM_LIMIT_BYTES)


def _resident(shape):
    nd = len(shape)
    return pl.BlockSpec(shape, lambda *_: (0,) * nd, pipeline_mode=pl.Buffered(1))


def _dot(a, b):
    return jnp.dot(a, b, preferred_element_type=F32)


def _dot_nt(a, b):
    return lax.dot_general(a, b, (((1,), (1,)), ((), ())), preferred_element_type=F32)


def _dot_tn(a, b):
    return lax.dot_general(a, b, (((0,), (0,)), ((), ())), preferred_element_type=F32)


def _subtiles(n_rows):
    step = min(ROW_SUBTILE, n_rows)
    return [slice(r, r + step) for r in range(0, n_rows, step)]


def _rms(x, w):
    return x * lax.rsqrt(jnp.mean(x * x, axis=-1, keepdims=True) + RMS_EPS) * w


def _rms_mod(x, w, shift, scale):
    return _rms(x, w) * (1.0 + scale) + shift


def _ada_kernel(c_ref, w_ref, b_ref, o_ref):
    c = c_ref[...]
    s = c * jax.nn.sigmoid(c)
    o_ref[0] = jnp.dot(s, w_ref[0], preferred_element_type=F32, precision=lax.Precision.HIGHEST) + b_ref[0]


def _ada(cc, ada_w, ada_b):
    depth, d, n = ada_w.shape
    r = cc.shape[0]
    tn = 1536
    return pl.pallas_call(
        _ada_kernel,
        out_shape=jax.ShapeDtypeStruct((depth, r, n), F32),
        grid=(depth, n // tn),
        in_specs=[pl.BlockSpec((r, d), lambda i, j: (0, 0)),
                  pl.BlockSpec((1, d, tn), lambda i, j: (i, 0, j)),
                  pl.BlockSpec((1, 1, tn), lambda i, j: (i, 0, j))],
        out_specs=pl.BlockSpec((1, r, tn), lambda i, j: (i, 0, j)),
        compiler_params=_cparams("arbitrary", "arbitrary"),
        name="ada_mod",
    )(cc, ada_w, ada_b.reshape(depth, 1, n))


def _mod_spec(tiles_per_batch):
    if tiles_per_batch is None:
        return pl.BlockSpec((1, 6, D_MODEL), lambda i: (0, 0, 0))
    return pl.BlockSpec((1, 6, D_MODEL), lambda i: (i // tiles_per_batch, 0, 0))


def _ab_in_kernel(x_ref, mod_ref, nw_ref, w_ref, snw_ref, sguw_ref, sgub_ref,
                  a_ref, q_ref, k_ref, v_ref, *, tm):
    tiles = _subtiles(tm)
    hs = [_rms_mod(x_ref[rows, :], nw_ref[...], mod_ref[0, 0:1, :], mod_ref[0, 1:2, :]).astype(BF16)
          for rows in tiles]
    uvs = [_dot(h, w_ref[:, :O_NA_Q]) for h in hs]
    for rows, h, uv in zip(tiles, hs, uvs):
        qkv = _dot(h, w_ref[:, O_NA_Q:])
        q_ref[rows, :] = (qkv[:, :NA_WIDTH] * NA_HEAD_DIM ** -0.5).astype(BF16)
        k_ref[rows, :] = qkv[:, NA_WIDTH:2 * NA_WIDTH].astype(BF16)
        v_ref[rows, :] = qkv[:, 2 * NA_WIDTH:].astype(BF16)
        uv = jax.nn.gelu(uv)
        u = uv[:, :A_WIDTH]
        v = _rms(uv[:, A_WIDTH:], snw_ref[...]).astype(BF16)
        for ci in range(uv.shape[0] // A_CHUNK):
            rs = slice(ci * A_CHUNK, (ci + 1) * A_CHUNK)
            out_rows = slice(rows.start + rs.start, rows.start + rs.stop)
            for g in range(A_GROUPS):
                cs = slice(g * LANES, (g + 1) * LANES)
                gate = _dot(sguw_ref[g], v[rs, cs]) + sgub_ref[g]
                a_ref[out_rows, cs] = (u[rs, cs] * gate).astype(BF16)


def _ab_in(x2, mod, tiles_per_batch, tm, nw, w_bf, snw, sguw_bf, sgub_exp):
    m = x2.shape[0]
    n_in = w_bf.shape[1]
    row = lambda w: pl.BlockSpec((tm, w), lambda i: (i, 0))
    out = jax.ShapeDtypeStruct((m, A_WIDTH), BF16)
    return pl.pallas_call(
        functools.partial(_ab_in_kernel, tm=tm),
        out_shape=(out, out, out, out),
        grid=(m // tm,),
        in_specs=[row(D_MODEL), _mod_spec(tiles_per_batch), _resident((1, D_MODEL)),
                  _resident((D_MODEL, n_in)), _resident((1, A_WIDTH)),
                  _resident((A_GROUPS, A_CHUNK, A_CHUNK)), _resident((A_GROUPS, A_CHUNK, LANES))],
        out_specs=(row(A_WIDTH),) * 4,
        compiler_params=_cparams("parallel"),
        name="ab_in",
    )(x2, mod, nw, w_bf, snw, sguw_bf, sgub_exp)


def _na_bias_table(rel_bias):
    w, nrow = GRID_W, 2 * NA_WIN_ROWS - 1
    col = np.arange(w)
    col_start = np.clip(col - NA_WIN_COLS // 2, 0, w - NA_WIN_COLS)
    col_mask = (col[None, :] >= col_start[:, None]) & (col[None, :] < col_start[:, None] + NA_WIN_COLS)
    lo = w - NA_WIN_COLS
    r_ext = jnp.pad(rel_bias, ((0, 0), (0, 0), (lo, 2 * w - lo - (2 * NA_WIN_COLS - 1))))
    flat = jnp.tile(r_ext, (1, 1, w))[:, :, :w * (2 * w - 1)]
    toe = flat.reshape(NA_HEADS, nrow, w, 2 * w - 1)[:, :, :, w - 1:]
    toe = jnp.where(col_mask, toe, NEG_INF).reshape(NA_HEADS // 2, 2, nrow, w, w)
    strips = [jnp.concatenate([toe[:, :, j - o + NA_WIN_ROWS - 1] for j in range(NA_WIN_ROWS)], axis=-1)
              for o in range(NA_WIN_ROWS)]
    return jnp.stack([s.reshape(NA_HEADS // 2, 2 * w, NA_WIN_ROWS * w) for s in strips], axis=1)


def _stack_heads(qb):
    first = lax.broadcasted_iota(jnp.int32, (1, LANES), 1) < NA_HEAD_DIM
    zero = jnp.zeros_like(qb)
    return jnp.concatenate([jnp.where(first, qb, zero), jnp.where(first, zero, qb)], axis=0)


def _unstack_heads(r):
    n = r.shape[0] // 2
    first = lax.broadcasted_iota(jnp.int32, (1, LANES), 1) < NA_HEAD_DIM
    return jnp.where(first, r[:n], r[n:])


def _na_kernel(q_ref, k_ref, v_ref, kc_ref, vc_ref, tab_ref, o_ref, kt_scr, kct_scr, *, rows):
    t = rows * GRID_W
    kt_scr[0] = k_ref[0].T
    kt_scr[1, :, :t - GRID_W] = k_ref[0, GRID_W:, :].T
    kct_scr[...] = kc_ref[0].T
    kct = kct_scr[...]
    vc = vc_ref[0]
    win = NA_WIN_ROWS * GRID_W

    def window(r):
        return min(max(r - NA_WIN_ROWS // 2, 0), rows - NA_WIN_ROWS)

    def scores(r):
        start = window(r)
        odd = start % 2
        kt0 = (start - odd) * GRID_W
        qs = _stack_heads(q_ref[0, r * GRID_W:(r + 1) * GRID_W, :])
        return _dot(qs, kt_scr[odd, :, kt0:kt0 + win]) + tab_ref[0, r - start], _dot(qs, kct)

    def softmax(s):
        s_nb, s_cx = s
        m = jnp.maximum(jnp.max(s_nb, axis=-1, keepdims=True), jnp.max(s_cx, axis=-1, keepdims=True))
        e_nb = jnp.exp(s_nb - m)
        e_cx = jnp.exp(s_cx - m)
        l = jnp.sum(e_nb, axis=-1, keepdims=True) + jnp.sum(e_cx, axis=-1, keepdims=True)
        return e_nb.astype(BF16), e_cx.astype(BF16), l

    def values(r, p):
        e_nb, e_cx, l = p
        k0 = window(r) * GRID_W
        acc = _dot(e_nb, v_ref[0, k0:k0 + win, :]) + _dot(e_cx, vc)
        o_ref[0, r * GRID_W:(r + 1) * GRID_W, :] = _unstack_heads(acc / l).astype(BF16)

    lag = NA_STAGE_LAG
    s_vals, p_vals = {}, {}
    for r in range(rows + 2 * lag):
        if 0 <= r - 2 * lag < rows:
            values(r - 2 * lag, p_vals.pop(r - 2 * lag))
        if 0 <= r - lag < rows:
            p_vals[r - lag] = softmax(s_vals.pop(r - lag))
        if r < rows:
            s_vals[r] = scores(r)


def _na(q, k, v, kc, vc, table):
    b, t, _ = q.shape
    rows = t // GRID_W
    n_keys = NA_WIN_ROWS * GRID_W + CTX_LEN
    lat = pl.BlockSpec((1, t, LANES), lambda i, j: (i, 0, j))
    ctx = pl.BlockSpec((1, CTX_LEN, LANES), lambda i, j: (i, 0, j))
    return pl.pallas_call(
        functools.partial(_na_kernel, rows=rows),
        out_shape=jax.ShapeDtypeStruct((b, t, NA_WIDTH), BF16),
        grid=(b, NA_HEADS // 2),
        in_specs=[lat, lat, lat, ctx, ctx,
                  pl.BlockSpec((1, NA_WIN_ROWS, 2 * GRID_W, NA_WIN_ROWS * GRID_W), lambda i, j: (j, 0, 0, 0))],
        out_specs=lat,
        scratch_shapes=[pltpu.VMEM((2, LANES, t), BF16), pltpu.VMEM((LANES, CTX_LEN), BF16)],
        compiler_params=_cparams("parallel", "parallel"),
        name="na_attn",
    )(q, k, v, kc, vc, table)


def _ctx_attn_kernel(q_ref, k_ref, v_ref, o_ref):
    s = _dot_nt(_stack_heads(q_ref[0]), k_ref[0])
    e = jnp.exp(s - jnp.max(s, axis=-1, keepdims=True))
    acc = _dot(e.astype(BF16), v_ref[0])
    o_ref[0] = _unstack_heads(acc / jnp.sum(e, axis=-1, keepdims=True)).astype(BF16)


def _ctx_attn(q, k, v):
    b = q.shape[0]
    spec = pl.BlockSpec((1, CTX_LEN, LANES), lambda i, j: (i, 0, j))
    return pl.pallas_call(
        _ctx_attn_kernel,
        out_shape=jax.ShapeDtypeStruct((b, CTX_LEN, NA_WIDTH), BF16),
        grid=(b, NA_HEADS // 2),
        in_specs=[spec, spec, spec],
        out_specs=spec,
        compiler_params=_cparams("parallel", "parallel"),
        name="ctx_attn",
    )(q, k, v)


def _out_ffn_kernel(a_ref, b_ref, x_ref, mod_ref, wo_ref, nw_ref, wg_ref, wu_ref, wd_ref, fnw_ref, o_ref,
                    *, final):
    half = wo_ref.shape[0] // 2
    piece = FFN_HIDDEN // FFN_SPLIT
    tiles = _subtiles(x_ref.shape[0])
    ys = [_dot(a_ref[rows, :], wo_ref[:half, :]) + _dot(b_ref[rows, :], wo_ref[half:, :]) for rows in tiles]
    x1s = [x_ref[rows, :] + mod_ref[0, 2:3, :] * y for rows, y in zip(tiles, ys)]
    hs = [_rms_mod(x1, nw_ref[...], mod_ref[0, 3:4, :], mod_ref[0, 4:5, :]).astype(BF16) for x1 in x1s]
    accs = [None] * len(tiles)
    for j in range(FFN_SPLIT):
        cs = slice(j * piece, (j + 1) * piece)
        gs = [_dot(h, wg_ref[:, cs]) for h in hs]
        acts = [(g * jax.nn.sigmoid(g) * _dot(h, wu_ref[:, cs])).astype(BF16) for g, h in zip(gs, hs)]
        parts = [_dot(act, wd_ref[cs, :]) for act in acts]
        accs = [part if acc is None else acc + part for acc, part in zip(accs, parts)]
    for rows, x1, acc in zip(tiles, x1s, accs):
        x2 = x1 + mod_ref[0, 5:6, :] * acc
        o_ref[rows, :] = _rms(x2, fnw_ref[...]) if final else x2


def _out_ffn(a, b, x2, mod, tiles_per_batch, tm, wo_bf, nw, wg_bf, wu_bf, wd_bf, fnw, final):
    m = x2.shape[0]
    row = lambda w: pl.BlockSpec((tm, w), lambda i: (i, 0))
    return pl.pallas_call(
        functools.partial(_out_ffn_kernel, final=final),
        out_shape=jax.ShapeDtypeStruct((m, D_MODEL), F32),
        grid=(m // tm,),
        in_specs=[row(a.shape[1]), row(b.shape[1]), row(D_MODEL), _mod_spec(tiles_per_batch),
                  _resident(wo_bf.shape), _resident((1, D_MODEL)), _resident(wg_bf.shape),
                  _resident(wu_bf.shape), _resident(wd_bf.shape), _resident((1, D_MODEL))],
        out_specs=row(D_MODEL),
        compiler_params=_cparams("parallel"),
        name="out_ffn_final" if final else "out_ffn",
    )(a, b, x2, mod, wo_bf, nw, wg_bf, wu_bf, wd_bf, fnw)


def _rope_tables(t):
    half = GLA_DK // 4
    inv_freq = ROPE_BASE ** (-np.arange(half, dtype=np.float64) / half)
    pos = np.arange(t)
    ang_r = (pos // GRID_W)[:, None] * inv_freq[None, :]
    ang_c = (pos % GRID_W)[:, None] * inv_freq[None, :]
    cos = np.concatenate([np.cos(ang_r)] * 2 + [np.cos(ang_c)] * 2, axis=1)
    sin = np.concatenate([-np.sin(ang_r), np.sin(ang_r), -np.sin(ang_c), np.sin(ang_c)], axis=1)
    return (jnp.asarray(np.tile(cos, (1, 2)), F32), jnp.asarray(np.tile(sin, (1, 2)), F32))


def _log_sigmoid(z):
    return jnp.minimum(z, 0.0) - jnp.log1p(jnp.exp(-jnp.abs(z)))


def _cd_in_kernel(x_ref, mod_ref, nw_ref, w_ref, wa_ref, wdec_ref, bdec_ref, cos_ref, sin_ref,
                  f_ref, qk_ref, v_ref, g_ref, la_ref, *, rope):
    tiles = _subtiles(x_ref.shape[0])
    hs = [_rms_mod(x_ref[rows, :], nw_ref[...], mod_ref[0, 0:1, :], mod_ref[0, 1:2, :]).astype(BF16)
          for rows in tiles]
    for rows, h in zip(tiles, hs):
        a = _dot(h, wa_ref[...]).astype(BF16)
        z = _dot(a, wdec_ref[...]) + bdec_ref[...]

        def decay_piece(j):
            cols = slice(j * LANES, (j + 1) * LANES)
            la_ref[rows, cols] = _log_sigmoid(z[:, cols]) * (1.0 / GLA_GATE_TEMP)

        f = _dot(h, w_ref[:, :O_Q])
        decay_piece(0)
        for a in range(f.shape[0] // GRID_W):
            r0 = (rows.start // GRID_W + a) * FFT_PITCH
            f_ref[r0:r0 + GRID_W, :] = f[a * GRID_W:(a + 1) * GRID_W]
            f_ref[r0 + GRID_W:r0 + FFT_PITCH, :] = jnp.zeros((FFT_PITCH - GRID_W, f.shape[1]), F32)
        qk = _dot(h, w_ref[:, O_Q:O_V])
        decay_piece(1)
        if rope:
            reps = qk.shape[1] // LANES
            cos = jnp.concatenate([cos_ref[rows, :]] * reps, axis=1)
            sin = jnp.concatenate([sin_ref[rows, :]] * reps, axis=1)
            w = qk.shape[1]
            q16 = GLA_DK // 4
            first = (lax.broadcasted_iota(jnp.int32, (1, w), 1) % (2 * q16)) < q16
            partner = jnp.where(first, pltpu.roll(qk, w - q16, 1), pltpu.roll(qk, q16, 1))
            qk = qk * cos + partner * sin
        is_q = lax.broadcasted_iota(jnp.int32, (1, qk.shape[1]), 1) < GLA_KEY_WIDTH
        qk_ref[rows, :] = jnp.where(is_q, qk * GLA_DK ** -0.5, qk)
        v_ref[rows, :] = _dot(h, w_ref[:, O_V:O_G]).astype(BF16)
        decay_piece(2)
        g_ref[rows, :] = _dot(h, w_ref[:, O_G:O_A])
        decay_piece(3)


def _cd_in(x2, mod, tiles_per_batch, tm, nw, w_bf, wa_bf, wdec_bf, bdec, cos, sin, rope):
    m = x2.shape[0]
    row = lambda w: pl.BlockSpec((tm, w), lambda i: (i, 0))
    if rope:
        tab = pl.BlockSpec((tm, LANES), lambda i: (i % tiles_per_batch, 0))
    else:
        tab = pl.BlockSpec((tm, LANES), lambda i: (0, 0))
    w512 = 2 * GLA_KEY_WIDTH
    pad = lambda n: n // GRID_W * FFT_PITCH
    outs = (jax.ShapeDtypeStruct((pad(m), FN_WIDTH), F32), jax.ShapeDtypeStruct((m, w512), F32),
            jax.ShapeDtypeStruct((m, GLA_VAL_WIDTH), BF16), jax.ShapeDtypeStruct((m, GLA_VAL_WIDTH), F32),
            jax.ShapeDtypeStruct((m, w512), F32))
    return pl.pallas_call(
        functools.partial(_cd_in_kernel, rope=rope),
        out_shape=outs,
        grid=(m // tm,),
        in_specs=[row(D_MODEL), _mod_spec(tiles_per_batch), _resident((1, D_MODEL)),
                  _resident(w_bf.shape), _resident(wa_bf.shape), _resident(wdec_bf.shape),
                  _resident((1, w512)), tab, tab],
        out_specs=(pl.BlockSpec((pad(tm), FN_WIDTH), lambda i: (i, 0)),
                   row(w512), row(GLA_VAL_WIDTH), row(GLA_VAL_WIDTH), row(w512)),
        compiler_params=_cparams("parallel"),
        name="cd_in_rope" if rope else "cd_in_ctx",
    )(x2, mod, nw, w_bf, wa_bf, wdec_bf, bdec, cos, sin)


def _fft_tables(t):
    n1 = GRID_W
    assert t == n1 * n1
    a = np.arange(n1)
    ang1 = 2 * np.pi * np.outer(a, a) / n1
    cs = np.concatenate([np.cos(ang1), -np.sin(ang1)], axis=0)
    kap = a[:, None, None] + n1 * a[None, :, None]
    ang2 = 2 * np.pi * kap * a[None, None, :] / t
    gr, gi = np.cos(ang2), -np.sin(ang2)
    g = np.concatenate([np.concatenate([gr, -gi], axis=2), np.concatenate([gi, gr], axis=2)], axis=1)
    c = np.arange(FN_GROUP_CH)
    ang3 = 2 * np.pi * np.outer(c, c) / FN_GROUP_CH
    norm = 1.0 / np.sqrt(t * FN_GROUP_CH)
    return (jnp.asarray(cs, F32), jnp.asarray(g, F32),
            jnp.asarray(np.cos(ang3) * norm, F32), jnp.asarray(np.sin(ang3) * norm, F32))


def _fft_kernel(x_ref, cs_ref, g_ref, cc_ref, sc_ref, o_ref, p_scr, zr_scr, zi_scr):
    n1 = GRID_W
    cs = cs_ref[...].astype(BF16)

    def stage1(b, carry):
        xb = x_ref[0, pl.ds(b, n1, stride=FFT_PITCH), :].astype(BF16)
        p_scr[pl.ds(pl.multiple_of(b * FFT_PITCH2, 8), 2 * n1), :] = _dot(cs, xb)
        return carry

    lax.fori_loop(0, n1, stage1, 0, unroll=FFT_UNROLL)

    def stage2(k1, carry):
        pr = p_scr[pl.ds(k1, n1, stride=FFT_PITCH2), :]
        pi = p_scr[pl.ds(n1 + k1, n1, stride=FFT_PITCH2), :]
        z = _dot(g_ref[k1].astype(BF16), jnp.concatenate([pr, pi], axis=0).astype(BF16))
        rows = pl.ds(pl.multiple_of(k1 * FFT_PITCH, 8), n1)
        zr_scr[rows, :] = z[:n1]
        zi_scr[rows, :] = z[n1:]
        return carry

    lax.fori_loop(0, n1, stage2, 0, unroll=FFT_UNROLL)

    ccsc = jnp.concatenate([cc_ref[...], sc_ref[...]], axis=0).astype(BF16)

    def stage3(j, carry):
        parts = []
        for u in range(FFT_UNROLL):
            k2 = j * FFT_UNROLL + u
            parts.append(jnp.concatenate([zr_scr[pl.ds(k2, n1, stride=FFT_PITCH), :],
                                          zi_scr[pl.ds(k2, n1, stride=FFT_PITCH), :]], axis=1))
        out = _dot(jnp.concatenate(parts, axis=0).astype(BF16), ccsc)
        o_ref[0, pl.ds(pl.multiple_of(j * FFT_UNROLL * n1, FFT_UNROLL * n1), FFT_UNROLL * n1), :] = out.astype(BF16)
        return carry

    lax.fori_loop(0, n1 // FFT_UNROLL, stage3, 0)


def _fourier(f_padded, tables):
    b, tp, w = f_padded.shape
    t = tp // FFT_PITCH * GRID_W
    cw = FN_GROUP_CH
    cs, g, cc, sc = tables
    return pl.pallas_call(
        _fft_kernel,
        out_shape=jax.ShapeDtypeStruct((b, t, w), BF16),
        grid=(b, w // cw),
        in_specs=[pl.BlockSpec((1, tp, cw), lambda i, j: (i, 0, j)),
                  _resident(cs.shape), _resident(g.shape), _resident(cc.shape), _resident(sc.shape)],
        out_specs=pl.BlockSpec((1, t, cw), lambda i, j: (i, 0, j)),
        scratch_shapes=[pltpu.VMEM((GRID_W * FFT_PITCH2, cw), F32), pltpu.VMEM((tp, cw), F32),
                        pltpu.VMEM((tp, cw), F32)],
        compiler_params=_cparams("parallel", "parallel"),
        name="fourier_mix",
    )(f_padded, cs, g, cc, sc)


def _block_diag(x, block):
    head = lax.broadcasted_iota(jnp.int32, (1, x.shape[1]), 1) // block
    return jnp.concatenate([jnp.where(head == h, x, jnp.zeros_like(x)) for h in range(GLA_HEADS)], axis=0)


def _cum_rows(x, reverse):
    n, w = x.shape
    row = lax.broadcasted_iota(jnp.int32, (n, 1), 0)
    s = 1
    while s < n:
        if s < 8:
            if reverse:
                shifted = jnp.where(row < n - s, pltpu.roll(x, n - s, 0), 0.0)
            else:
                shifted = jnp.where(row >= s, pltpu.roll(x, s, 0), 0.0)
        else:
            pad = jnp.zeros((s, w), F32)
            shifted = jnp.concatenate([x[s:], pad] if reverse else [pad, x[:n - s]], axis=0)
        x = x + shifted
        s *= 2
    return x


def _gla_prep(q, k, la, reverse, want_out):
    c = GLA_CHUNK
    b = _cum_rows(la, reverse)
    b_last = b[0:1, :] if reverse else b[c - 1:c, :]
    kd = (k * jnp.exp(b_last - b)).astype(BF16)
    decay = jnp.exp(b_last)
    if not want_out:
        return kd, decay
    qe = (q * jnp.exp(b)).astype(BF16)
    ke = _block_diag(k * jnp.exp(-b), GLA_DK).astype(BF16)
    l = lax.broadcasted_iota(jnp.int32, (c, GLA_HEADS * c), 0)
    m = lax.broadcasted_iota(jnp.int32, (c, GLA_HEADS * c), 1) % c
    att = jnp.where((l <= m) if reverse else (l >= m), _dot_nt(qe, ke), 0.0).astype(BF16)
    return kd, decay, qe, att


def _gla_state_step(kd, decay, v, state):
    v_rows = jnp.concatenate([v[:, h * GLA_DV:(h + 1) * GLA_DV] for h in range(GLA_HEADS)], axis=0)
    return state * decay + _dot_tn(v_rows, _block_diag(kd, GLA_DK))


def _gla_apply(prep, v, state):
    c = GLA_CHUNK
    kd, decay, qe, att = prep
    inter = _dot_nt(_block_diag(qe, GLA_DK), state.astype(BF16))
    out = _dot(att, _block_diag(v, GLA_DV)) + jnp.concatenate(
        [inter[h * c:(h + 1) * c] for h in range(GLA_HEADS)], axis=1)
    return out, _gla_state_step(kd, decay, v, state)


def _gla_kernel(qkf_ref, vf_ref, laf_ref, gf_ref, qkb_ref, vb_ref, lab_ref, gb_ref,
                kc_ref, vc_ref, lac_ref, hw_ref, o_ref, sf_scr, sb_scr, acc_scr, *, n_tiles, tile):
    c = GLA_CHUNK
    i = pl.program_id(1)
    kw = GLA_KEY_WIDTH
    n_chunks = tile // c

    @pl.when(i == 0)
    def _():
        n_ctx = kc_ref.shape[1] // c
        sf = jnp.zeros((GLA_DV, kw), F32)
        sb = jnp.zeros((GLA_DV, kw), F32)
        for n in range(n_ctx):
            rf = slice(n * c, (n + 1) * c)
            rb = slice((n_ctx - 1 - n) * c, (n_ctx - n) * c)
            sf = _gla_state_step(*_gla_prep(None, kc_ref[0, rf, :], lac_ref[0, rf, :kw], False, False),
                                 vc_ref[0, rf, :], sf)
            sb = _gla_state_step(*_gla_prep(None, kc_ref[0, rb, :], lac_ref[0, rb, kw:], True, False),
                                 vc_ref[0, rb, :], sb)
        sf_scr[...] = sf
        sb_scr[...] = sb

    hw = hw_ref[...]

    def finish(o, g):
        parts = []
        for h in range(GLA_HEADS):
            oh = o[:, h * GLA_DV:(h + 1) * GLA_DV]
            parts.append(oh * lax.rsqrt(jnp.mean(oh * oh, axis=-1, keepdims=True) + RMS_EPS))
        return (jnp.concatenate(parts, axis=1) * hw * (g * jax.nn.sigmoid(g))).astype(BF16)

    def sweep(second_pass):
        def emit(o, g_ref, rloc, row0):
            rows = pl.ds(pl.multiple_of(row0, c), c)
            if second_pass:
                o_ref[0, rows, :] = finish(o + acc_scr[rows, :], g_ref[0, rloc, :])
            else:
                acc_scr[rows, :] = o

        def rows_of(n):
            nb = n_chunks - 1 - n
            return slice(n * c, (n + 1) * c), slice(nb * c, (nb + 1) * c)

        def prep(n):
            rf, rb = rows_of(n)
            qkf, qkb = qkf_ref[0, rf, :], qkb_ref[0, rb, :]
            return (_gla_prep(qkf[:, :kw], qkf[:, kw:], laf_ref[0, rf, :], False, True),
                    _gla_prep(qkb[:, :kw], qkb[:, kw:], lab_ref[0, rb, :], True, True))

        sf, sb = sf_scr[...], sb_scr[...]
        ready = {n: prep(n) for n in range(min(GLA_PREP_AHEAD, n_chunks))}
        for n in range(n_chunks):
            rf, rb = rows_of(n)
            pf, pb = ready.pop(n)
            of, sf = _gla_apply(pf, vf_ref[0, rf, :], sf)
            emit(of, gf_ref, rf, i * tile + rf.start)
            ob, sb = _gla_apply(pb, vb_ref[0, rb, :], sb)
            emit(ob, gb_ref, rb, (n_tiles - 1 - i) * tile + rb.start)
            if n + GLA_PREP_AHEAD < n_chunks:
                ready[n + GLA_PREP_AHEAD] = prep(n + GLA_PREP_AHEAD)
        sf_scr[...] = sf
        sb_scr[...] = sb

    @pl.when(i < n_tiles // 2)
    def _():
        sweep(False)

    @pl.when(i >= n_tiles // 2)
    def _():
        sweep(True)


def _gla(qk, v, la, g, qk_c, v_c, la_c, head_w):
    b, t, _ = qk.shape
    tile = GLA_TILE
    n_tiles = t // tile
    kw, vw = GLA_KEY_WIDTH, GLA_VAL_WIDTH
    fwd = lambda w, col: pl.BlockSpec((1, tile, w), lambda bi, i: (bi, i, col))
    bwd = lambda w, col: pl.BlockSpec((1, tile, w), lambda bi, i: (bi, n_tiles - 1 - i, col))
    ctx = lambda w: pl.BlockSpec((1, CTX_LEN, w), lambda bi, i: (bi, 0, 0))
    return pl.pallas_call(
        functools.partial(_gla_kernel, n_tiles=n_tiles, tile=tile),
        out_shape=jax.ShapeDtypeStruct((b, t, vw), BF16),
        grid=(b, n_tiles),
        in_specs=[fwd(2 * kw, 0), fwd(vw, 0), fwd(kw, 0), fwd(vw, 0),
                  bwd(2 * kw, 0), bwd(vw, 0), bwd(kw, 1), bwd(vw, 0),
                  pl.BlockSpec((1, CTX_LEN, kw), lambda bi, i: (bi, 0, 1)), ctx(vw), ctx(2 * kw),
                  pl.BlockSpec((1, vw), lambda bi, i: (0, 0))],
        out_specs=pl.BlockSpec((1, t, vw), lambda bi, i: (bi, 0, 0)),
        scratch_shapes=[pltpu.VMEM((GLA_DV, kw), F32), pltpu.VMEM((GLA_DV, kw), F32), pltpu.VMEM((t, vw), F32)],
        compiler_params=_cparams("parallel", "arbitrary"),
        name="gla",
    )(qk, v, la, g, qk, v, la, g, qk_c, v_c, la_c, head_w)


def kernel(x, c, ctx, c_ctx, ada_w, ada_b, norm_mix_w, norm_ffn_w, ffn_w_gate, ffn_w_up, ffn_w_down,
           ab_w_in, ab_w_out, ab_sgu_norm_w, ab_sgu_w, ab_sgu_b, ab_rel_bias,
           cd_w_in, cd_w_out, cd_decay_w_fwd, cd_decay_b_fwd, cd_decay_w_bwd, cd_decay_b_bwd, cd_head_norm_w,
           final_norm_w):
    bsz, t, d = x.shape
    n_ctx = ctx.shape[1]
    tpb = t // ROW_TILE
    x2 = x.reshape(bsz * t, d)
    ctx2 = ctx.reshape(bsz * n_ctx, d)
    row = lambda w: w.reshape(1, -1)

    cc = jnp.concatenate([c, c_ctx[None, :], jnp.zeros((16 - bsz - 1, d), F32)], axis=0)
    mod = _ada(cc, ada_w, ada_b)
    mod_x = [mod[i, :bsz].reshape(bsz, 6, d) for i in range(2)]
    mod_c = [mod[i, bsz:bsz + 1].reshape(1, 6, d) for i in range(2)]
    ffn = [(ffn_w_gate[i].astype(BF16), ffn_w_up[i].astype(BF16), ffn_w_down[i].astype(BF16)) for i in range(2)]

    w_in = ab_w_in[0].astype(BF16)
    sgu = (row(ab_sgu_norm_w[0]), ab_sgu_w[0].astype(BF16),
           jnp.broadcast_to(ab_sgu_b[0][:, :, None], (A_GROUPS, A_CHUNK, LANES)))
    a_l, q_l, k_l, v_l = _ab_in(x2, mod_x[0], tpb, ROW_TILE, row(norm_mix_w[0]), w_in, *sgu)
    a_c, q_c, k_c, v_c = _ab_in(ctx2, mod_c[0], None, n_ctx, row(norm_mix_w[0]), w_in, *sgu)
    seq = lambda z, n: z.reshape(bsz, n, z.shape[-1])
    b_l = _na(seq(q_l, t), seq(k_l, t), seq(v_l, t), seq(k_c, n_ctx), seq(v_c, n_ctx), _na_bias_table(ab_rel_bias[0]))
    b_c = _ctx_attn(seq(q_c, n_ctx), seq(k_c, n_ctx), seq(v_c, n_ctx))
    wo = ab_w_out[0].astype(BF16)
    x2 = _out_ffn(a_l, b_l.reshape(bsz * t, -1), x2, mod_x[0], tpb, ROW_TILE, wo, row(norm_ffn_w[0]), *ffn[0],
                  row(final_norm_w), False)
    ctx2 = _out_ffn(a_c, b_c.reshape(bsz * n_ctx, -1), ctx2, mod_c[0], None, n_ctx, wo, row(norm_ffn_w[0]),
                    *ffn[0], row(final_norm_w), False)

    w_in = cd_w_in[0]
    w_main = w_in[:, :O_A].astype(BF16)
    w_a = jnp.pad(w_in[:, O_A:], ((0, 0), (0, LANES - 2 * GLA_LOW_RANK))).astype(BF16)
    w_dec = jnp.zeros((LANES, 2 * GLA_KEY_WIDTH), F32)
    w_dec = w_dec.at[:GLA_LOW_RANK, :GLA_KEY_WIDTH].set(cd_decay_w_fwd[0])
    w_dec = w_dec.at[GLA_LOW_RANK:2 * GLA_LOW_RANK, GLA_KEY_WIDTH:].set(cd_decay_w_bwd[0]).astype(BF16)
    b_dec = jnp.concatenate([cd_decay_b_fwd[0], cd_decay_b_bwd[0]]).reshape(1, -1)
    cos, sin = _rope_tables(t)
    f_l, qk_l, v_l, g_l, la_l = _cd_in(x2, mod_x[1], tpb, ROW_TILE, row(norm_mix_w[1]), w_main, w_a, w_dec, b_dec,
                                      cos, sin, True)
    _, qk_c, v_c, _, la_c = _cd_in(ctx2, mod_c[1], None, n_ctx, row(norm_mix_w[1]), w_main, w_a, w_dec, b_dec,
                                   cos, sin, False)
    fm = _fourier(f_l.reshape(bsz, -1, FN_WIDTH), _fft_tables(t))
    head_w = jnp.tile(cd_head_norm_w[0], GLA_HEADS).reshape(1, -1)
    go = _gla(seq(qk_l, t), seq(v_l, t), seq(la_l, t), seq(g_l, t),
              seq(qk_c, n_ctx), seq(v_c, n_ctx), seq(la_c, n_ctx), head_w)
    out = _out_ffn(fm.reshape(bsz * t, -1), go.reshape(bsz * t, -1), x2, mod_x[1], tpb, ROW_TILE,
                   cd_w_out[0].astype(BF16), row(norm_ffn_w[1]), *ffn[1], row(final_norm_w), True)
    return out.reshape(bsz, t, d)
```

```python
import functools

import numpy as np
import jax
import jax.numpy as jnp
from jax import lax
from jax.experimental import pallas as pl
from jax.experimental.pallas import tpu as pltpu

F32 = jnp.float32
BF16 = jnp.bfloat16

D_MODEL = 1024
CTX_LEN = 256
GRID_W = 64
A_WIDTH = 512
A_GROUPS = 4
A_CHUNK = 128
NA_HEADS = 8
NA_HEAD_DIM = 64
NA_WIDTH = 512
NA_WIN_ROWS = 8
NA_WIN_COLS = 16
O_NA_Q = 2 * A_WIDTH
FN_WIDTH = 512
FN_GROUP_CH = 128
GLA_HEADS = 4
GLA_VAL_WIDTH = 512
GLA_KEY_WIDTH = 256
GLA_DK = 64
GLA_DV = 128
GLA_LOW_RANK = 16
GLA_GATE_TEMP = 16.0
GLA_CHUNK = 64
O_Q = FN_WIDTH
O_V = O_Q + 2 * GLA_KEY_WIDTH
O_G = O_V + GLA_VAL_WIDTH
O_A = O_G + GLA_VAL_WIDTH
FFN_HIDDEN = 2816
ROPE_BASE = 10000.0
RMS_EPS = 1e-6
NEG_INF = -1e30

LANES = 128
VMEM_LIMIT_BYTES = 56 * 2 ** 20

ROW_TILE = 512
ROW_SUBTILE = 256
FFN_SPLIT = 2
GLA_TILE = 512
NA_STAGE_LAG = 2
FFT_UNROLL = 32
FFT_PITCH = GRID_W + 8
FFT_PITCH2 = 2 * GRID_W + 8
GLA_PREP_AHEAD = 2


def _cparams(*sem):
    return pltpu.CompilerParams(dimension_semantics=sem, vmem_limit_bytes=VMEM_LIMIT_BYTES)


def _resident(shape):
    nd = len(shape)
    return pl.BlockSpec(shape, lambda *_: (0,) * nd, pipeline_mode=pl.Buffered(1))


def _dot(a, b):
    return jnp.dot(a, b, preferred_element_type=F32)


def _dot_nt(a, b):
    return lax.dot_general(a, b, (((1,), (1,)), ((), ())), preferred_element_type=F32)


def _dot_tn(a, b):
    return lax.dot_general(a, b, (((0,), (0,)), ((), ())), preferred_element_type=F32)


def _subtiles(n_rows):
    step = min(ROW_SUBTILE, n_rows)
    return [slice(r, r + step) for r in range(0, n_rows, step)]


def _rms(x, w):
    return x * lax.rsqrt(jnp.mean(x * x, axis=-1, keepdims=True) + RMS_EPS) * w


def _rms_mod(x, w, shift, scale):
    return _rms(x, w) * (1.0 + scale) + shift


def _ada_kernel(c_ref, w_ref, b_ref, o_ref):
    c = c_ref[...]
    s = c * jax.nn.sigmoid(c)
    o_ref[0] = _dot(s.astype(BF16), w_ref[0].astype(BF16)) + b_ref[0]


def _ada(cc, ada_w, ada_b):
    depth, d, n = ada_w.shape
    r = cc.shape[0]
    tn = 1536
    return pl.pallas_call(
        _ada_kernel,
        out_shape=jax.ShapeDtypeStruct((depth, r, n), F32),
        grid=(depth, n // tn),
        in_specs=[pl.BlockSpec((r, d), lambda i, j: (0, 0)),
                  pl.BlockSpec((1, d, tn), lambda i, j: (i, 0, j)),
                  pl.BlockSpec((1, 1, tn), lambda i, j: (i, 0, j))],
        out_specs=pl.BlockSpec((1, r, tn), lambda i, j: (i, 0, j)),
        compiler_params=_cparams("arbitrary", "arbitrary"),
        name="ada_mod",
    )(cc, ada_w, ada_b.reshape(depth, 1, n))


def _mod_spec(tiles_per_batch):
    if tiles_per_batch is None:
        return pl.BlockSpec((1, 6, D_MODEL), lambda i: (0, 0, 0))
    return pl.BlockSpec((1, 6, D_MODEL), lambda i: (i // tiles_per_batch, 0, 0))


def _ab_in_kernel(x_ref, mod_ref, nw_ref, w_ref, snw_ref, sguw_ref, sgub_ref,
                  a_ref, q_ref, k_ref, v_ref, *, tm):
    tiles = _subtiles(tm)
    hs = [_rms_mod(x_ref[rows, :], nw_ref[...], mod_ref[0, 0:1, :], mod_ref[0, 1:2, :]).astype(BF16)
          for rows in tiles]
    uvs = [_dot(h, w_ref[:, :O_NA_Q]) for h in hs]
    for rows, h, uv in zip(tiles, hs, uvs):
        qkv = _dot(h, w_ref[:, O_NA_Q:])
        q_ref[rows, :] = (qkv[:, :NA_WIDTH] * NA_HEAD_DIM ** -0.5).astype(BF16)
        k_ref[rows, :] = qkv[:, NA_WIDTH:2 * NA_WIDTH].astype(BF16)
        v_ref[rows, :] = qkv[:, 2 * NA_WIDTH:].astype(BF16)
        uv = jax.nn.gelu(uv)
        u = uv[:, :A_WIDTH]
        v = _rms(uv[:, A_WIDTH:], snw_ref[...]).astype(BF16)
        for ci in range(uv.shape[0] // A_CHUNK):
            rs = slice(ci * A_CHUNK, (ci + 1) * A_CHUNK)
            out_rows = slice(rows.start + rs.start, rows.start + rs.stop)
            for g in range(A_GROUPS):
                cs = slice(g * LANES, (g + 1) * LANES)
                gate = _dot(sguw_ref[g], v[rs, cs]) + sgub_ref[g]
                a_ref[out_rows, cs] = (u[rs, cs] * gate).astype(BF16)


def _ab_in(x2, mod, tiles_per_batch, tm, nw, w_bf, snw, sguw_bf, sgub_exp):
    m = x2.shape[0]
    n_in = w_bf.shape[1]
    row = lambda w: pl.BlockSpec((tm, w), lambda i: (i, 0))
    out = jax.ShapeDtypeStruct((m, A_WIDTH), BF16)
    return pl.pallas_call(
        functools.partial(_ab_in_kernel, tm=tm),
        out_shape=(out, out, out, out),
        grid=(m // tm,),
        in_specs=[row(D_MODEL), _mod_spec(tiles_per_batch), _resident((1, D_MODEL)),
                  _resident((D_MODEL, n_in)), _resident((1, A_WIDTH)),
                  _resident((A_GROUPS, A_CHUNK, A_CHUNK)), _resident((A_GROUPS, A_CHUNK, LANES))],
        out_specs=(row(A_WIDTH),) * 4,
        compiler_params=_cparams("parallel"),
        name="ab_in",
    )(x2, mod, nw, w_bf, snw, sguw_bf, sgub_exp)


def _na_bias_table(rel_bias):
    w, nrow = GRID_W, 2 * NA_WIN_ROWS - 1
    col = np.arange(w)
    col_start = np.clip(col - NA_WIN_COLS // 2, 0, w - NA_WIN_COLS)
    col_mask = (col[None, :] >= col_start[:, None]) & (col[None, :] < col_start[:, None] + NA_WIN_COLS)
    lo = w - NA_WIN_COLS
    r_ext = jnp.pad(rel_bias, ((0, 0), (0, 0), (lo, 2 * w - lo - (2 * NA_WIN_COLS - 1))))
    flat = jnp.tile(r_ext, (1, 1, w))[:, :, :w * (2 * w - 1)]
    toe = flat.reshape(NA_HEADS, nrow, w, 2 * w - 1)[:, :, :, w - 1:]
    toe = jnp.where(col_mask, toe, NEG_INF).reshape(NA_HEADS // 2, 2, nrow, w, w)
    strips = [jnp.concatenate([toe[:, :, j - o + NA_WIN_ROWS - 1] for j in range(NA_WIN_ROWS)], axis=-1)
              for o in range(NA_WIN_ROWS)]
    return jnp.stack([s.reshape(NA_HEADS // 2, 2 * w, NA_WIN_ROWS * w) for s in strips], axis=1)


def _stack_heads(qb):
    first = lax.broadcasted_iota(jnp.int32, (1, LANES), 1) < NA_HEAD_DIM
    zero = jnp.zeros_like(qb)
    return jnp.concatenate([jnp.where(first, qb, zero), jnp.where(first, zero, qb)], axis=0)


def _unstack_heads(r):
    n = r.shape[0] // 2
    first = lax.broadcasted_iota(jnp.int32, (1, LANES), 1) < NA_HEAD_DIM
    return jnp.where(first, r[:n], r[n:])


def _na_kernel(q_ref, k_ref, v_ref, kc_ref, vc_ref, tab_ref, o_ref, kt_scr, kct_scr, *, rows):
    t = rows * GRID_W
    kt_scr[0] = k_ref[0].T
    kt_scr[1, :, :t - GRID_W] = k_ref[0, GRID_W:, :].T
    kct_scr[...] = kc_ref[0].T
    kct = kct_scr[...]
    vc = vc_ref[0]
    win = NA_WIN_ROWS * GRID_W

    def window(r):
        return min(max(r - NA_WIN_ROWS // 2, 0), rows - NA_WIN_ROWS)

    def scores(r):
        start = window(r)
        odd = start % 2
        kt0 = (start - odd) * GRID_W
        qs = _stack_heads(q_ref[0, r * GRID_W:(r + 1) * GRID_W, :])
        return _dot(qs, kt_scr[odd, :, kt0:kt0 + win]) + tab_ref[0, r - start], _dot(qs, kct)

    def softmax(s):
        s_nb, s_cx = s
        m = jnp.maximum(jnp.max(s_nb, axis=-1, keepdims=True), jnp.max(s_cx, axis=-1, keepdims=True))
        e_nb = jnp.exp(s_nb - m)
        e_cx = jnp.exp(s_cx - m)
        l = jnp.sum(e_nb, axis=-1, keepdims=True) + jnp.sum(e_cx, axis=-1, keepdims=True)
        return e_nb.astype(BF16), e_cx.astype(BF16), l

    def values(r, p):
        e_nb, e_cx, l = p
        k0 = window(r) * GRID_W
        acc = _dot(e_nb, v_ref[0, k0:k0 + win, :]) + _dot(e_cx, vc)
        o_ref[0, r * GRID_W:(r + 1) * GRID_W, :] = _unstack_heads(acc / l).astype(BF16)

    lag = NA_STAGE_LAG
    s_vals, p_vals = {}, {}
    for r in range(rows + 2 * lag):
        if 0 <= r - 2 * lag < rows:
            values(r - 2 * lag, p_vals.pop(r - 2 * lag))
        if 0 <= r - lag < rows:
            p_vals[r - lag] = softmax(s_vals.pop(r - lag))
        if r < rows:
            s_vals[r] = scores(r)


def _na(q, k, v, kc, vc, table):
    b, t, _ = q.shape
    rows = t // GRID_W
    lat = pl.BlockSpec((1, t, LANES), lambda j, i: (i, 0, j))
    ctx = pl.BlockSpec((1, CTX_LEN, LANES), lambda j, i: (i, 0, j))
    return pl.pallas_call(
        functools.partial(_na_kernel, rows=rows),
        out_shape=jax.ShapeDtypeStruct((b, t, NA_WIDTH), BF16),
        grid=(NA_HEADS // 2, b),
        in_specs=[lat, lat, lat, ctx, ctx,
                  pl.BlockSpec((1, NA_WIN_ROWS, 2 * GRID_W, NA_WIN_ROWS * GRID_W), lambda j, i: (j, 0, 0, 0))],
        out_specs=lat,
        scratch_shapes=[pltpu.VMEM((2, LANES, t), BF16), pltpu.VMEM((LANES, CTX_LEN), BF16)],
        compiler_params=_cparams("parallel", "parallel"),
        name="na_attn",
    )(q, k, v, kc, vc, table)


def _ctx_attn_kernel(q_ref, k_ref, v_ref, o_ref):
    s = _dot_nt(_stack_heads(q_ref[0]), k_ref[0])
    e = jnp.exp(s - jnp.max(s, axis=-1, keepdims=True))
    acc = _dot(e.astype(BF16), v_ref[0])
    o_ref[0] = _unstack_heads(acc / jnp.sum(e, axis=-1, keepdims=True)).astype(BF16)


def _ctx_attn(q, k, v):
    b = q.shape[0]
    spec = pl.BlockSpec((1, CTX_LEN, LANES), lambda i, j: (i, 0, j))
    return pl.pallas_call(
        _ctx_attn_kernel,
        out_shape=jax.ShapeDtypeStruct((b, CTX_LEN, NA_WIDTH), BF16),
        grid=(b, NA_HEADS // 2),
        in_specs=[spec, spec, spec],
        out_specs=spec,
        compiler_params=_cparams("parallel", "parallel"),
        name="ctx_attn",
    )(q, k, v)


def _out_ffn_kernel(a_ref, b_ref, x_ref, mod_ref, wo_ref, nw_ref, wg_ref, wu_ref, wd_ref, fnw_ref, o_ref,
                    *, final):
    half = wo_ref.shape[0] // 2
    piece = FFN_HIDDEN // FFN_SPLIT
    tiles = _subtiles(x_ref.shape[0])
    ys = [_dot(a_ref[rows, :], wo_ref[:half, :]) + _dot(b_ref[rows, :], wo_ref[half:, :]) for rows in tiles]
    x1s = [x_ref[rows, :] + mod_ref[0, 2:3, :] * y for rows, y in zip(tiles, ys)]
    hs = [_rms_mod(x1, nw_ref[...], mod_ref[0, 3:4, :], mod_ref[0, 4:5, :]).astype(BF16) for x1 in x1s]
    accs = [None] * len(tiles)
    for j in range(FFN_SPLIT):
        cs = slice(j * piece, (j + 1) * piece)
        gs = [_dot(h, wg_ref[:, cs]) for h in hs]
        acts = [(g * jax.nn.sigmoid(g) * _dot(h, wu_ref[:, cs])).astype(BF16) for g, h in zip(gs, hs)]
        parts = [_dot(act, wd_ref[cs, :]) for act in acts]
        accs = [part if acc is None else acc + part for acc, part in zip(accs, parts)]
    for rows, x1, acc in zip(tiles, x1s, accs):
        x2 = x1 + mod_ref[0, 5:6, :] * acc
        o_ref[rows, :] = _rms(x2, fnw_ref[...]) if final else x2


def _out_ffn(a, b, x2, mod, tiles_per_batch, tm, wo_bf, nw, wg_bf, wu_bf, wd_bf, fnw, final):
    m = x2.shape[0]
    row = lambda w: pl.BlockSpec((tm, w), lambda i: (i, 0))
    return pl.pallas_call(
        functools.partial(_out_ffn_kernel, final=final),
        out_shape=jax.ShapeDtypeStruct((m, D_MODEL), F32),
        grid=(m // tm,),
        in_specs=[row(a.shape[1]), row(b.shape[1]), row(D_MODEL), _mod_spec(tiles_per_batch),
                  _resident(wo_bf.shape), _resident((1, D_MODEL)), _resident(wg_bf.shape),
                  _resident(wu_bf.shape), _resident(wd_bf.shape), _resident((1, D_MODEL))],
        out_specs=row(D_MODEL),
        compiler_params=_cparams("parallel"),
        name="out_ffn_final" if final else "out_ffn",
    )(a, b, x2, mod, wo_bf, nw, wg_bf, wu_bf, wd_bf, fnw)


def _rope_tables(t):
    half = GLA_DK // 4
    inv_freq = ROPE_BASE ** (-np.arange(half, dtype=np.float64) / half)
    pos = np.arange(t)
    ang_r = (pos // GRID_W)[:, None] * inv_freq[None, :]
    ang_c = (pos % GRID_W)[:, None] * inv_freq[None, :]
    cos = np.concatenate([np.cos(ang_r)] * 2 + [np.cos(ang_c)] * 2, axis=1)
    sin = np.concatenate([-np.sin(ang_r), np.sin(ang_r), -np.sin(ang_c), np.sin(ang_c)], axis=1)
    return (jnp.asarray(np.tile(cos, (1, 2)), F32), jnp.asarray(np.tile(sin, (1, 2)), F32))


def _log_sigmoid(z):
    return jnp.minimum(z, 0.0) - jnp.log1p(jnp.exp(-jnp.abs(z)))


def _cd_in_kernel(x_ref, mod_ref, nw_ref, w_ref, wa_ref, wdec_ref, bdec_ref, cos_ref, sin_ref,
                  f_ref, qk_ref, v_ref, g_ref, la_ref, *, rope):
    for rows in [slice(0, x_ref.shape[0])]:
        h = _rms_mod(x_ref[rows, :], nw_ref[...], mod_ref[0, 0:1, :], mod_ref[0, 1:2, :]).astype(BF16)
        f = _dot(h, w_ref[:, :O_Q])
        for a in range(f.shape[0] // GRID_W):
            r0 = (rows.start // GRID_W + a) * FFT_PITCH
            f_ref[r0:r0 + GRID_W, :] = f[a * GRID_W:(a + 1) * GRID_W]
            f_ref[r0 + GRID_W:r0 + FFT_PITCH, :] = jnp.zeros((FFT_PITCH - GRID_W, f.shape[1]), F32)
        qk = _dot(h, w_ref[:, O_Q:O_V])
        if rope:
            reps = qk.shape[1] // LANES
            cos = jnp.concatenate([cos_ref[rows, :]] * reps, axis=1)
            sin = jnp.concatenate([sin_ref[rows, :]] * reps, axis=1)
            w = qk.shape[1]
            q16 = GLA_DK // 4
            first = (lax.broadcasted_iota(jnp.int32, (1, w), 1) % (2 * q16)) < q16
            partner = jnp.where(first, pltpu.roll(qk, w - q16, 1), pltpu.roll(qk, q16, 1))
            qk = qk * cos + partner * sin
        is_q = lax.broadcasted_iota(jnp.int32, (1, qk.shape[1]), 1) < GLA_KEY_WIDTH
        qk_ref[rows, :] = jnp.where(is_q, qk * GLA_DK ** -0.5, qk)
        v_ref[rows, :] = _dot(h, w_ref[:, O_V:O_G]).astype(BF16)
        g_ref[rows, :] = _dot(h, w_ref[:, O_G:O_A])
        a = _dot(h, wa_ref[...]).astype(BF16)
        la_ref[rows, :] = _log_sigmoid(_dot(a, wdec_ref[...]) + bdec_ref[...]) * (1.0 / GLA_GATE_TEMP)


def _cd_in(x2, mod, tiles_per_batch, tm, nw, w_bf, wa_bf, wdec_bf, bdec, cos, sin, rope):
    m = x2.shape[0]
    row = lambda w: pl.BlockSpec((tm, w), lambda i: (i, 0))
    if rope:
        tab = pl.BlockSpec((tm, LANES), lambda i: (i % tiles_per_batch, 0))
    else:
        tab = pl.BlockSpec((tm, LANES), lambda i: (0, 0))
    w512 = 2 * GLA_KEY_WIDTH
    pad = lambda n: n // GRID_W * FFT_PITCH
    outs = (jax.ShapeDtypeStruct((pad(m), FN_WIDTH), F32), jax.ShapeDtypeStruct((m, w512), F32),
            jax.ShapeDtypeStruct((m, GLA_VAL_WIDTH), BF16), jax.ShapeDtypeStruct((m, GLA_VAL_WIDTH), F32),
            jax.ShapeDtypeStruct((m, w512), F32))
    return pl.pallas_call(
        functools.partial(_cd_in_kernel, rope=rope),
        out_shape=outs,
        grid=(m // tm,),
        in_specs=[row(D_MODEL), _mod_spec(tiles_per_batch), _resident((1, D_MODEL)),
                  _resident(w_bf.shape), _resident(wa_bf.shape), _resident(wdec_bf.shape),
                  _resident((1, w512)), tab, tab],
        out_specs=(pl.BlockSpec((pad(tm), FN_WIDTH), lambda i: (i, 0)),
                   row(w512), row(GLA_VAL_WIDTH), row(GLA_VAL_WIDTH), row(w512)),
        compiler_params=_cparams("parallel"),
        name="cd_in_rope" if rope else "cd_in_ctx",
    )(x2, mod, nw, w_bf, wa_bf, wdec_bf, bdec, cos, sin)


def _fft_tables(t):
    n1 = GRID_W
    assert t == n1 * n1
    a = np.arange(n1)
    ang1 = 2 * np.pi * np.outer(a, a) / n1
    cs = np.concatenate([np.cos(ang1), -np.sin(ang1)], axis=0)
    kap = a[:, None, None] + n1 * a[None, :, None]
    ang2 = 2 * np.pi * kap * a[None, None, :] / t
    gr, gi = np.cos(ang2), -np.sin(ang2)
    g = np.concatenate([np.concatenate([gr, -gi], axis=2), np.concatenate([gi, gr], axis=2)], axis=1)
    c = np.arange(FN_GROUP_CH)
    ang3 = 2 * np.pi * np.outer(c, c) / FN_GROUP_CH
    norm = 1.0 / np.sqrt(t * FN_GROUP_CH)
    return (jnp.asarray(cs, F32), jnp.asarray(g, F32),
            jnp.asarray(np.cos(ang3) * norm, F32), jnp.asarray(np.sin(ang3) * norm, F32))


def _fft_kernel(x_ref, cs_ref, g_ref, cc_ref, sc_ref, o_ref, p_scr, zr_scr, zi_scr):
    n1 = GRID_W
    cs = cs_ref[...].astype(BF16)

    def stage1(b, carry):
        xb = x_ref[0, pl.ds(b, n1, stride=FFT_PITCH), :].astype(BF16)
        p_scr[pl.ds(pl.multiple_of(b * FFT_PITCH2, 8), 2 * n1), :] = _dot(cs, xb)
        return carry

    lax.fori_loop(0, n1, stage1, 0, unroll=FFT_UNROLL)

    def stage2(k1, carry):
        pr = p_scr[pl.ds(k1, n1, stride=FFT_PITCH2), :]
        pi = p_scr[pl.ds(n1 + k1, n1, stride=FFT_PITCH2), :]
        z = _dot(g_ref[k1].astype(BF16), jnp.concatenate([pr, pi], axis=0).astype(BF16))
        rows = pl.ds(pl.multiple_of(k1 * FFT_PITCH, 8), n1)
        zr_scr[rows, :] = z[:n1]
        zi_scr[rows, :] = z[n1:]
        return carry

    lax.fori_loop(0, n1, stage2, 0, unroll=FFT_UNROLL)

    ccsc = jnp.concatenate([cc_ref[...], sc_ref[...]], axis=0).astype(BF16)

    def stage3(j, carry):
        parts = []
        for u in range(FFT_UNROLL):
            k2 = j * FFT_UNROLL + u
            parts.append(jnp.concatenate([zr_scr[pl.ds(k2, n1, stride=FFT_PITCH), :],
                                          zi_scr[pl.ds(k2, n1, stride=FFT_PITCH), :]], axis=1))
        out = _dot(jnp.concatenate(parts, axis=0).astype(BF16), ccsc)
        o_ref[0, pl.ds(pl.multiple_of(j * FFT_UNROLL * n1, FFT_UNROLL * n1), FFT_UNROLL * n1), :] = out.astype(BF16)
        return carry

    lax.fori_loop(0, n1 // FFT_UNROLL, stage3, 0)


def _fourier(f_padded, tables):
    b, tp, w = f_padded.shape
    t = tp // FFT_PITCH * GRID_W
    cw = FN_GROUP_CH
    cs, g, cc, sc = tables
    return pl.pallas_call(
        _fft_kernel,
        out_shape=jax.ShapeDtypeStruct((b, t, w), BF16),
        grid=(b, w // cw),
        in_specs=[pl.BlockSpec((1, tp, cw), lambda i, j: (i, 0, j)),
                  _resident(cs.shape), _resident(g.shape), _resident(cc.shape), _resident(sc.shape)],
        out_specs=pl.BlockSpec((1, t, cw), lambda i, j: (i, 0, j)),
        scratch_shapes=[pltpu.VMEM((GRID_W * FFT_PITCH2, cw), F32), pltpu.VMEM((tp, cw), F32),
                        pltpu.VMEM((tp, cw), F32)],
        compiler_params=_cparams("parallel", "parallel"),
        name="fourier_mix",
    )(f_padded, cs, g, cc, sc)


def _block_diag(x, block):
    head = lax.broadcasted_iota(jnp.int32, (1, x.shape[1]), 1) // block
    return jnp.concatenate([jnp.where(head == h, x, jnp.zeros_like(x)) for h in range(GLA_HEADS)], axis=0)


def _cum_rows(x, reverse):
    n, w = x.shape
    row = lax.broadcasted_iota(jnp.int32, (n, 1), 0)
    s = 1
    while s < n:
        if s < 8:
            if reverse:
                shifted = jnp.where(row < n - s, pltpu.roll(x, n - s, 0), 0.0)
            else:
                shifted = jnp.where(row >= s, pltpu.roll(x, s, 0), 0.0)
        else:
            pad = jnp.zeros((s, w), F32)
            shifted = jnp.concatenate([x[s:], pad] if reverse else [pad, x[:n - s]], axis=0)
        x = x + shifted
        s *= 2
    return x


def _gla_prep(q, k, la, reverse, want_out):
    c = GLA_CHUNK
    b = _cum_rows(la, reverse)
    b_last = b[0:1, :] if reverse else b[c - 1:c, :]
    kd = (k * jnp.exp(b_last - b)).astype(BF16)
    decay = jnp.exp(b_last)
    if not want_out:
        return kd, decay
    qe = (q * jnp.exp(b)).astype(BF16)
    ke = _block_diag(k * jnp.exp(-b), GLA_DK).astype(BF16)
    l = lax.broadcasted_iota(jnp.int32, (c, GLA_HEADS * c), 0)
    m = lax.broadcasted_iota(jnp.int32, (c, GLA_HEADS * c), 1) % c
    att = jnp.where((l <= m) if reverse else (l >= m), _dot_nt(qe, ke), 0.0).astype(BF16)
    return kd, decay, qe, att


def _gla_state_step(kd, decay, v, state):
    v_rows = jnp.concatenate([v[:, h * GLA_DV:(h + 1) * GLA_DV] for h in range(GLA_HEADS)], axis=0)
    return state * decay + _dot_tn(v_rows, _block_diag(kd, GLA_DK))


def _gla_apply(prep, v, state):
    c = GLA_CHUNK
    kd, decay, qe, att = prep
    inter = _dot_nt(_block_diag(qe, GLA_DK), state.astype(BF16))
    out = _dot(att, _block_diag(v, GLA_DV)) + jnp.concatenate(
        [inter[h * c:(h + 1) * c] for h in range(GLA_HEADS)], axis=1)
    return out, _gla_state_step(kd, decay, v, state)


def _gla_kernel(qkf_ref, vf_ref, laf_ref, gf_ref, qkb_ref, vb_ref, lab_ref, gb_ref,
                kc_ref, vc_ref, lac_ref, hw_ref, o_ref, sf_scr, sb_scr, acc_scr, *, n_tiles, tile):
    c = GLA_CHUNK
    i = pl.program_id(1)
    kw = GLA_KEY_WIDTH
    n_chunks = tile // c

    @pl.when(i == 0)
    def _():
        n_ctx = kc_ref.shape[1] // c
        sf = jnp.zeros((GLA_DV, kw), F32)
        sb = jnp.zeros((GLA_DV, kw), F32)
        for n in range(n_ctx):
            rf = slice(n * c, (n + 1) * c)
            rb = slice((n_ctx - 1 - n) * c, (n_ctx - n) * c)
            sf = _gla_state_step(*_gla_prep(None, kc_ref[0, rf, :], lac_ref[0, rf, :kw], False, False),
                                 vc_ref[0, rf, :], sf)
            sb = _gla_state_step(*_gla_prep(None, kc_ref[0, rb, :], lac_ref[0, rb, kw:], True, False),
                                 vc_ref[0, rb, :], sb)
        sf_scr[...] = sf
        sb_scr[...] = sb

    hw = hw_ref[...]

    def finish(o, g):
        parts = []
        for h in range(GLA_HEADS):
            oh = o[:, h * GLA_DV:(h + 1) * GLA_DV]
            parts.append(oh * lax.rsqrt(jnp.mean(oh * oh, axis=-1, keepdims=True) + RMS_EPS))
        return (jnp.concatenate(parts, axis=1) * hw * (g * jax.nn.sigmoid(g))).astype(BF16)

    def sweep(second_pass):
        def emit(o, g_ref, rloc, row0):
            rows = pl.ds(pl.multiple_of(row0, c), c)
            if second_pass:
                o_ref[0, rows, :] = finish(o + acc_scr[rows, :], g_ref[0, rloc, :])
            else:
                acc_scr[rows, :] = o

        def rows_of(n):
            nb = n_chunks - 1 - n
            return slice(n * c, (n + 1) * c), slice(nb * c, (nb + 1) * c)

        def prep(n):
            rf, rb = rows_of(n)
            qkf, qkb = qkf_ref[0, rf, :], qkb_ref[0, rb, :]
            return (_gla_prep(qkf[:, :kw], qkf[:, kw:], laf_ref[0, rf, :], False, True),
                    _gla_prep(qkb[:, :kw], qkb[:, kw:], lab_ref[0, rb, :], True, True))

        sf, sb = sf_scr[...], sb_scr[...]
        ready = {n: prep(n) for n in range(min(GLA_PREP_AHEAD, n_chunks))}
        for n in range(n_chunks):
            rf, rb = rows_of(n)
            pf, pb = ready.pop(n)
            of, sf = _gla_apply(pf, vf_ref[0, rf, :], sf)
            emit(of, gf_ref, rf, i * tile + rf.start)
            ob, sb = _gla_apply(pb, vb_ref[0, rb, :], sb)
            emit(ob, gb_ref, rb, (n_tiles - 1 - i) * tile + rb.start)
            if n + GLA_PREP_AHEAD < n_chunks:
                ready[n + GLA_PREP_AHEAD] = prep(n + GLA_PREP_AHEAD)
        sf_scr[...] = sf
        sb_scr[...] = sb

    @pl.when(i < n_tiles // 2)
    def _():
        sweep(False)

    @pl.when(i >= n_tiles // 2)
    def _():
        sweep(True)


def _gla(qk, v, la, g, qk_c, v_c, la_c, head_w):
    b, t, _ = qk.shape
    tile = GLA_TILE
    n_tiles = t // tile
    kw, vw = GLA_KEY_WIDTH, GLA_VAL_WIDTH
    fwd = lambda w, col: pl.BlockSpec((1, tile, w), lambda bi, i: (bi, i, col))
    bwd = lambda w, col: pl.BlockSpec((1, tile, w), lambda bi, i: (bi, n_tiles - 1 - i, col))
    ctx = lambda w: pl.BlockSpec((1, CTX_LEN, w), lambda bi, i: (bi, 0, 0))
    half = n_tiles // 2
    gate_fwd = pl.BlockSpec((1, tile, vw), lambda bi, i: (bi, jnp.maximum(i, half), 0))
    gate_bwd = pl.BlockSpec((1, tile, vw), lambda bi, i: (bi, n_tiles - 1 - jnp.maximum(i, half), 0))
    return pl.pallas_call(
        functools.partial(_gla_kernel, n_tiles=n_tiles, tile=tile),
        out_shape=jax.ShapeDtypeStruct((b, t, vw), BF16),
        grid=(b, n_tiles),
        in_specs=[fwd(2 * kw, 0), fwd(vw, 0), fwd(kw, 0), gate_fwd,
                  bwd(2 * kw, 0), bwd(vw, 0), bwd(kw, 1), gate_bwd,
                  pl.BlockSpec((1, CTX_LEN, kw), lambda bi, i: (bi, 0, 1)), ctx(vw), ctx(2 * kw),
                  pl.BlockSpec((1, vw), lambda bi, i: (0, 0))],
        out_specs=pl.BlockSpec((1, t, vw), lambda bi, i: (bi, 0, 0)),
        scratch_shapes=[pltpu.VMEM((GLA_DV, kw), F32), pltpu.VMEM((GLA_DV, kw), F32), pltpu.VMEM((t, vw), F32)],
        compiler_params=_cparams("parallel", "arbitrary"),
        name="gla",
    )(qk, v, la, g, qk, v, la, g, qk_c, v_c, la_c, head_w)


def kernel(x, c, ctx, c_ctx, ada_w, ada_b, norm_mix_w, norm_ffn_w, ffn_w_gate, ffn_w_up, ffn_w_down,
           ab_w_in, ab_w_out, ab_sgu_norm_w, ab_sgu_w, ab_sgu_b, ab_rel_bias,
           cd_w_in, cd_w_out, cd_decay_w_fwd, cd_decay_b_fwd, cd_decay_w_bwd, cd_decay_b_bwd, cd_head_norm_w,
           final_norm_w):
    bsz, t, d = x.shape
    n_ctx = ctx.shape[1]
    tpb = t // ROW_TILE
    x2 = x.reshape(bsz * t, d)
    ctx2 = ctx.reshape(bsz * n_ctx, d)
    row = lambda w: w.reshape(1, -1)

    cc = jnp.concatenate([c, c_ctx[None, :], jnp.zeros((16 - bsz - 1, d), F32)], axis=0)
    mod = _ada(cc, ada_w, ada_b)
    mod_x = [mod[i, :bsz].reshape(bsz, 6, d) for i in range(2)]
    mod_c = [mod[i, bsz:bsz + 1].reshape(1, 6, d) for i in range(2)]
    ffn = [(ffn_w_gate[i].astype(BF16), ffn_w_up[i].astype(BF16), ffn_w_down[i].astype(BF16)) for i in range(2)]

    w_in = ab_w_in[0].astype(BF16)
    sgu = (row(ab_sgu_norm_w[0]), ab_sgu_w[0].astype(BF16),
           jnp.broadcast_to(ab_sgu_b[0][:, :, None], (A_GROUPS, A_CHUNK, LANES)))
    a_l, q_l, k_l, v_l = _ab_in(x2, mod_x[0], tpb, ROW_TILE, row(norm_mix_w[0]), w_in, *sgu)
    a_c, q_c, k_c, v_c = _ab_in(ctx2, mod_c[0], None, n_ctx, row(norm_mix_w[0]), w_in, *sgu)
    seq = lambda z, n: z.reshape(bsz, n, z.shape[-1])
    b_l = _na(seq(q_l, t), seq(k_l, t), seq(v_l, t), seq(k_c, n_ctx), seq(v_c, n_ctx), _na_bias_table(ab_rel_bias[0]))
    b_c = _ctx_attn(seq(q_c, n_ctx), seq(k_c, n_ctx), seq(v_c, n_ctx))
    wo = ab_w_out[0].astype(BF16)
    x2 = _out_ffn(a_l, b_l.reshape(bsz * t, -1), x2, mod_x[0], tpb, ROW_TILE, wo, row(norm_ffn_w[0]), *ffn[0],
                  row(final_norm_w), False)
    ctx2 = _out_ffn(a_c, b_c.reshape(bsz * n_ctx, -1), ctx2, mod_c[0], None, n_ctx, wo, row(norm_ffn_w[0]),
                    *ffn[0], row(final_norm_w), False)

    w_in = cd_w_in[0]
    w_main = w_in[:, :O_A].astype(BF16)
    w_a = jnp.pad(w_in[:, O_A:], ((0, 0), (0, LANES - 2 * GLA_LOW_RANK))).astype(BF16)
    w_dec = jnp.zeros((LANES, 2 * GLA_KEY_WIDTH), F32)
    w_dec = w_dec.at[:GLA_LOW_RANK, :GLA_KEY_WIDTH].set(cd_decay_w_fwd[0])
    w_dec = w_dec.at[GLA_LOW_RANK:2 * GLA_LOW_RANK, GLA_KEY_WIDTH:].set(cd_decay_w_bwd[0]).astype(BF16)
    b_dec = jnp.concatenate([cd_decay_b_fwd[0], cd_decay_b_bwd[0]]).reshape(1, -1)
    cos, sin = _rope_tables(t)
    f_l, qk_l, v_l, g_l, la_l = _cd_in(x2, mod_x[1], tpb, ROW_TILE, row(norm_mix_w[1]), w_main, w_a, w_dec, b_dec,
                                      cos, sin, True)
    _, qk_c, v_c, _, la_c = _cd_in(ctx2, mod_c[1], None, n_ctx, row(norm_mix_w[1]), w_main, w_a, w_dec, b_dec,
                                   cos, sin, False)
    fm = _fourier(f_l.reshape(bsz, -1, FN_WIDTH), _fft_tables(t))
    head_w = jnp.tile(cd_head_norm_w[0], GLA_HEADS).reshape(1, -1)
    go = _gla(seq(qk_l, t), seq(v_l, t), seq(la_l, t), seq(g_l, t),
              seq(qk_c, n_ctx), seq(v_c, n_ctx), seq(la_c, n_ctx), head_w)
    out = _out_ffn(fm.reshape(bsz * t, -1), go.reshape(bsz * t, -1), x2, mod_x[1], tpb, ROW_TILE,
                   cd_w_out[0].astype(BF16), row(norm_ffn_w[1]), *ffn[1], row(final_norm_w), True)
    return out.reshape(bsz, t, d)
```

```python
import functools

import numpy as np
import jax
import jax.numpy as jnp
from jax import lax
from jax.experimental import pallas as pl
from jax.experimental.pallas import tpu as pltpu

F32 = jnp.float32
BF16 = jnp.bfloat16

D_MODEL = 1024
CTX_LEN = 256
GRID_W = 64
A_WIDTH = 512
A_GROUPS = 4
A_CHUNK = 128
NA_HEADS = 8
NA_HEAD_DIM = 64
NA_WIDTH = 512
NA_WIN_ROWS = 8
NA_WIN_COLS = 16
O_NA_Q = 2 * A_WIDTH
FN_WIDTH = 512
FN_GROUP_CH = 128
GLA_HEADS = 4
GLA_VAL_WIDTH = 512
GLA_KEY_WIDTH = 256
GLA_DK = 64
GLA_DV = 128
GLA_LOW_RANK = 16
GLA_GATE_TEMP = 16.0
GLA_CHUNK = 64
O_Q = FN_WIDTH
O_V = O_Q + 2 * GLA_KEY_WIDTH
O_G = O_V + GLA_VAL_WIDTH
O_A = O_G + GLA_VAL_WIDTH
FFN_HIDDEN = 2816
ROPE_BASE = 10000.0
RMS_EPS = 1e-6
NEG_INF = -1e30

LANES = 128
VMEM_LIMIT_BYTES = 56 * 2 ** 20

ROW_TILE = 512
ROW_SUBTILE = 256
FFN_SPLIT = 2
GLA_TILE = 512
NA_STAGE_LAG = 2
FFT_UNROLL = 32
FFT_PITCH = GRID_W + 8
FFT_PITCH2 = 2 * GRID_W + 8
GLA_PREP_AHEAD = 2


def _cparams(*sem):
    return pltpu.CompilerParams(dimension_semantics=sem, vmem_limit_bytes=VMEM_LIMIT_BYTES)


def _resident(shape):
    nd = len(shape)
    return pl.BlockSpec(shape, lambda *_: (0,) * nd, pipeline_mode=pl.Buffered(1))


def _dot(a, b):
    return jnp.dot(a, b, preferred_element_type=F32)


def _dot_nt(a, b):
    return lax.dot_general(a, b, (((1,), (1,)), ((), ())), preferred_element_type=F32)


def _dot_tn(a, b):
    return lax.dot_general(a, b, (((0,), (0,)), ((), ())), preferred_element_type=F32)


def _subtiles(n_rows):
    step = min(ROW_SUBTILE, n_rows)
    return [slice(r, r + step) for r in range(0, n_rows, step)]


def _rms(x, w):
    return x * lax.rsqrt(jnp.mean(x * x, axis=-1, keepdims=True) + RMS_EPS) * w


def _rms_mod(x, w, shift, scale):
    return x * lax.rsqrt(jnp.mean(x * x, axis=-1, keepdims=True) + RMS_EPS) * (w * (1.0 + scale)) + shift


def _ada_kernel(c_ref, w_ref, b_ref, o_ref):
    c = c_ref[...]
    s = c * jax.nn.sigmoid(c)
    o_ref[0] = _dot(s.astype(BF16), w_ref[0].astype(BF16)) + b_ref[0]


def _ada(cc, ada_w, ada_b):
    depth, d, n = ada_w.shape
    r = cc.shape[0]
    tn = 1536
    return pl.pallas_call(
        _ada_kernel,
        out_shape=jax.ShapeDtypeStruct((depth, r, n), F32),
        grid=(depth, n // tn),
        in_specs=[pl.BlockSpec((r, d), lambda i, j: (0, 0)),
                  pl.BlockSpec((1, d, tn), lambda i, j: (i, 0, j)),
                  pl.BlockSpec((1, 1, tn), lambda i, j: (i, 0, j))],
        out_specs=pl.BlockSpec((1, r, tn), lambda i, j: (i, 0, j)),
        compiler_params=_cparams("arbitrary", "arbitrary"),
        name="ada_mod",
    )(cc, ada_w, ada_b.reshape(depth, 1, n))


def _mod_spec(tiles_per_batch):
    if tiles_per_batch is None:
        return pl.BlockSpec((1, 6, D_MODEL), lambda i: (0, 0, 0))
    return pl.BlockSpec((1, 6, D_MODEL), lambda i: (i // tiles_per_batch, 0, 0))


def _ab_in_kernel(x_ref, mod_ref, nw_ref, w_ref, snw_ref, sguw_ref, sgub_ref,
                  a_ref, q_ref, k_ref, v_ref, *, tm):
    tiles = _subtiles(tm)
    hs = [_rms_mod(x_ref[rows, :], nw_ref[...], mod_ref[0, 0:1, :], mod_ref[0, 1:2, :]).astype(BF16)
          for rows in tiles]
    uvs = [_dot(h, w_ref[:, :O_NA_Q]) for h in hs]
    for rows, h, uv in zip(tiles, hs, uvs):
        qkv = _dot(h, w_ref[:, O_NA_Q:])
        q_ref[rows, :] = (qkv[:, :NA_WIDTH] * NA_HEAD_DIM ** -0.5).astype(BF16)
        k_ref[rows, :] = qkv[:, NA_WIDTH:2 * NA_WIDTH].astype(BF16)
        v_ref[rows, :] = qkv[:, 2 * NA_WIDTH:].astype(BF16)
        uv = jax.nn.gelu(uv)
        u = uv[:, :A_WIDTH]
        v = _rms(uv[:, A_WIDTH:], snw_ref[...]).astype(BF16)
        for ci in range(uv.shape[0] // A_CHUNK):
            rs = slice(ci * A_CHUNK, (ci + 1) * A_CHUNK)
            out_rows = slice(rows.start + rs.start, rows.start + rs.stop)
            for g in range(A_GROUPS):
                cs = slice(g * LANES, (g + 1) * LANES)
                gate = _dot(sguw_ref[g], v[rs, cs]) + sgub_ref[g]
                a_ref[out_rows, cs] = (u[rs, cs] * gate).astype(BF16)


def _ab_in(x2, mod, tiles_per_batch, tm, nw, w_bf, snw, sguw_bf, sgub_exp):
    m = x2.shape[0]
    n_in = w_bf.shape[1]
    row = lambda w: pl.BlockSpec((tm, w), lambda i: (i, 0))
    out = jax.ShapeDtypeStruct((m, A_WIDTH), BF16)
    return pl.pallas_call(
        functools.partial(_ab_in_kernel, tm=tm),
        out_shape=(out, out, out, out),
        grid=(m // tm,),
        in_specs=[row(D_MODEL), _mod_spec(tiles_per_batch), _resident((1, D_MODEL)),
                  _resident((D_MODEL, n_in)), _resident((1, A_WIDTH)),
                  _resident((A_GROUPS, A_CHUNK, A_CHUNK)), _resident((A_GROUPS, A_CHUNK, LANES))],
        out_specs=(row(A_WIDTH),) * 4,
        compiler_params=_cparams("parallel"),
        name="ab_in",
    )(x2, mod, nw, w_bf, snw, sguw_bf, sgub_exp)


def _na_bias_table(rel_bias):
    w, nrow = GRID_W, 2 * NA_WIN_ROWS - 1
    col = np.arange(w)
    col_start = np.clip(col - NA_WIN_COLS // 2, 0, w - NA_WIN_COLS)
    col_mask = (col[None, :] >= col_start[:, None]) & (col[None, :] < col_start[:, None] + NA_WIN_COLS)
    lo = w - NA_WIN_COLS
    r_ext = jnp.pad(rel_bias, ((0, 0), (0, 0), (lo, 2 * w - lo - (2 * NA_WIN_COLS - 1))))
    flat = jnp.tile(r_ext, (1, 1, w))[:, :, :w * (2 * w - 1)]
    toe = flat.reshape(NA_HEADS, nrow, w, 2 * w - 1)[:, :, :, w - 1:]
    toe = jnp.where(col_mask, toe, NEG_INF).reshape(NA_HEADS // 2, 2, nrow, w, w)
    strips = [jnp.concatenate([toe[:, :, j - o + NA_WIN_ROWS - 1] for j in range(NA_WIN_ROWS)], axis=-1)
              for o in range(NA_WIN_ROWS)]
    return jnp.stack([s.reshape(NA_HEADS // 2, 2 * w, NA_WIN_ROWS * w) for s in strips], axis=1)


def _stack_heads(qb):
    first = lax.broadcasted_iota(jnp.int32, (1, LANES), 1) < NA_HEAD_DIM
    zero = jnp.zeros_like(qb)
    return jnp.concatenate([jnp.where(first, qb, zero), jnp.where(first, zero, qb)], axis=0)


def _unstack_heads(r):
    n = r.shape[0] // 2
    first = lax.broadcasted_iota(jnp.int32, (1, LANES), 1) < NA_HEAD_DIM
    return jnp.where(first, r[:n], r[n:])


def _na_kernel(q_ref, k_ref, v_ref, kc_ref, vc_ref, tab_ref, o_ref, kt_scr, kct_scr, *, rows):
    t = rows * GRID_W
    kt_scr[0] = k_ref[0].T
    kt_scr[1, :, :t - GRID_W] = k_ref[0, GRID_W:, :].T
    kct_scr[...] = kc_ref[0].T
    kct = kct_scr[...]
    vc = vc_ref[0]
    win = NA_WIN_ROWS * GRID_W

    def window(r):
        return min(max(r - NA_WIN_ROWS // 2, 0), rows - NA_WIN_ROWS)

    def scores(r):
        start = window(r)
        odd = start % 2
        kt0 = (start - odd) * GRID_W
        qs = _stack_heads(q_ref[0, r * GRID_W:(r + 1) * GRID_W, :])
        return _dot(qs, kt_scr[odd, :, kt0:kt0 + win]) + tab_ref[0, r - start], _dot(qs, kct)

    def softmax(s):
        s_nb, s_cx = s
        m = jnp.maximum(jnp.max(s_nb, axis=-1, keepdims=True), jnp.max(s_cx, axis=-1, keepdims=True))
        return jnp.exp(s_nb - m).astype(BF16), jnp.exp(s_cx - m).astype(BF16)

    ones_nb = jnp.ones((win, LANES), BF16)
    vc_aug = jnp.concatenate([vc, jnp.ones((vc.shape[0], LANES), BF16)], axis=1)

    def values(r, p):
        e_nb, e_cx = p
        k0 = window(r) * GRID_W
        vb_aug = jnp.concatenate([v_ref[0, k0:k0 + win, :], ones_nb], axis=1)
        acc = _dot(e_nb, vb_aug) + _dot(e_cx, vc_aug)
        o_ref[0, r * GRID_W:(r + 1) * GRID_W, :] = _unstack_heads(acc[:, :LANES] / acc[:, LANES:]).astype(BF16)

    lag = NA_STAGE_LAG
    s_vals, p_vals = {}, {}
    for r in range(rows + 2 * lag):
        if 0 <= r - 2 * lag < rows:
            values(r - 2 * lag, p_vals.pop(r - 2 * lag))
        if 0 <= r - lag < rows:
            p_vals[r - lag] = softmax(s_vals.pop(r - lag))
        if r < rows:
            s_vals[r] = scores(r)


def _na(q, k, v, kc, vc, table):
    b, t, _ = q.shape
    rows = t // GRID_W
    lat = pl.BlockSpec((1, t, LANES), lambda j, i: (i, 0, j))
    ctx = pl.BlockSpec((1, CTX_LEN, LANES), lambda j, i: (i, 0, j))
    return pl.pallas_call(
        functools.partial(_na_kernel, rows=rows),
        out_shape=jax.ShapeDtypeStruct((b, t, NA_WIDTH), BF16),
        grid=(NA_HEADS // 2, b),
        in_specs=[lat, lat, lat, ctx, ctx,
                  pl.BlockSpec((1, NA_WIN_ROWS, 2 * GRID_W, NA_WIN_ROWS * GRID_W), lambda j, i: (j, 0, 0, 0))],
        out_specs=lat,
        scratch_shapes=[pltpu.VMEM((2, LANES, t), BF16), pltpu.VMEM((LANES, CTX_LEN), BF16)],
        compiler_params=_cparams("parallel", "parallel"),
        name="na_attn",
    )(q, k, v, kc, vc, table)


def _ctx_attn_kernel(q_ref, k_ref, v_ref, o_ref):
    s = _dot_nt(_stack_heads(q_ref[0]), k_ref[0])
    e = jnp.exp(s - jnp.max(s, axis=-1, keepdims=True))
    acc = _dot(e.astype(BF16), v_ref[0])
    o_ref[0] = _unstack_heads(acc / jnp.sum(e, axis=-1, keepdims=True)).astype(BF16)


def _ctx_attn(q, k, v):
    b = q.shape[0]
    spec = pl.BlockSpec((1, CTX_LEN, LANES), lambda i, j: (i, 0, j))
    return pl.pallas_call(
        _ctx_attn_kernel,
        out_shape=jax.ShapeDtypeStruct((b, CTX_LEN, NA_WIDTH), BF16),
        grid=(b, NA_HEADS // 2),
        in_specs=[spec, spec, spec],
        out_specs=spec,
        compiler_params=_cparams("parallel", "parallel"),
        name="ctx_attn",
    )(q, k, v)


def _out_ffn_kernel(a_ref, b_ref, x_ref, mod_ref, wo_ref, nw_ref, wg_ref, wu_ref, wd_ref, fnw_ref, o_ref,
                    *, final):
    half = wo_ref.shape[0] // 2
    piece = FFN_HIDDEN // FFN_SPLIT
    tiles = _subtiles(x_ref.shape[0])
    ys = [_dot(a_ref[rows, :], wo_ref[:half, :]) + _dot(b_ref[rows, :], wo_ref[half:, :]) for rows in tiles]
    x1s = [x_ref[rows, :] + mod_ref[0, 2:3, :] * y for rows, y in zip(tiles, ys)]
    hs = [_rms_mod(x1, nw_ref[...], mod_ref[0, 3:4, :], mod_ref[0, 4:5, :]).astype(BF16) for x1 in x1s]
    accs = [None] * len(tiles)
    for j in range(FFN_SPLIT):
        cs = slice(j * piece, (j + 1) * piece)
        gs = [_dot(h, wg_ref[:, cs]) for h in hs]
        acts = [(g * jax.nn.sigmoid(g) * _dot(h, wu_ref[:, cs])).astype(BF16) for g, h in zip(gs, hs)]
        parts = [_dot(act, wd_ref[cs, :]) for act in acts]
        accs = [part if acc is None else acc + part for acc, part in zip(accs, parts)]
    for rows, x1, acc in zip(tiles, x1s, accs):
        x2 = x1 + mod_ref[0, 5:6, :] * acc
        o_ref[rows, :] = _rms(x2, fnw_ref[...]) if final else x2


def _out_ffn(a, b, x2, mod, tiles_per_batch, tm, wo_bf, nw, wg_bf, wu_bf, wd_bf, fnw, final):
    m = x2.shape[0]
    row = lambda w: pl.BlockSpec((tm, w), lambda i: (i, 0))
    return pl.pallas_call(
        functools.partial(_out_ffn_kernel, final=final),
        out_shape=jax.ShapeDtypeStruct((m, D_MODEL), F32),
        grid=(m // tm,),
        in_specs=[row(a.shape[1]), row(b.shape[1]), row(D_MODEL), _mod_spec(tiles_per_batch),
                  _resident(wo_bf.shape), _resident((1, D_MODEL)), _resident(wg_bf.shape),
                  _resident(wu_bf.shape), _resident(wd_bf.shape), _resident((1, D_MODEL))],
        out_specs=row(D_MODEL),
        compiler_params=_cparams("parallel"),
        name="out_ffn_final" if final else "out_ffn",
    )(a, b, x2, mod, wo_bf, nw, wg_bf, wu_bf, wd_bf, fnw)


def _rope_tables(t):
    half = GLA_DK // 4
    inv_freq = ROPE_BASE ** (-np.arange(half, dtype=np.float64) / half)
    pos = np.arange(t)
    ang_r = (pos // GRID_W)[:, None] * inv_freq[None, :]
    ang_c = (pos % GRID_W)[:, None] * inv_freq[None, :]
    cos = np.concatenate([np.cos(ang_r)] * 2 + [np.cos(ang_c)] * 2, axis=1)
    sin = np.concatenate([-np.sin(ang_r), np.sin(ang_r), -np.sin(ang_c), np.sin(ang_c)], axis=1)
    return (jnp.asarray(np.tile(cos, (1, 2)), F32), jnp.asarray(np.tile(sin, (1, 2)), F32))


def _log_sigmoid(z):
    return jnp.minimum(z, 0.0) - jnp.log(1.0 + jnp.exp(-jnp.abs(z)))


def _cd_in_kernel(x_ref, mod_ref, nw_ref, w_ref, wa_ref, wdec_ref, bdec_ref, cos_ref, sin_ref,
                  f_ref, qk_ref, v_ref, g_ref, la_ref, *, rope):
    for rows in [slice(0, x_ref.shape[0])]:
        h = _rms_mod(x_ref[rows, :], nw_ref[...], mod_ref[0, 0:1, :], mod_ref[0, 1:2, :]).astype(BF16)
        f = _dot(h, w_ref[:, :O_Q])
        for a in range(f.shape[0] // GRID_W):
            r0 = (rows.start // GRID_W + a) * FFT_PITCH
            f_ref[r0:r0 + GRID_W, :] = f[a * GRID_W:(a + 1) * GRID_W]
            f_ref[r0 + GRID_W:r0 + FFT_PITCH, :] = jnp.zeros((FFT_PITCH - GRID_W, f.shape[1]), F32)
        qk = _dot(h, w_ref[:, O_Q:O_V])
        if rope:
            reps = qk.shape[1] // LANES
            cos = jnp.concatenate([cos_ref[rows, :]] * reps, axis=1)
            sin = jnp.concatenate([sin_ref[rows, :]] * reps, axis=1)
            w = qk.shape[1]
            q16 = GLA_DK // 4
            first = (lax.broadcasted_iota(jnp.int32, (1, w), 1) % (2 * q16)) < q16
            partner = jnp.where(first, pltpu.roll(qk, w - q16, 1), pltpu.roll(qk, q16, 1))
            qk = qk * cos + partner * sin
        is_q = lax.broadcasted_iota(jnp.int32, (1, qk.shape[1]), 1) < GLA_KEY_WIDTH
        qk_ref[rows, :] = jnp.where(is_q, qk * GLA_DK ** -0.5, qk)
        v_ref[rows, :] = _dot(h, w_ref[:, O_V:O_G]).astype(BF16)
        g_ref[rows, :] = _dot(h, w_ref[:, O_G:O_A])
        a = _dot(h, wa_ref[...]).astype(BF16)
        la_ref[rows, :] = _log_sigmoid(_dot(a, wdec_ref[...]) + bdec_ref[...]) * (1.0 / GLA_GATE_TEMP)


def _cd_in(x2, mod, tiles_per_batch, tm, nw, w_bf, wa_bf, wdec_bf, bdec, cos, sin, rope):
    m = x2.shape[0]
    row = lambda w: pl.BlockSpec((tm, w), lambda i: (i, 0))
    if rope:
        tab = pl.BlockSpec((tm, LANES), lambda i: (i % tiles_per_batch, 0))
    else:
        tab = pl.BlockSpec((tm, LANES), lambda i: (0, 0))
    w512 = 2 * GLA_KEY_WIDTH
    pad = lambda n: n // GRID_W * FFT_PITCH
    outs = (jax.ShapeDtypeStruct((pad(m), FN_WIDTH), F32), jax.ShapeDtypeStruct((m, w512), F32),
            jax.ShapeDtypeStruct((m, GLA_VAL_WIDTH), BF16), jax.ShapeDtypeStruct((m, GLA_VAL_WIDTH), F32),
            jax.ShapeDtypeStruct((m, w512), F32))
    return pl.pallas_call(
        functools.partial(_cd_in_kernel, rope=rope),
        out_shape=outs,
        grid=(m // tm,),
        in_specs=[row(D_MODEL), _mod_spec(tiles_per_batch), _resident((1, D_MODEL)),
                  _resident(w_bf.shape), _resident(wa_bf.shape), _resident(wdec_bf.shape),
                  _resident((1, w512)), tab, tab],
        out_specs=(pl.BlockSpec((pad(tm), FN_WIDTH), lambda i: (i, 0)),
                   row(w512), row(GLA_VAL_WIDTH), row(GLA_VAL_WIDTH), row(w512)),
        compiler_params=_cparams("parallel"),
        name="cd_in_rope" if rope else "cd_in_ctx",
    )(x2, mod, nw, w_bf, wa_bf, wdec_bf, bdec, cos, sin)


def _fft_tables(t):
    n1 = GRID_W
    assert t == n1 * n1
    a = np.arange(n1)
    ang1 = 2 * np.pi * np.outer(a, a) / n1
    cs = np.concatenate([np.cos(ang1), -np.sin(ang1)], axis=0)
    kap = a[:, None, None] + n1 * a[None, :, None]
    ang2 = 2 * np.pi * kap * a[None, None, :] / t
    gr, gi = np.cos(ang2), -np.sin(ang2)
    g = np.concatenate([np.concatenate([gr, -gi], axis=2), np.concatenate([gi, gr], axis=2)], axis=1)
    c = np.arange(FN_GROUP_CH)
    ang3 = 2 * np.pi * np.outer(c, c) / FN_GROUP_CH
    norm = 1.0 / np.sqrt(t * FN_GROUP_CH)
    return (jnp.asarray(cs, F32), jnp.asarray(g, F32),
            jnp.asarray(np.cos(ang3) * norm, F32), jnp.asarray(np.sin(ang3) * norm, F32))


def _fft_kernel(x_ref, cs_ref, g_ref, cc_ref, sc_ref, o_ref, p_scr, zr_scr, zi_scr):
    n1 = GRID_W
    cs = cs_ref[...].astype(BF16)

    def stage1(b, carry):
        xb = x_ref[0, pl.ds(b, n1, stride=FFT_PITCH), :].astype(BF16)
        p_scr[pl.ds(pl.multiple_of(b * FFT_PITCH2, 8), 2 * n1), :] = _dot(cs, xb)
        return carry

    lax.fori_loop(0, n1, stage1, 0, unroll=FFT_UNROLL)

    def stage2(k1, carry):
        pr = p_scr[pl.ds(k1, n1, stride=FFT_PITCH2), :]
        pi = p_scr[pl.ds(n1 + k1, n1, stride=FFT_PITCH2), :]
        z = _dot(g_ref[k1].astype(BF16), jnp.concatenate([pr, pi], axis=0).astype(BF16))
        rows = pl.ds(pl.multiple_of(k1 * FFT_PITCH, 8), n1)
        zr_scr[rows, :] = z[:n1]
        zi_scr[rows, :] = z[n1:]
        return carry

    lax.fori_loop(0, n1, stage2, 0, unroll=FFT_UNROLL)

    ccsc = jnp.concatenate([cc_ref[...], sc_ref[...]], axis=0).astype(BF16)

    def stage3(j, carry):
        parts = []
        for u in range(FFT_UNROLL):
            k2 = j * FFT_UNROLL + u
            parts.append(jnp.concatenate([zr_scr[pl.ds(k2, n1, stride=FFT_PITCH), :],
                                          zi_scr[pl.ds(k2, n1, stride=FFT_PITCH), :]], axis=1))
        out = _dot(jnp.concatenate(parts, axis=0).astype(BF16), ccsc)
        o_ref[0, pl.ds(pl.multiple_of(j * FFT_UNROLL * n1, FFT_UNROLL * n1), FFT_UNROLL * n1), :] = out.astype(BF16)
        return carry

    lax.fori_loop(0, n1 // FFT_UNROLL, stage3, 0)


def _fourier(f_padded, tables):
    b, tp, w = f_padded.shape
    t = tp // FFT_PITCH * GRID_W
    cw = FN_GROUP_CH
    cs, g, cc, sc = tables
    return pl.pallas_call(
        _fft_kernel,
        out_shape=jax.ShapeDtypeStruct((b, t, w), BF16),
        grid=(b, w // cw),
        in_specs=[pl.BlockSpec((1, tp, cw), lambda i, j: (i, 0, j)),
                  _resident(cs.shape), _resident(g.shape), _resident(cc.shape), _resident(sc.shape)],
        out_specs=pl.BlockSpec((1, t, cw), lambda i, j: (i, 0, j)),
        scratch_shapes=[pltpu.VMEM((GRID_W * FFT_PITCH2, cw), F32), pltpu.VMEM((tp, cw), F32),
                        pltpu.VMEM((tp, cw), F32)],
        compiler_params=_cparams("parallel", "parallel"),
        name="fourier_mix",
    )(f_padded, cs, g, cc, sc)


def _block_diag(x, block):
    head = lax.broadcasted_iota(jnp.int32, (1, x.shape[1]), 1) // block
    return jnp.concatenate([jnp.where(head == h, x, jnp.zeros_like(x)) for h in range(GLA_HEADS)], axis=0)


def _cum_rows(x, reverse):
    n, w = x.shape
    row = lax.broadcasted_iota(jnp.int32, (n, 1), 0)
    s = 1
    while s < n:
        if s < 8:
            if reverse:
                shifted = jnp.where(row < n - s, pltpu.roll(x, n - s, 0), 0.0)
            else:
                shifted = jnp.where(row >= s, pltpu.roll(x, s, 0), 0.0)
        else:
            pad = jnp.zeros((s, w), F32)
            shifted = jnp.concatenate([x[s:], pad] if reverse else [pad, x[:n - s]], axis=0)
        x = x + shifted
        s *= 2
    return x


def _gla_prep(q, k, la, reverse, want_out):
    c = GLA_CHUNK
    b = _cum_rows(la, reverse)
    b_last = b[0:1, :] if reverse else b[c - 1:c, :]
    kd = (k * jnp.exp(b_last - b)).astype(BF16)
    decay = jnp.exp(b_last)
    if not want_out:
        return kd, decay
    qe = (q * jnp.exp(b)).astype(BF16)
    ke = _block_diag(k * jnp.exp(-b), GLA_DK).astype(BF16)
    l = lax.broadcasted_iota(jnp.int32, (c, GLA_HEADS * c), 0)
    m = lax.broadcasted_iota(jnp.int32, (c, GLA_HEADS * c), 1) % c
    att = jnp.where((l <= m) if reverse else (l >= m), _dot_nt(qe, ke), 0.0).astype(BF16)
    return kd, decay, qe, att


def _gla_state_step(kd, decay, v, state):
    v_rows = jnp.concatenate([v[:, h * GLA_DV:(h + 1) * GLA_DV] for h in range(GLA_HEADS)], axis=0)
    return state * decay + _dot_tn(v_rows, _block_diag(kd, GLA_DK))


def _gla_apply(prep, v, state):
    c = GLA_CHUNK
    kd, decay, qe, att = prep
    inter = _dot_nt(_block_diag(qe, GLA_DK), state.astype(BF16))
    out = _dot(att, _block_diag(v, GLA_DV)) + jnp.concatenate(
        [inter[h * c:(h + 1) * c] for h in range(GLA_HEADS)], axis=1)
    return out, _gla_state_step(kd, decay, v, state)


def _gla_kernel(qkf_ref, vf_ref, laf_ref, gf_ref, qkb_ref, vb_ref, lab_ref, gb_ref,
                kc_ref, vc_ref, lac_ref, hw_ref, o_ref, sf_scr, sb_scr, acc_scr, *, n_tiles, tile):
    c = GLA_CHUNK
    i = pl.program_id(1)
    kw = GLA_KEY_WIDTH
    n_chunks = tile // c

    @pl.when(i == 0)
    def _():
        n_ctx = kc_ref.shape[1] // c
        sf = jnp.zeros((GLA_DV, kw), F32)
        sb = jnp.zeros((GLA_DV, kw), F32)
        for n in range(n_ctx):
            rf = slice(n * c, (n + 1) * c)
            rb = slice((n_ctx - 1 - n) * c, (n_ctx - n) * c)
            sf = _gla_state_step(*_gla_prep(None, kc_ref[0, rf, :], lac_ref[0, rf, :kw], False, False),
                                 vc_ref[0, rf, :], sf)
            sb = _gla_state_step(*_gla_prep(None, kc_ref[0, rb, :], lac_ref[0, rb, kw:], True, False),
                                 vc_ref[0, rb, :], sb)
        sf_scr[...] = sf
        sb_scr[...] = sb

    hw = hw_ref[...]

    def finish(o, g):
        parts = []
        for h in range(GLA_HEADS):
            oh = o[:, h * GLA_DV:(h + 1) * GLA_DV]
            parts.append(oh * lax.rsqrt(jnp.mean(oh * oh, axis=-1, keepdims=True) + RMS_EPS))
        return (jnp.concatenate(parts, axis=1) * hw * (g * jax.nn.sigmoid(g))).astype(BF16)

    def sweep(second_pass):
        def emit(o, g_ref, rloc, row0):
            rows = pl.ds(pl.multiple_of(row0, c), c)
            if second_pass:
                o_ref[0, rows, :] = finish(o + acc_scr[rows, :], g_ref[0, rloc, :])
            else:
                acc_scr[rows, :] = o

        def rows_of(n):
            nb = n_chunks - 1 - n
            return slice(n * c, (n + 1) * c), slice(nb * c, (nb + 1) * c)

        def prep(n):
            rf, rb = rows_of(n)
            qkf, qkb = qkf_ref[0, rf, :], qkb_ref[0, rb, :]
            return (_gla_prep(qkf[:, :kw], qkf[:, kw:], laf_ref[0, rf, :], False, True),
                    _gla_prep(qkb[:, :kw], qkb[:, kw:], lab_ref[0, rb, :], True, True))

        sf, sb = sf_scr[...], sb_scr[...]
        ready = {n: prep(n) for n in range(min(GLA_PREP_AHEAD, n_chunks))}
        for n in range(n_chunks):
            rf, rb = rows_of(n)
            pf, pb = ready.pop(n)
            of, sf = _gla_apply(pf, vf_ref[0, rf, :], sf)
            emit(of, gf_ref, rf, i * tile + rf.start)
            ob, sb = _gla_apply(pb, vb_ref[0, rb, :], sb)
            emit(ob, gb_ref, rb, (n_tiles - 1 - i) * tile + rb.start)
            if n + GLA_PREP_AHEAD < n_chunks:
                ready[n + GLA_PREP_AHEAD] = prep(n + GLA_PREP_AHEAD)
        sf_scr[...] = sf
        sb_scr[...] = sb

    @pl.when(i < n_tiles // 2)
    def _():
        sweep(False)

    @pl.when(i >= n_tiles // 2)
    def _():
        sweep(True)


def _gla(qk, v, la, g, qk_c, v_c, la_c, head_w):
    b, t, _ = qk.shape
    tile = GLA_TILE
    n_tiles = t // tile
    kw, vw = GLA_KEY_WIDTH, GLA_VAL_WIDTH
    fwd = lambda w, col: pl.BlockSpec((1, tile, w), lambda bi, i: (bi, i, col))
    bwd = lambda w, col: pl.BlockSpec((1, tile, w), lambda bi, i: (bi, n_tiles - 1 - i, col))
    ctx = lambda w: pl.BlockSpec((1, CTX_LEN, w), lambda bi, i: (bi, 0, 0))
    half = n_tiles // 2
    gate_fwd = pl.BlockSpec((1, tile, vw), lambda bi, i: (bi, jnp.maximum(i, half), 0))
    gate_bwd = pl.BlockSpec((1, tile, vw), lambda bi, i: (bi, n_tiles - 1 - jnp.maximum(i, half), 0))
    return pl.pallas_call(
        functools.partial(_gla_kernel, n_tiles=n_tiles, tile=tile),
        out_shape=jax.ShapeDtypeStruct((b, t, vw), BF16),
        grid=(b, n_tiles),
        in_specs=[fwd(2 * kw, 0), fwd(vw, 0), fwd(kw, 0), gate_fwd,
                  bwd(2 * kw, 0), bwd(vw, 0), bwd(kw, 1), gate_bwd,
                  pl.BlockSpec((1, CTX_LEN, kw), lambda bi, i: (bi, 0, 1)), ctx(vw), ctx(2 * kw),
                  pl.BlockSpec((1, vw), lambda bi, i: (0, 0))],
        out_specs=pl.BlockSpec((1, t, vw), lambda bi, i: (bi, 0, 0)),
        scratch_shapes=[pltpu.VMEM((GLA_DV, kw), F32), pltpu.VMEM((GLA_DV, kw), F32), pltpu.VMEM((t, vw), F32)],
        compiler_params=_cparams("parallel", "arbitrary"),
        name="gla",
    )(qk, v, la, g, qk, v, la, g, qk_c, v_c, la_c, head_w)


def kernel(x, c, ctx, c_ctx, ada_w, ada_b, norm_mix_w, norm_ffn_w, ffn_w_gate, ffn_w_up, ffn_w_down,
           ab_w_in, ab_w_out, ab_sgu_norm_w, ab_sgu_w, ab_sgu_b, ab_rel_bias,
           cd_w_in, cd_w_out, cd_decay_w_fwd, cd_decay_b_fwd, cd_decay_w_bwd, cd_decay_b_bwd, cd_head_norm_w,
           final_norm_w):
    bsz, t, d = x.shape
    n_ctx = ctx.shape[1]
    tpb = t // ROW_TILE
    x2 = x.reshape(bsz * t, d)
    ctx2 = ctx.reshape(bsz * n_ctx, d)
    row = lambda w: w.reshape(1, -1)

    cc = jnp.concatenate([c, c_ctx[None, :], jnp.zeros((16 - bsz - 1, d), F32)], axis=0)
    mod = _ada(cc, ada_w, ada_b)
    mod_x = [mod[i, :bsz].reshape(bsz, 6, d) for i in range(2)]
    mod_c = [mod[i, bsz:bsz + 1].reshape(1, 6, d) for i in range(2)]
    ffn = [(ffn_w_gate[i].astype(BF16), ffn_w_up[i].astype(BF16), ffn_w_down[i].astype(BF16)) for i in range(2)]

    w_in = ab_w_in[0].astype(BF16)
    sgu = (row(ab_sgu_norm_w[0]), ab_sgu_w[0].astype(BF16),
           jnp.broadcast_to(ab_sgu_b[0][:, :, None], (A_GROUPS, A_CHUNK, LANES)))
    a_l, q_l, k_l, v_l = _ab_in(x2, mod_x[0], tpb, ROW_TILE, row(norm_mix_w[0]), w_in, *sgu)
    a_c, q_c, k_c, v_c = _ab_in(ctx2, mod_c[0], None, n_ctx, row(norm_mix_w[0]), w_in, *sgu)
    seq = lambda z, n: z.reshape(bsz, n, z.shape[-1])
    b_l = _na(seq(q_l, t), seq(k_l, t), seq(v_l, t), seq(k_c, n_ctx), seq(v_c, n_ctx), _na_bias_table(ab_rel_bias[0]))
    b_c = _ctx_attn(seq(q_c, n_ctx), seq(k_c, n_ctx), seq(v_c, n_ctx))
    wo = ab_w_out[0].astype(BF16)
    x2 = _out_ffn(a_l, b_l.reshape(bsz * t, -1), x2, mod_x[0], tpb, ROW_TILE, wo, row(norm_ffn_w[0]), *ffn[0],
                  row(final_norm_w), False)
    ctx2 = _out_ffn(a_c, b_c.reshape(bsz * n_ctx, -1), ctx2, mod_c[0], None, n_ctx, wo, row(norm_ffn_w[0]),
                    *ffn[0], row(final_norm_w), False)

    w_in = cd_w_in[0]
    w_main = w_in[:, :O_A].astype(BF16)
    w_a = jnp.pad(w_in[:, O_A:], ((0, 0), (0, LANES - 2 * GLA_LOW_RANK))).astype(BF16)
    w_dec = jnp.zeros((LANES, 2 * GLA_KEY_WIDTH), F32)
    w_dec = w_dec.at[:GLA_LOW_RANK, :GLA_KEY_WIDTH].set(cd_decay_w_fwd[0])
    w_dec = w_dec.at[GLA_LOW_RANK:2 * GLA_LOW_RANK, GLA_KEY_WIDTH:].set(cd_decay_w_bwd[0]).astype(BF16)
    b_dec = jnp.concatenate([cd_decay_b_fwd[0], cd_decay_b_bwd[0]]).reshape(1, -1)
    cos, sin = _rope_tables(t)
    f_l, qk_l, v_l, g_l, la_l = _cd_in(x2, mod_x[1], tpb, ROW_TILE, row(norm_mix_w[1]), w_main, w_a, w_dec, b_dec,
                                      cos, sin, True)
    _, qk_c, v_c, _, la_c = _cd_in(ctx2, mod_c[1], None, n_ctx, row(norm_mix_w[1]), w_main, w_a, w_dec, b_dec,
                                   cos, sin, False)
    fm = _fourier(f_l.reshape(bsz, -1, FN_WIDTH), _fft_tables(t))
    head_w = jnp.tile(cd_head_norm_w[0], GLA_HEADS).reshape(1, -1)
    go = _gla(seq(qk_l, t), seq(v_l, t), seq(la_l, t), seq(g_l, t),
              seq(qk_c, n_ctx), seq(v_c, n_ctx), seq(la_c, n_ctx), head_w)
    out = _out_ffn(fm.reshape(bsz * t, -1), go.reshape(bsz * t, -1), x2, mod_x[1], tpb, ROW_TILE,
                   cd_w_out[0].astype(BF16), row(norm_ffn_w[1]), *ffn[1], row(final_norm_w), True)
    return out.reshape(bsz, t, d)
```

```python
import functools

import numpy as np
import jax
import jax.numpy as jnp
from jax import lax
from jax.experimental import pallas as pl
from jax.experimental.pallas import tpu as pltpu

F32 = jnp.float32
BF16 = jnp.bfloat16

D_MODEL = 1024
CTX_LEN = 256
GRID_W = 64
A_WIDTH = 512
A_GROUPS = 4
A_CHUNK = 128
NA_HEADS = 8
NA_HEAD_DIM = 64
NA_WIDTH = 512
NA_WIN_ROWS = 8
NA_WIN_COLS = 16
O_NA_Q = 2 * A_WIDTH
FN_WIDTH = 512
FN_GROUP_CH = 128
GLA_HEADS = 4
GLA_VAL_WIDTH = 512
GLA_KEY_WIDTH = 256
GLA_DK = 64
GLA_DV = 128
GLA_LOW_RANK = 16
GLA_GATE_TEMP = 16.0
GLA_CHUNK = 64
O_Q = FN_WIDTH
O_V = O_Q + 2 * GLA_KEY_WIDTH
O_G = O_V + GLA_VAL_WIDTH
O_A = O_G + GLA_VAL_WIDTH
FFN_HIDDEN = 2816
ROPE_BASE = 10000.0
RMS_EPS = 1e-6
NEG_INF = -1e30

LANES = 128
VMEM_LIMIT_BYTES = 56 * 2 ** 20

ROW_TILE = 512
FFN_ROW_TILE = 1024
ROW_SUBTILE = 256
FFN_SPLIT = 2
GLA_TILE = 512
NA_STAGE_LAG = 2
FFT_UNROLL = 32
FFT_PITCH = GRID_W + 8
FFT_PITCH2 = 2 * GRID_W + 8
GLA_PREP_AHEAD = 2


def _cparams(*sem):
    return pltpu.CompilerParams(dimension_semantics=sem, vmem_limit_bytes=VMEM_LIMIT_BYTES)


def _resident(shape):
    nd = len(shape)
    return pl.BlockSpec(shape, lambda *_: (0,) * nd, pipeline_mode=pl.Buffered(1))


def _dot(a, b):
    return jnp.dot(a, b, preferred_element_type=F32)


def _dot_nt(a, b):
    return lax.dot_general(a, b, (((1,), (1,)), ((), ())), preferred_element_type=F32)


def _dot_tn(a, b):
    return lax.dot_general(a, b, (((0,), (0,)), ((), ())), preferred_element_type=F32)


def _subtiles(n_rows):
    step = min(ROW_SUBTILE, n_rows)
    return [slice(r, r + step) for r in range(0, n_rows, step)]


def _rms(x, w):
    return x * lax.rsqrt(jnp.mean(x * x, axis=-1, keepdims=True) + RMS_EPS) * w


def _rms_mod(x, w, shift, scale):
    return x * lax.rsqrt(jnp.mean(x * x, axis=-1, keepdims=True) + RMS_EPS) * (w * (1.0 + scale)) + shift


def _ada_kernel(c_ref, w_ref, b_ref, o_ref):
    c = c_ref[...]
    s = c * jax.nn.sigmoid(c)
    o_ref[0] = _dot(s.astype(BF16), w_ref[0].astype(BF16)) + b_ref[0]


def _ada(cc, ada_w, ada_b):
    depth, d, n = ada_w.shape
    r = cc.shape[0]
    tn = 1536
    return pl.pallas_call(
        _ada_kernel,
        out_shape=jax.ShapeDtypeStruct((depth, r, n), F32),
        grid=(depth, n // tn),
        in_specs=[pl.BlockSpec((r, d), lambda i, j: (0, 0)),
                  pl.BlockSpec((1, d, tn), lambda i, j: (i, 0, j)),
                  pl.BlockSpec((1, 1, tn), lambda i, j: (i, 0, j))],
        out_specs=pl.BlockSpec((1, r, tn), lambda i, j: (i, 0, j)),
        compiler_params=_cparams("arbitrary", "arbitrary"),
        name="ada_mod",
    )(cc, ada_w, ada_b.reshape(depth, 1, n))


def _mod_spec(tiles_per_batch):
    if tiles_per_batch is None:
        return pl.BlockSpec((1, 6, D_MODEL), lambda i: (0, 0, 0))
    return pl.BlockSpec((1, 6, D_MODEL), lambda i: (i // tiles_per_batch, 0, 0))


def _ab_in_kernel(x_ref, mod_ref, nw_ref, w_ref, snw_ref, sguw_ref, sgub_ref,
                  a_ref, q_ref, k_ref, v_ref, *, tm):
    tiles = _subtiles(tm)
    hs = [_rms_mod(x_ref[rows, :], nw_ref[...], mod_ref[0, 0:1, :], mod_ref[0, 1:2, :]).astype(BF16)
          for rows in tiles]
    uvs = [_dot(h, w_ref[:, :O_NA_Q]) for h in hs]
    for rows, h, uv in zip(tiles, hs, uvs):
        qkv = _dot(h, w_ref[:, O_NA_Q:])
        q_ref[rows, :] = (qkv[:, :NA_WIDTH] * NA_HEAD_DIM ** -0.5).astype(BF16)
        k_ref[rows, :] = qkv[:, NA_WIDTH:2 * NA_WIDTH].astype(BF16)
        v_ref[rows, :] = qkv[:, 2 * NA_WIDTH:].astype(BF16)
        uv = jax.nn.gelu(uv)
        u = uv[:, :A_WIDTH]
        v = _rms(uv[:, A_WIDTH:], snw_ref[...]).astype(BF16)
        for ci in range(uv.shape[0] // A_CHUNK):
            rs = slice(ci * A_CHUNK, (ci + 1) * A_CHUNK)
            out_rows = slice(rows.start + rs.start, rows.start + rs.stop)
            for g in range(A_GROUPS):
                cs = slice(g * LANES, (g + 1) * LANES)
                gate = _dot(sguw_ref[g], v[rs, cs]) + sgub_ref[g]
                a_ref[out_rows, cs] = (u[rs, cs] * gate).astype(BF16)


def _ab_in(x2, mod, tiles_per_batch, tm, nw, w_bf, snw, sguw_bf, sgub_exp):
    m = x2.shape[0]
    n_in = w_bf.shape[1]
    row = lambda w: pl.BlockSpec((tm, w), lambda i: (i, 0))
    out = jax.ShapeDtypeStruct((m, A_WIDTH), BF16)
    return pl.pallas_call(
        functools.partial(_ab_in_kernel, tm=tm),
        out_shape=(out, out, out, out),
        grid=(m // tm,),
        in_specs=[row(D_MODEL), _mod_spec(tiles_per_batch), _resident((1, D_MODEL)),
                  _resident((D_MODEL, n_in)), _resident((1, A_WIDTH)),
                  _resident((A_GROUPS, A_CHUNK, A_CHUNK)), _resident((A_GROUPS, A_CHUNK, LANES))],
        out_specs=(row(A_WIDTH),) * 4,
        compiler_params=_cparams("parallel"),
        name="ab_in",
    )(x2, mod, nw, w_bf, snw, sguw_bf, sgub_exp)


def _na_bias_table(rel_bias):
    w, nrow = GRID_W, 2 * NA_WIN_ROWS - 1
    col = np.arange(w)
    col_start = np.clip(col - NA_WIN_COLS // 2, 0, w - NA_WIN_COLS)
    col_mask = (col[None, :] >= col_start[:, None]) & (col[None, :] < col_start[:, None] + NA_WIN_COLS)
    lo = w - NA_WIN_COLS
    r_ext = jnp.pad(rel_bias, ((0, 0), (0, 0), (lo, 2 * w - lo - (2 * NA_WIN_COLS - 1))))
    flat = jnp.tile(r_ext, (1, 1, w))[:, :, :w * (2 * w - 1)]
    toe = flat.reshape(NA_HEADS, nrow, w, 2 * w - 1)[:, :, :, w - 1:]
    toe = jnp.where(col_mask, toe, NEG_INF).reshape(NA_HEADS // 2, 2, nrow, w, w)
    strips = [jnp.concatenate([toe[:, :, j - o + NA_WIN_ROWS - 1] for j in range(NA_WIN_ROWS)], axis=-1)
              for o in range(NA_WIN_ROWS)]
    return jnp.stack([s.reshape(NA_HEADS // 2, 2 * w, NA_WIN_ROWS * w) for s in strips], axis=1)


def _stack_heads(qb):
    first = lax.broadcasted_iota(jnp.int32, (1, LANES), 1) < NA_HEAD_DIM
    zero = jnp.zeros_like(qb)
    return jnp.concatenate([jnp.where(first, qb, zero), jnp.where(first, zero, qb)], axis=0)


def _unstack_heads(r):
    n = r.shape[0] // 2
    first = lax.broadcasted_iota(jnp.int32, (1, LANES), 1) < NA_HEAD_DIM
    return jnp.where(first, r[:n], r[n:])


def _na_kernel(q_ref, k_ref, v_ref, kc_ref, vc_ref, tab_ref, o_ref, kt_scr, kct_scr, *, rows):
    t = rows * GRID_W
    kt_scr[0] = k_ref[0].T
    kt_scr[1, :, :t - GRID_W] = k_ref[0, GRID_W:, :].T
    kct_scr[...] = kc_ref[0].T
    kct = kct_scr[...]
    vc = vc_ref[0]
    win = NA_WIN_ROWS * GRID_W

    def window(r):
        return min(max(r - NA_WIN_ROWS // 2, 0), rows - NA_WIN_ROWS)

    def scores(r):
        start = window(r)
        odd = start % 2
        kt0 = (start - odd) * GRID_W
        qs = _stack_heads(q_ref[0, r * GRID_W:(r + 1) * GRID_W, :])
        return _dot(qs, kt_scr[odd, :, kt0:kt0 + win]) + tab_ref[0, r - start], _dot(qs, kct)

    def softmax(s):
        s_nb, s_cx = s
        m = jnp.maximum(jnp.max(s_nb, axis=-1, keepdims=True), jnp.max(s_cx, axis=-1, keepdims=True))
        return jnp.exp(s_nb - m).astype(BF16), jnp.exp(s_cx - m).astype(BF16)

    ones_nb = jnp.ones((win, LANES), BF16)
    vc_aug = jnp.concatenate([vc, jnp.ones((vc.shape[0], LANES), BF16)], axis=1)

    def values(r, p):
        e_nb, e_cx = p
        k0 = window(r) * GRID_W
        vb_aug = jnp.concatenate([v_ref[0, k0:k0 + win, :], ones_nb], axis=1)
        acc = _dot(e_nb, vb_aug) + _dot(e_cx, vc_aug)
        o_ref[0, r * GRID_W:(r + 1) * GRID_W, :] = _unstack_heads(acc[:, :LANES] / acc[:, LANES:]).astype(BF16)

    lag = NA_STAGE_LAG
    s_vals, p_vals = {}, {}
    for r in range(rows + 2 * lag):
        if 0 <= r - 2 * lag < rows:
            values(r - 2 * lag, p_vals.pop(r - 2 * lag))
        if 0 <= r - lag < rows:
            p_vals[r - lag] = softmax(s_vals.pop(r - lag))
        if r < rows:
            s_vals[r] = scores(r)


def _na(q, k, v, kc, vc, table):
    b, t, _ = q.shape
    rows = t // GRID_W
    lat = pl.BlockSpec((1, t, LANES), lambda j, i: (i, 0, j))
    ctx = pl.BlockSpec((1, CTX_LEN, LANES), lambda j, i: (i, 0, j))
    return pl.pallas_call(
        functools.partial(_na_kernel, rows=rows),
        out_shape=jax.ShapeDtypeStruct((b, t, NA_WIDTH), BF16),
        grid=(NA_HEADS // 2, b),
        in_specs=[lat, lat, lat, ctx, ctx,
                  pl.BlockSpec((1, NA_WIN_ROWS, 2 * GRID_W, NA_WIN_ROWS * GRID_W), lambda j, i: (j, 0, 0, 0))],
        out_specs=lat,
        scratch_shapes=[pltpu.VMEM((2, LANES, t), BF16), pltpu.VMEM((LANES, CTX_LEN), BF16)],
        compiler_params=_cparams("parallel", "parallel"),
        name="na_attn",
    )(q, k, v, kc, vc, table)


def _ctx_attn_kernel(q_ref, k_ref, v_ref, o_ref):
    for j in range(NA_HEADS // 2):
        cols = slice(j * LANES, (j + 1) * LANES)
        s = _dot_nt(_stack_heads(q_ref[0, :, cols]), k_ref[0, :, cols])
        e = jnp.exp(s - jnp.max(s, axis=-1, keepdims=True))
        acc = _dot(e.astype(BF16), v_ref[0, :, cols])
        o_ref[0, :, cols] = _unstack_heads(acc / jnp.sum(e, axis=-1, keepdims=True)).astype(BF16)


def _ctx_attn(q, k, v):
    b = q.shape[0]
    spec = pl.BlockSpec((1, CTX_LEN, NA_WIDTH), lambda i: (i, 0, 0))
    return pl.pallas_call(
        _ctx_attn_kernel,
        out_shape=jax.ShapeDtypeStruct((b, CTX_LEN, NA_WIDTH), BF16),
        grid=(b,),
        in_specs=[spec, spec, spec],
        out_specs=spec,
        compiler_params=_cparams("parallel"),
        name="ctx_attn",
    )(q, k, v)


def _out_ffn_kernel(a_ref, b_ref, x_ref, mod_ref, wo_ref, nw_ref, wg_ref, wu_ref, wd_ref, fnw_ref, o_ref,
                    *, final):
    half = wo_ref.shape[0] // 2
    piece = FFN_HIDDEN // FFN_SPLIT
    tiles = _subtiles(x_ref.shape[0])
    ys = [_dot(a_ref[rows, :], wo_ref[:half, :]) + _dot(b_ref[rows, :], wo_ref[half:, :]) for rows in tiles]
    x1s = [x_ref[rows, :] + mod_ref[0, 2:3, :] * y for rows, y in zip(tiles, ys)]
    hs = [_rms_mod(x1, nw_ref[...], mod_ref[0, 3:4, :], mod_ref[0, 4:5, :]).astype(BF16) for x1 in x1s]
    accs = [None] * len(tiles)
    for j in range(FFN_SPLIT):
        cs = slice(j * piece, (j + 1) * piece)
        gs = [_dot(h, wg_ref[:, cs]) for h in hs]
        acts = [(g * jax.nn.sigmoid(g) * _dot(h, wu_ref[:, cs])).astype(BF16) for g, h in zip(gs, hs)]
        parts = [_dot(act, wd_ref[cs, :]) for act in acts]
        accs = [part if acc is None else acc + part for acc, part in zip(accs, parts)]
    for rows, x1, acc in zip(tiles, x1s, accs):
        x2 = x1 + mod_ref[0, 5:6, :] * acc
        o_ref[rows, :] = _rms(x2, fnw_ref[...]) if final else x2


def _out_ffn(a, b, x2, mod, tiles_per_batch, tm, wo_bf, nw, wg_bf, wu_bf, wd_bf, fnw, final):
    m = x2.shape[0]
    row = lambda w: pl.BlockSpec((tm, w), lambda i: (i, 0))
    return pl.pallas_call(
        functools.partial(_out_ffn_kernel, final=final),
        out_shape=jax.ShapeDtypeStruct((m, D_MODEL), F32),
        grid=(m // tm,),
        in_specs=[row(a.shape[1]), row(b.shape[1]), row(D_MODEL), _mod_spec(tiles_per_batch),
                  _resident(wo_bf.shape), _resident((1, D_MODEL)), _resident(wg_bf.shape),
                  _resident(wu_bf.shape), _resident(wd_bf.shape), _resident((1, D_MODEL))],
        out_specs=row(D_MODEL),
        compiler_params=_cparams("parallel"),
        name="out_ffn_final" if final else "out_ffn",
    )(a, b, x2, mod, wo_bf, nw, wg_bf, wu_bf, wd_bf, fnw)


def _rope_tables(t):
    half = GLA_DK // 4
    inv_freq = ROPE_BASE ** (-np.arange(half, dtype=np.float64) / half)
    pos = np.arange(t)
    ang_r = (pos // GRID_W)[:, None] * inv_freq[None, :]
    ang_c = (pos % GRID_W)[:, None] * inv_freq[None, :]
    cos = np.concatenate([np.cos(ang_r)] * 2 + [np.cos(ang_c)] * 2, axis=1)
    sin = np.concatenate([-np.sin(ang_r), np.sin(ang_r), -np.sin(ang_c), np.sin(ang_c)], axis=1)
    return (jnp.asarray(np.tile(cos, (1, 2)), F32), jnp.asarray(np.tile(sin, (1, 2)), F32))


def _log_sigmoid(z):
    return jnp.minimum(z, 0.0) - jnp.log(1.0 + jnp.exp(-jnp.abs(z)))


def _cd_in_kernel(x_ref, mod_ref, nw_ref, w_ref, wa_ref, wdec_ref, bdec_ref, cos_ref, sin_ref,
                  f_ref, qk_ref, v_ref, g_ref, la_ref, *, rope):
    for rows in [slice(0, x_ref.shape[0])]:
        h = _rms_mod(x_ref[rows, :], nw_ref[...], mod_ref[0, 0:1, :], mod_ref[0, 1:2, :]).astype(BF16)
        f = _dot(h, w_ref[:, :O_Q])
        for a in range(f.shape[0] // GRID_W):
            r0 = (rows.start // GRID_W + a) * FFT_PITCH
            f_ref[r0:r0 + GRID_W, :] = f[a * GRID_W:(a + 1) * GRID_W]
            f_ref[r0 + GRID_W:r0 + FFT_PITCH, :] = jnp.zeros((FFT_PITCH - GRID_W, f.shape[1]), F32)
        qk = _dot(h, w_ref[:, O_Q:O_V])
        if rope:
            reps = qk.shape[1] // LANES
            cos = jnp.concatenate([cos_ref[rows, :]] * reps, axis=1)
            sin = jnp.concatenate([sin_ref[rows, :]] * reps, axis=1)
            w = qk.shape[1]
            q16 = GLA_DK // 4
            first = (lax.broadcasted_iota(jnp.int32, (1, w), 1) % (2 * q16)) < q16
            partner = jnp.where(first, pltpu.roll(qk, w - q16, 1), pltpu.roll(qk, q16, 1))
            qk = qk * cos + partner * sin
        is_q = lax.broadcasted_iota(jnp.int32, (1, qk.shape[1]), 1) < GLA_KEY_WIDTH
        qk_ref[rows, :] = jnp.where(is_q, qk * GLA_DK ** -0.5, qk)
        v_ref[rows, :] = _dot(h, w_ref[:, O_V:O_G]).astype(BF16)
        g_ref[rows, :] = _dot(h, w_ref[:, O_G:O_A])
        a = _dot(h, wa_ref[...]).astype(BF16)
        la_ref[rows, :] = _log_sigmoid(_dot(a, wdec_ref[...]) + bdec_ref[...]) * (1.0 / GLA_GATE_TEMP)


def _cd_in(x2, mod, tiles_per_batch, tm, nw, w_bf, wa_bf, wdec_bf, bdec, cos, sin, rope):
    m = x2.shape[0]
    row = lambda w: pl.BlockSpec((tm, w), lambda i: (i, 0))
    if rope:
        tab = pl.BlockSpec((tm, LANES), lambda i: (i % tiles_per_batch, 0))
    else:
        tab = pl.BlockSpec((tm, LANES), lambda i: (0, 0))
    w512 = 2 * GLA_KEY_WIDTH
    pad = lambda n: n // GRID_W * FFT_PITCH
    outs = (jax.ShapeDtypeStruct((pad(m), FN_WIDTH), F32), jax.ShapeDtypeStruct((m, w512), F32),
            jax.ShapeDtypeStruct((m, GLA_VAL_WIDTH), BF16), jax.ShapeDtypeStruct((m, GLA_VAL_WIDTH), F32),
            jax.ShapeDtypeStruct((m, w512), F32))
    return pl.pallas_call(
        functools.partial(_cd_in_kernel, rope=rope),
        out_shape=outs,
        grid=(m // tm,),
        in_specs=[row(D_MODEL), _mod_spec(tiles_per_batch), _resident((1, D_MODEL)),
                  _resident(w_bf.shape), _resident(wa_bf.shape), _resident(wdec_bf.shape),
                  _resident((1, w512)), tab, tab],
        out_specs=(pl.BlockSpec((pad(tm), FN_WIDTH), lambda i: (i, 0)),
                   row(w512), row(GLA_VAL_WIDTH), row(GLA_VAL_WIDTH), row(w512)),
        compiler_params=_cparams("parallel"),
        name="cd_in_rope" if rope else "cd_in_ctx",
    )(x2, mod, nw, w_bf, wa_bf, wdec_bf, bdec, cos, sin)


def _fft_tables(t):
    n1 = GRID_W
    assert t == n1 * n1
    a = np.arange(n1)
    ang1 = 2 * np.pi * np.outer(a, a) / n1
    cs = np.concatenate([np.cos(ang1), -np.sin(ang1)], axis=0)
    kap = a[:, None, None] + n1 * a[None, :, None]
    ang2 = 2 * np.pi * kap * a[None, None, :] / t
    gr, gi = np.cos(ang2), -np.sin(ang2)
    g = np.concatenate([np.concatenate([gr, -gi], axis=2), np.concatenate([gi, gr], axis=2)], axis=1)
    c = np.arange(FN_GROUP_CH)
    ang3 = 2 * np.pi * np.outer(c, c) / FN_GROUP_CH
    norm = 1.0 / np.sqrt(t * FN_GROUP_CH)
    return (jnp.asarray(cs, F32), jnp.asarray(g, F32),
            jnp.asarray(np.cos(ang3) * norm, F32), jnp.asarray(np.sin(ang3) * norm, F32))


def _fft_kernel(x_ref, cs_ref, g_ref, cc_ref, sc_ref, o_ref, p_scr, zr_scr, zi_scr):
    n1 = GRID_W
    cs = cs_ref[...].astype(BF16)

    def stage1(b, carry):
        xb = x_ref[0, pl.ds(b, n1, stride=FFT_PITCH), :].astype(BF16)
        p_scr[pl.ds(pl.multiple_of(b * FFT_PITCH2, 8), 2 * n1), :] = _dot(cs, xb)
        return carry

    lax.fori_loop(0, n1, stage1, 0, unroll=FFT_UNROLL)

    def stage2(k1, carry):
        pr = p_scr[pl.ds(k1, n1, stride=FFT_PITCH2), :]
        pi = p_scr[pl.ds(n1 + k1, n1, stride=FFT_PITCH2), :]
        z = _dot(g_ref[k1].astype(BF16), jnp.concatenate([pr, pi], axis=0).astype(BF16))
        rows = pl.ds(pl.multiple_of(k1 * FFT_PITCH, 8), n1)
        zr_scr[rows, :] = z[:n1]
        zi_scr[rows, :] = z[n1:]
        return carry

    lax.fori_loop(0, n1, stage2, 0, unroll=FFT_UNROLL)

    ccsc = jnp.concatenate([cc_ref[...], sc_ref[...]], axis=0).astype(BF16)

    def stage3(j, carry):
        parts = []
        for u in range(FFT_UNROLL):
            k2 = j * FFT_UNROLL + u
            parts.append(jnp.concatenate([zr_scr[pl.ds(k2, n1, stride=FFT_PITCH), :],
                                          zi_scr[pl.ds(k2, n1, stride=FFT_PITCH), :]], axis=1))
        out = _dot(jnp.concatenate(parts, axis=0).astype(BF16), ccsc)
        o_ref[0, pl.ds(pl.multiple_of(j * FFT_UNROLL * n1, FFT_UNROLL * n1), FFT_UNROLL * n1), :] = out.astype(BF16)
        return carry

    lax.fori_loop(0, n1 // FFT_UNROLL, stage3, 0)


def _fourier(f_padded, tables):
    b, tp, w = f_padded.shape
    t = tp // FFT_PITCH * GRID_W
    cw = FN_GROUP_CH
    cs, g, cc, sc = tables
    return pl.pallas_call(
        _fft_kernel,
        out_shape=jax.ShapeDtypeStruct((b, t, w), BF16),
        grid=(b, w // cw),
        in_specs=[pl.BlockSpec((1, tp, cw), lambda i, j: (i, 0, j)),
                  _resident(cs.shape), _resident(g.shape), _resident(cc.shape), _resident(sc.shape)],
        out_specs=pl.BlockSpec((1, t, cw), lambda i, j: (i, 0, j)),
        scratch_shapes=[pltpu.VMEM((GRID_W * FFT_PITCH2, cw), F32), pltpu.VMEM((tp, cw), F32),
                        pltpu.VMEM((tp, cw), F32)],
        compiler_params=_cparams("parallel", "parallel"),
        name="fourier_mix",
    )(f_padded, cs, g, cc, sc)


def _block_diag(x, block):
    head = lax.broadcasted_iota(jnp.int32, (1, x.shape[1]), 1) // block
    return jnp.concatenate([jnp.where(head == h, x, jnp.zeros_like(x)) for h in range(GLA_HEADS)], axis=0)


def _cum_rows(x, reverse):
    n, w = x.shape
    row = lax.broadcasted_iota(jnp.int32, (n, 1), 0)
    s = 1
    while s < n:
        if s < 8:
            if reverse:
                shifted = jnp.where(row < n - s, pltpu.roll(x, n - s, 0), 0.0)
            else:
                shifted = jnp.where(row >= s, pltpu.roll(x, s, 0), 0.0)
        else:
            pad = jnp.zeros((s, w), F32)
            shifted = jnp.concatenate([x[s:], pad] if reverse else [pad, x[:n - s]], axis=0)
        x = x + shifted
        s *= 2
    return x


def _gla_prep(q, k, la, reverse, want_out):
    c = GLA_CHUNK
    b = _cum_rows(la, reverse)
    b_last = b[0:1, :] if reverse else b[c - 1:c, :]
    kd = (k * jnp.exp(b_last - b)).astype(BF16)
    decay = jnp.exp(b_last)
    if not want_out:
        return kd, decay
    qe = (q * jnp.exp(b)).astype(BF16)
    ke = _block_diag(k * jnp.exp(-b), GLA_DK).astype(BF16)
    l = lax.broadcasted_iota(jnp.int32, (c, GLA_HEADS * c), 0)
    m = lax.broadcasted_iota(jnp.int32, (c, GLA_HEADS * c), 1) % c
    att = jnp.where((l <= m) if reverse else (l >= m), _dot_nt(qe, ke), 0.0).astype(BF16)
    return kd, decay, qe, att


def _gla_state_step(kd, decay, v, state):
    v_rows = jnp.concatenate([v[:, h * GLA_DV:(h + 1) * GLA_DV] for h in range(GLA_HEADS)], axis=0)
    return state * decay + _dot_tn(v_rows, _block_diag(kd, GLA_DK))


def _gla_apply(prep, v, state):
    c = GLA_CHUNK
    kd, decay, qe, att = prep
    inter = _dot_nt(_block_diag(qe, GLA_DK), state.astype(BF16))
    out = _dot(att, _block_diag(v, GLA_DV)) + jnp.concatenate(
        [inter[h * c:(h + 1) * c] for h in range(GLA_HEADS)], axis=1)
    return out, _gla_state_step(kd, decay, v, state)


def _gla_kernel(qkf_ref, vf_ref, laf_ref, gf_ref, qkb_ref, vb_ref, lab_ref, gb_ref,
                kc_ref, vc_ref, lac_ref, hw_ref, o_ref, sf_scr, sb_scr, acc_scr, *, n_tiles, tile):
    c = GLA_CHUNK
    i = pl.program_id(1)
    kw = GLA_KEY_WIDTH
    n_chunks = tile // c

    @pl.when(i == 0)
    def _():
        n_ctx = kc_ref.shape[1] // c
        sf = jnp.zeros((GLA_DV, kw), F32)
        sb = jnp.zeros((GLA_DV, kw), F32)
        for n in range(n_ctx):
            rf = slice(n * c, (n + 1) * c)
            rb = slice((n_ctx - 1 - n) * c, (n_ctx - n) * c)
            sf = _gla_state_step(*_gla_prep(None, kc_ref[0, rf, :], lac_ref[0, rf, :kw], False, False),
                                 vc_ref[0, rf, :], sf)
            sb = _gla_state_step(*_gla_prep(None, kc_ref[0, rb, :], lac_ref[0, rb, kw:], True, False),
                                 vc_ref[0, rb, :], sb)
        sf_scr[...] = sf
        sb_scr[...] = sb

    hw = hw_ref[...]

    def finish(o, g):
        parts = []
        for h in range(GLA_HEADS):
            oh = o[:, h * GLA_DV:(h + 1) * GLA_DV]
            parts.append(oh * lax.rsqrt(jnp.mean(oh * oh, axis=-1, keepdims=True) + RMS_EPS))
        return (jnp.concatenate(parts, axis=1) * hw * (g * jax.nn.sigmoid(g))).astype(BF16)

    def sweep(second_pass):
        def emit(o, g_ref, rloc, row0):
            rows = pl.ds(pl.multiple_of(row0, c), c)
            if second_pass:
                o_ref[0, rows, :] = finish(o + acc_scr[rows, :], g_ref[0, rloc, :])
            else:
                acc_scr[rows, :] = o

        def rows_of(n):
            nb = n_chunks - 1 - n
            return slice(n * c, (n + 1) * c), slice(nb * c, (nb + 1) * c)

        def prep(n):
            rf, rb = rows_of(n)
            qkf, qkb = qkf_ref[0, rf, :], qkb_ref[0, rb, :]
            return (_gla_prep(qkf[:, :kw], qkf[:, kw:], laf_ref[0, rf, :], False, True),
                    _gla_prep(qkb[:, :kw], qkb[:, kw:], lab_ref[0, rb, :], True, True))

        sf, sb = sf_scr[...], sb_scr[...]
        ready = {n: prep(n) for n in range(min(GLA_PREP_AHEAD, n_chunks))}
        for n in range(n_chunks):
            rf, rb = rows_of(n)
            pf, pb = ready.pop(n)
            of, sf = _gla_apply(pf, vf_ref[0, rf, :], sf)
            emit(of, gf_ref, rf, i * tile + rf.start)
            ob, sb = _gla_apply(pb, vb_ref[0, rb, :], sb)
            emit(ob, gb_ref, rb, (n_tiles - 1 - i) * tile + rb.start)
            if n + GLA_PREP_AHEAD < n_chunks:
                ready[n + GLA_PREP_AHEAD] = prep(n + GLA_PREP_AHEAD)
        sf_scr[...] = sf
        sb_scr[...] = sb

    @pl.when(i < n_tiles // 2)
    def _():
        sweep(False)

    @pl.when(i >= n_tiles // 2)
    def _():
        sweep(True)


def _gla(qk, v, la, g, qk_c, v_c, la_c, head_w):
    b, t, _ = qk.shape
    tile = GLA_TILE
    n_tiles = t // tile
    kw, vw = GLA_KEY_WIDTH, GLA_VAL_WIDTH
    fwd = lambda w, col: pl.BlockSpec((1, tile, w), lambda bi, i: (bi, i, col))
    bwd = lambda w, col: pl.BlockSpec((1, tile, w), lambda bi, i: (bi, n_tiles - 1 - i, col))
    ctx = lambda w: pl.BlockSpec((1, CTX_LEN, w), lambda bi, i: (bi, 0, 0))
    half = n_tiles // 2
    gate_fwd = pl.BlockSpec((1, tile, vw), lambda bi, i: (bi, jnp.maximum(i, half), 0))
    gate_bwd = pl.BlockSpec((1, tile, vw), lambda bi, i: (bi, n_tiles - 1 - jnp.maximum(i, half), 0))
    return pl.pallas_call(
        functools.partial(_gla_kernel, n_tiles=n_tiles, tile=tile),
        out_shape=jax.ShapeDtypeStruct((b, t, vw), BF16),
        grid=(b, n_tiles),
        in_specs=[fwd(2 * kw, 0), fwd(vw, 0), fwd(kw, 0), gate_fwd,
                  bwd(2 * kw, 0), bwd(vw, 0), bwd(kw, 1), gate_bwd,
                  pl.BlockSpec((1, CTX_LEN, kw), lambda bi, i: (bi, 0, 1)), ctx(vw), ctx(2 * kw),
                  pl.BlockSpec((1, vw), lambda bi, i: (0, 0))],
        out_specs=pl.BlockSpec((1, t, vw), lambda bi, i: (bi, 0, 0)),
        scratch_shapes=[pltpu.VMEM((GLA_DV, kw), F32), pltpu.VMEM((GLA_DV, kw), F32), pltpu.VMEM((t, vw), F32)],
        compiler_params=_cparams("parallel", "arbitrary"),
        name="gla",
    )(qk, v, la, g, qk, v, la, g, qk_c, v_c, la_c, head_w)


def kernel(x, c, ctx, c_ctx, ada_w, ada_b, norm_mix_w, norm_ffn_w, ffn_w_gate, ffn_w_up, ffn_w_down,
           ab_w_in, ab_w_out, ab_sgu_norm_w, ab_sgu_w, ab_sgu_b, ab_rel_bias,
           cd_w_in, cd_w_out, cd_decay_w_fwd, cd_decay_b_fwd, cd_decay_w_bwd, cd_decay_b_bwd, cd_head_norm_w,
           final_norm_w):
    bsz, t, d = x.shape
    n_ctx = ctx.shape[1]
    tpb = t // ROW_TILE
    x2 = x.reshape(bsz * t, d)
    ctx2 = ctx.reshape(bsz * n_ctx, d)
    row = lambda w: w.reshape(1, -1)

    cc = jnp.concatenate([c, c_ctx[None, :], jnp.zeros((16 - bsz - 1, d), F32)], axis=0)
    mod = _ada(cc, ada_w, ada_b)
    mod_x = [mod[i, :bsz].reshape(bsz, 6, d) for i in range(2)]
    mod_c = [mod[i, bsz:bsz + 1].reshape(1, 6, d) for i in range(2)]
    ffn = [(ffn_w_gate[i].astype(BF16), ffn_w_up[i].astype(BF16), ffn_w_down[i].astype(BF16)) for i in range(2)]

    w_in = ab_w_in[0].astype(BF16)
    sgu = (row(ab_sgu_norm_w[0]), ab_sgu_w[0].astype(BF16),
           jnp.broadcast_to(ab_sgu_b[0][:, :, None], (A_GROUPS, A_CHUNK, LANES)))
    a_l, q_l, k_l, v_l = _ab_in(x2, mod_x[0], tpb, ROW_TILE, row(norm_mix_w[0]), w_in, *sgu)
    a_c, q_c, k_c, v_c = _ab_in(ctx2, mod_c[0], None, ROW_TILE, row(norm_mix_w[0]), w_in, *sgu)
    seq = lambda z, n: z.reshape(bsz, n, z.shape[-1])
    b_l = _na(seq(q_l, t), seq(k_l, t), seq(v_l, t), seq(k_c, n_ctx), seq(v_c, n_ctx), _na_bias_table(ab_rel_bias[0]))
    b_c = _ctx_attn(seq(q_c, n_ctx), seq(k_c, n_ctx), seq(v_c, n_ctx))
    wo = ab_w_out[0].astype(BF16)
    x2 = _out_ffn(a_l, b_l.reshape(bsz * t, -1), x2, mod_x[0], t // FFN_ROW_TILE, FFN_ROW_TILE, wo,
                  row(norm_ffn_w[0]), *ffn[0], row(final_norm_w), False)
    ctx2 = _out_ffn(a_c, b_c.reshape(bsz * n_ctx, -1), ctx2, mod_c[0], None, ROW_TILE, wo, row(norm_ffn_w[0]),
                    *ffn[0], row(final_norm_w), False)

    w_in = cd_w_in[0]
    w_main = w_in[:, :O_A].astype(BF16)
    w_a = jnp.pad(w_in[:, O_A:], ((0, 0), (0, LANES - 2 * GLA_LOW_RANK))).astype(BF16)
    w_dec = jnp.zeros((LANES, 2 * GLA_KEY_WIDTH), F32)
    w_dec = w_dec.at[:GLA_LOW_RANK, :GLA_KEY_WIDTH].set(cd_decay_w_fwd[0])
    w_dec = w_dec.at[GLA_LOW_RANK:2 * GLA_LOW_RANK, GLA_KEY_WIDTH:].set(cd_decay_w_bwd[0]).astype(BF16)
    b_dec = jnp.concatenate([cd_decay_b_fwd[0], cd_decay_b_bwd[0]]).reshape(1, -1)
    cos, sin = _rope_tables(t)
    f_l, qk_l, v_l, g_l, la_l = _cd_in(x2, mod_x[1], tpb, ROW_TILE, row(norm_mix_w[1]), w_main, w_a, w_dec, b_dec,
                                      cos, sin, True)
    _, qk_c, v_c, _, la_c = _cd_in(ctx2, mod_c[1], None, ROW_TILE, row(norm_mix_w[1]), w_main, w_a, w_dec, b_dec,
                                   cos, sin, False)
    fm = _fourier(f_l.reshape(bsz, -1, FN_WIDTH), _fft_tables(t))
    head_w = jnp.tile(cd_head_norm_w[0], GLA_HEADS).reshape(1, -1)
    go = _gla(seq(qk_l, t), seq(v_l, t), seq(la_l, t), seq(g_l, t),
              seq(qk_c, n_ctx), seq(v_c, n_ctx), seq(la_c, n_ctx), head_w)
    out = _out_ffn(fm.reshape(bsz * t, -1), go.reshape(bsz * t, -1), x2, mod_x[1], t // FFN_ROW_TILE, FFN_ROW_TILE,
                   cd_w_out[0].astype(BF16), row(norm_ffn_w[1]), *ffn[1], row(final_norm_w), True)
    return out.reshape(bsz, t, d)
```

```python
import functools

import numpy as np
import jax
import jax.numpy as jnp
from jax import lax
from jax.experimental import pallas as pl
from jax.experimental.pallas import tpu as pltpu

F32 = jnp.float32
BF16 = jnp.bfloat16

D_MODEL = 1024
CTX_LEN = 256
GRID_W = 64
A_WIDTH = 512
A_GROUPS = 4
A_CHUNK = 128
NA_HEADS = 8
NA_HEAD_DIM = 64
NA_WIDTH = 512
NA_WIN_ROWS = 8
NA_WIN_COLS = 16
O_NA_Q = 2 * A_WIDTH
FN_WIDTH = 512
FN_GROUP_CH = 128
GLA_HEADS = 4
GLA_VAL_WIDTH = 512
GLA_KEY_WIDTH = 256
GLA_DK = 64
GLA_DV = 128
GLA_LOW_RANK = 16
GLA_GATE_TEMP = 16.0
GLA_CHUNK = 64
O_Q = FN_WIDTH
O_V = O_Q + 2 * GLA_KEY_WIDTH
O_G = O_V + GLA_VAL_WIDTH
O_A = O_G + GLA_VAL_WIDTH
FFN_HIDDEN = 2816
ROPE_BASE = 10000.0
RMS_EPS = 1e-6
NEG_INF = -1e30

LANES = 128
VMEM_LIMIT_BYTES = 56 * 2 ** 20

ROW_TILE = 1024
FFN_ROW_TILE = 1024
ROW_SUBTILE = 256
FFN_SPLIT = 2
GLA_TILE = 1024
NA_STAGE_LAG = 2
FFT_UNROLL = 32
FFT_PITCH = GRID_W + 8
FFT_PITCH2 = 2 * GRID_W + 8
GLA_PREP_AHEAD = 2


def _cparams(*sem):
    return pltpu.CompilerParams(dimension_semantics=sem, vmem_limit_bytes=VMEM_LIMIT_BYTES)


def _resident(shape):
    nd = len(shape)
    return pl.BlockSpec(shape, lambda *_: (0,) * nd, pipeline_mode=pl.Buffered(1))


def _dot(a, b):
    return jnp.dot(a, b, preferred_element_type=F32)


def _dot_nt(a, b):
    return lax.dot_general(a, b, (((1,), (1,)), ((), ())), preferred_element_type=F32)


def _dot_tn(a, b):
    return lax.dot_general(a, b, (((0,), (0,)), ((), ())), preferred_element_type=F32)


def _subtiles(n_rows):
    step = min(ROW_SUBTILE, n_rows)
    return [slice(r, r + step) for r in range(0, n_rows, step)]


def _rms(x, w):
    return x * lax.rsqrt(jnp.mean(x * x, axis=-1, keepdims=True) + RMS_EPS) * w


def _rms_mod(x, w, shift, scale):
    return x * lax.rsqrt(jnp.mean(x * x, axis=-1, keepdims=True) + RMS_EPS) * (w * (1.0 + scale)) + shift


def _ada_kernel(c_ref, w_ref, b_ref, o_ref):
    c = c_ref[...]
    s = c * jax.nn.sigmoid(c)
    o_ref[0] = _dot(s.astype(BF16), w_ref[0].astype(BF16)) + b_ref[0]


def _ada(cc, ada_w, ada_b):
    depth, d, n = ada_w.shape
    r = cc.shape[0]
    tn = 1536
    return pl.pallas_call(
        _ada_kernel,
        out_shape=jax.ShapeDtypeStruct((depth, r, n), F32),
        grid=(depth, n // tn),
        in_specs=[pl.BlockSpec((r, d), lambda i, j: (0, 0)),
                  pl.BlockSpec((1, d, tn), lambda i, j: (i, 0, j)),
                  pl.BlockSpec((1, 1, tn), lambda i, j: (i, 0, j))],
        out_specs=pl.BlockSpec((1, r, tn), lambda i, j: (i, 0, j)),
        compiler_params=_cparams("arbitrary", "arbitrary"),
        name="ada_mod",
    )(cc, ada_w, ada_b.reshape(depth, 1, n))


def _mod_spec(tiles_per_batch):
    if tiles_per_batch is None:
        return pl.BlockSpec((1, 6, D_MODEL), lambda i: (0, 0, 0))
    return pl.BlockSpec((1, 6, D_MODEL), lambda i: (i // tiles_per_batch, 0, 0))


def _ab_in_kernel(x_ref, mod_ref, nw_ref, w_ref, snw_ref, sguw_ref, sgub_ref,
                  a_ref, q_ref, k_ref, v_ref, *, tm):
    tiles = _subtiles(tm)
    hs = [_rms_mod(x_ref[rows, :], nw_ref[...], mod_ref[0, 0:1, :], mod_ref[0, 1:2, :]).astype(BF16)
          for rows in tiles]
    uvs = [_dot(h, w_ref[:, :O_NA_Q]) for h in hs]
    for rows, h, uv in zip(tiles, hs, uvs):
        qkv = _dot(h, w_ref[:, O_NA_Q:])
        q_ref[rows, :] = (qkv[:, :NA_WIDTH] * NA_HEAD_DIM ** -0.5).astype(BF16)
        k_ref[rows, :] = qkv[:, NA_WIDTH:2 * NA_WIDTH].astype(BF16)
        v_ref[rows, :] = qkv[:, 2 * NA_WIDTH:].astype(BF16)
        uv = jax.nn.gelu(uv)
        u = uv[:, :A_WIDTH]
        v = _rms(uv[:, A_WIDTH:], snw_ref[...]).astype(BF16)
        for ci in range(uv.shape[0] // A_CHUNK):
            rs = slice(ci * A_CHUNK, (ci + 1) * A_CHUNK)
            out_rows = slice(rows.start + rs.start, rows.start + rs.stop)
            for g in range(A_GROUPS):
                cs = slice(g * LANES, (g + 1) * LANES)
                gate = _dot(sguw_ref[g], v[rs, cs]) + sgub_ref[g]
                a_ref[out_rows, cs] = (u[rs, cs] * gate).astype(BF16)


def _ab_in(x2, mod, tiles_per_batch, tm, nw, w_bf, snw, sguw_bf, sgub_exp):
    m = x2.shape[0]
    n_in = w_bf.shape[1]
    row = lambda w: pl.BlockSpec((tm, w), lambda i: (i, 0))
    out = jax.ShapeDtypeStruct((m, A_WIDTH), BF16)
    return pl.pallas_call(
        functools.partial(_ab_in_kernel, tm=tm),
        out_shape=(out, out, out, out),
        grid=(m // tm,),
        in_specs=[row(D_MODEL), _mod_spec(tiles_per_batch), _resident((1, D_MODEL)),
                  _resident((D_MODEL, n_in)), _resident((1, A_WIDTH)),
                  _resident((A_GROUPS, A_CHUNK, A_CHUNK)), _resident((A_GROUPS, A_CHUNK, LANES))],
        out_specs=(row(A_WIDTH),) * 4,
        compiler_params=_cparams("parallel"),
        name="ab_in",
    )(x2, mod, nw, w_bf, snw, sguw_bf, sgub_exp)


def _na_bias_table(rel_bias):
    w, nrow = GRID_W, 2 * NA_WIN_ROWS - 1
    col = np.arange(w)
    col_start = np.clip(col - NA_WIN_COLS // 2, 0, w - NA_WIN_COLS)
    col_mask = (col[None, :] >= col_start[:, None]) & (col[None, :] < col_start[:, None] + NA_WIN_COLS)
    lo = w - NA_WIN_COLS
    r_ext = jnp.pad(rel_bias, ((0, 0), (0, 0), (lo, 2 * w - lo - (2 * NA_WIN_COLS - 1))))
    flat = jnp.tile(r_ext, (1, 1, w))[:, :, :w * (2 * w - 1)]
    toe = flat.reshape(NA_HEADS, nrow, w, 2 * w - 1)[:, :, :, w - 1:]
    toe = jnp.where(col_mask, toe, NEG_INF).reshape(NA_HEADS // 2, 2, nrow, w, w)
    strips = [jnp.concatenate([toe[:, :, j - o + NA_WIN_ROWS - 1] for j in range(NA_WIN_ROWS)], axis=-1)
              for o in range(NA_WIN_ROWS)]
    return jnp.stack([s.reshape(NA_HEADS // 2, 2 * w, NA_WIN_ROWS * w) for s in strips], axis=1)


def _stack_heads(qb):
    first = lax.broadcasted_iota(jnp.int32, (1, LANES), 1) < NA_HEAD_DIM
    zero = jnp.zeros_like(qb)
    return jnp.concatenate([jnp.where(first, qb, zero), jnp.where(first, zero, qb)], axis=0)


def _unstack_heads(r):
    n = r.shape[0] // 2
    first = lax.broadcasted_iota(jnp.int32, (1, LANES), 1) < NA_HEAD_DIM
    return jnp.where(first, r[:n], r[n:])


def _na_kernel(q_ref, k_ref, v_ref, kc_ref, vc_ref, tab_ref, o_ref, kt_scr, kct_scr, *, rows):
    t = rows * GRID_W
    kt_scr[0] = k_ref[0].T
    kt_scr[1, :, :t - GRID_W] = k_ref[0, GRID_W:, :].T
    kct_scr[...] = kc_ref[0].T
    kct = kct_scr[...]
    vc = vc_ref[0]
    win = NA_WIN_ROWS * GRID_W

    def window(r):
        return min(max(r - NA_WIN_ROWS // 2, 0), rows - NA_WIN_ROWS)

    def scores(r):
        start = window(r)
        odd = start % 2
        kt0 = (start - odd) * GRID_W
        qs = _stack_heads(q_ref[0, r * GRID_W:(r + 1) * GRID_W, :])
        return _dot(qs, kt_scr[odd, :, kt0:kt0 + win]) + tab_ref[0, r - start], _dot(qs, kct)

    def softmax(s):
        s_nb, s_cx = s
        m = jnp.maximum(jnp.max(s_nb, axis=-1, keepdims=True), jnp.max(s_cx, axis=-1, keepdims=True))
        return jnp.exp(s_nb - m).astype(BF16), jnp.exp(s_cx - m).astype(BF16)

    ones_nb = jnp.ones((win, LANES), BF16)
    vc_aug = jnp.concatenate([vc, jnp.ones((vc.shape[0], LANES), BF16)], axis=1)

    def values(r, p):
        e_nb, e_cx = p
        k0 = window(r) * GRID_W
        vb_aug = jnp.concatenate([v_ref[0, k0:k0 + win, :], ones_nb], axis=1)
        acc = _dot(e_nb, vb_aug) + _dot(e_cx, vc_aug)
        o_ref[0, r * GRID_W:(r + 1) * GRID_W, :] = _unstack_heads(acc[:, :LANES] / acc[:, LANES:]).astype(BF16)

    lag = NA_STAGE_LAG
    s_vals, p_vals = {}, {}
    for r in range(rows + 2 * lag):
        if 0 <= r - 2 * lag < rows:
            values(r - 2 * lag, p_vals.pop(r - 2 * lag))
        if 0 <= r - lag < rows:
            p_vals[r - lag] = softmax(s_vals.pop(r - lag))
        if r < rows:
            s_vals[r] = scores(r)


def _na(q, k, v, kc, vc, table):
    b, t, _ = q.shape
    rows = t // GRID_W
    lat = pl.BlockSpec((1, t, LANES), lambda j, i: (i, 0, j))
    ctx = pl.BlockSpec((1, CTX_LEN, LANES), lambda j, i: (i, 0, j))
    return pl.pallas_call(
        functools.partial(_na_kernel, rows=rows),
        out_shape=jax.ShapeDtypeStruct((b, t, NA_WIDTH), BF16),
        grid=(NA_HEADS // 2, b),
        in_specs=[lat, lat, lat, ctx, ctx,
                  pl.BlockSpec((1, NA_WIN_ROWS, 2 * GRID_W, NA_WIN_ROWS * GRID_W), lambda j, i: (j, 0, 0, 0))],
        out_specs=lat,
        scratch_shapes=[pltpu.VMEM((2, LANES, t), BF16), pltpu.VMEM((LANES, CTX_LEN), BF16)],
        compiler_params=_cparams("parallel", "parallel"),
        name="na_attn",
    )(q, k, v, kc, vc, table)


def _ctx_attn_kernel(q_ref, k_ref, v_ref, o_ref):
    for j in range(NA_HEADS // 2):
        cols = slice(j * LANES, (j + 1) * LANES)
        s = _dot_nt(_stack_heads(q_ref[0, :, cols]), k_ref[0, :, cols])
        e = jnp.exp(s - jnp.max(s, axis=-1, keepdims=True))
        acc = _dot(e.astype(BF16), v_ref[0, :, cols])
        o_ref[0, :, cols] = _unstack_heads(acc / jnp.sum(e, axis=-1, keepdims=True)).astype(BF16)


def _ctx_attn(q, k, v):
    b = q.shape[0]
    spec = pl.BlockSpec((1, CTX_LEN, NA_WIDTH), lambda i: (i, 0, 0))
    return pl.pallas_call(
        _ctx_attn_kernel,
        out_shape=jax.ShapeDtypeStruct((b, CTX_LEN, NA_WIDTH), BF16),
        grid=(b,),
        in_specs=[spec, spec, spec],
        out_specs=spec,
        compiler_params=_cparams("parallel"),
        name="ctx_attn",
    )(q, k, v)


def _out_ffn_kernel(a_ref, b_ref, x_ref, mod_ref, wo_ref, nw_ref, wg_ref, wu_ref, wd_ref, fnw_ref, o_ref,
                    *, final):
    half = wo_ref.shape[0] // 2
    piece = FFN_HIDDEN // FFN_SPLIT
    tiles = _subtiles(x_ref.shape[0])
    ys = [_dot(a_ref[rows, :], wo_ref[:half, :]) + _dot(b_ref[rows, :], wo_ref[half:, :]) for rows in tiles]
    x1s = [x_ref[rows, :] + mod_ref[0, 2:3, :] * y for rows, y in zip(tiles, ys)]
    hs = [_rms_mod(x1, nw_ref[...], mod_ref[0, 3:4, :], mod_ref[0, 4:5, :]).astype(BF16) for x1 in x1s]
    accs = [None] * len(tiles)
    for j in range(FFN_SPLIT):
        cs = slice(j * piece, (j + 1) * piece)
        gs = [_dot(h, wg_ref[:, cs]) for h in hs]
        acts = [(g * jax.nn.sigmoid(g) * _dot(h, wu_ref[:, cs])).astype(BF16) for g, h in zip(gs, hs)]
        parts = [_dot(act, wd_ref[cs, :]) for act in acts]
        accs = [part if acc is None else acc + part for acc, part in zip(accs, parts)]
    for rows, x1, acc in zip(tiles, x1s, accs):
        x2 = x1 + mod_ref[0, 5:6, :] * acc
        o_ref[rows, :] = _rms(x2, fnw_ref[...]) if final else x2


def _out_ffn(a, b, x2, mod, tiles_per_batch, tm, wo_bf, nw, wg_bf, wu_bf, wd_bf, fnw, final):
    m = x2.shape[0]
    row = lambda w: pl.BlockSpec((tm, w), lambda i: (i, 0))
    return pl.pallas_call(
        functools.partial(_out_ffn_kernel, final=final),
        out_shape=jax.ShapeDtypeStruct((m, D_MODEL), F32),
        grid=(m // tm,),
        in_specs=[row(a.shape[1]), row(b.shape[1]), row(D_MODEL), _mod_spec(tiles_per_batch),
                  _resident(wo_bf.shape), _resident((1, D_MODEL)), _resident(wg_bf.shape),
                  _resident(wu_bf.shape), _resident(wd_bf.shape), _resident((1, D_MODEL))],
        out_specs=row(D_MODEL),
        compiler_params=_cparams("parallel"),
        name="out_ffn_final" if final else "out_ffn",
    )(a, b, x2, mod, wo_bf, nw, wg_bf, wu_bf, wd_bf, fnw)


def _rope_tables(t):
    half = GLA_DK // 4
    inv_freq = ROPE_BASE ** (-np.arange(half, dtype=np.float64) / half)
    pos = np.arange(t)
    ang_r = (pos // GRID_W)[:, None] * inv_freq[None, :]
    ang_c = (pos % GRID_W)[:, None] * inv_freq[None, :]
    cos = np.concatenate([np.cos(ang_r)] * 2 + [np.cos(ang_c)] * 2, axis=1)
    sin = np.concatenate([-np.sin(ang_r), np.sin(ang_r), -np.sin(ang_c), np.sin(ang_c)], axis=1)
    return (jnp.asarray(np.tile(cos, (1, 2)), F32), jnp.asarray(np.tile(sin, (1, 2)), F32))


def _log_sigmoid(z):
    return jnp.minimum(z, 0.0) - jnp.log(1.0 + jnp.exp(-jnp.abs(z)))


def _cd_in_kernel(x_ref, mod_ref, nw_ref, w_ref, wa_ref, wdec_ref, bdec_ref, cos_ref, sin_ref,
                  f_ref, qk_ref, v_ref, g_ref, la_ref, *, rope):
    for rows in [slice(0, x_ref.shape[0])]:
        h = _rms_mod(x_ref[rows, :], nw_ref[...], mod_ref[0, 0:1, :], mod_ref[0, 1:2, :]).astype(BF16)
        f = _dot(h, w_ref[:, :O_Q])
        for a in range(f.shape[0] // GRID_W):
            r0 = (rows.start // GRID_W + a) * FFT_PITCH
            f_ref[r0:r0 + GRID_W, :] = f[a * GRID_W:(a + 1) * GRID_W]
            f_ref[r0 + GRID_W:r0 + FFT_PITCH, :] = jnp.zeros((FFT_PITCH - GRID_W, f.shape[1]), F32)
        qk = _dot(h, w_ref[:, O_Q:O_V])
        if rope:
            reps = qk.shape[1] // LANES
            cos = jnp.concatenate([cos_ref[rows, :]] * reps, axis=1)
            sin = jnp.concatenate([sin_ref[rows, :]] * reps, axis=1)
            w = qk.shape[1]
            q16 = GLA_DK // 4
            first = (lax.broadcasted_iota(jnp.int32, (1, w), 1) % (2 * q16)) < q16
            partner = jnp.where(first, pltpu.roll(qk, w - q16, 1), pltpu.roll(qk, q16, 1))
            qk = qk * cos + partner * sin
        is_q = lax.broadcasted_iota(jnp.int32, (1, qk.shape[1]), 1) < GLA_KEY_WIDTH
        qk_ref[rows, :] = jnp.where(is_q, qk * GLA_DK ** -0.5, qk)
        v_ref[rows, :] = _dot(h, w_ref[:, O_V:O_G]).astype(BF16)
        g_ref[rows, :] = _dot(h, w_ref[:, O_G:O_A])
        a = _dot(h, wa_ref[...]).astype(BF16)
        la_ref[rows, :] = _log_sigmoid(_dot(a, wdec_ref[...]) + bdec_ref[...]) * (1.0 / GLA_GATE_TEMP)


def _cd_in(x2, mod, tiles_per_batch, tm, nw, w_bf, wa_bf, wdec_bf, bdec, cos, sin, rope):
    m = x2.shape[0]
    row = lambda w: pl.BlockSpec((tm, w), lambda i: (i, 0))
    if rope:
        tab = pl.BlockSpec((tm, LANES), lambda i: (i % tiles_per_batch, 0))
    else:
        tab = pl.BlockSpec((tm, LANES), lambda i: (0, 0))
    w512 = 2 * GLA_KEY_WIDTH
    pad = lambda n: n // GRID_W * FFT_PITCH
    outs = (jax.ShapeDtypeStruct((pad(m), FN_WIDTH), F32), jax.ShapeDtypeStruct((m, w512), F32),
            jax.ShapeDtypeStruct((m, GLA_VAL_WIDTH), BF16), jax.ShapeDtypeStruct((m, GLA_VAL_WIDTH), F32),
            jax.ShapeDtypeStruct((m, w512), F32))
    return pl.pallas_call(
        functools.partial(_cd_in_kernel, rope=rope),
        out_shape=outs,
        grid=(m // tm,),
        in_specs=[row(D_MODEL), _mod_spec(tiles_per_batch), _resident((1, D_MODEL)),
                  _resident(w_bf.shape), _resident(wa_bf.shape), _resident(wdec_bf.shape),
                  _resident((1, w512)), tab, tab],
        out_specs=(pl.BlockSpec((pad(tm), FN_WIDTH), lambda i: (i, 0)),
                   row(w512), row(GLA_VAL_WIDTH), row(GLA_VAL_WIDTH), row(w512)),
        compiler_params=_cparams("parallel"),
        name="cd_in_rope" if rope else "cd_in_ctx",
    )(x2, mod, nw, w_bf, wa_bf, wdec_bf, bdec, cos, sin)


def _fft_tables(t):
    n1 = GRID_W
    assert t == n1 * n1
    a = np.arange(n1)
    ang1 = 2 * np.pi * np.outer(a, a) / n1
    cs = np.concatenate([np.cos(ang1), -np.sin(ang1)], axis=0)
    kap = a[:, None, None] + n1 * a[None, :, None]
    ang2 = 2 * np.pi * kap * a[None, None, :] / t
    gr, gi = np.cos(ang2), -np.sin(ang2)
    g = np.concatenate([np.concatenate([gr, -gi], axis=2), np.concatenate([gi, gr], axis=2)], axis=1)
    c = np.arange(FN_GROUP_CH)
    ang3 = 2 * np.pi * np.outer(c, c) / FN_GROUP_CH
    norm = 1.0 / np.sqrt(t * FN_GROUP_CH)
    return (jnp.asarray(cs, F32), jnp.asarray(g, F32),
            jnp.asarray(np.cos(ang3) * norm, F32), jnp.asarray(np.sin(ang3) * norm, F32))


def _fft_kernel(x_ref, cs_ref, g_ref, cc_ref, sc_ref, o_ref, p_scr, zr_scr, zi_scr):
    n1 = GRID_W
    cs = cs_ref[...].astype(BF16)

    def stage1(b, carry):
        xb = x_ref[0, pl.ds(b, n1, stride=FFT_PITCH), :].astype(BF16)
        p_scr[pl.ds(pl.multiple_of(b * FFT_PITCH2, 8), 2 * n1), :] = _dot(cs, xb)
        return carry

    lax.fori_loop(0, n1, stage1, 0, unroll=FFT_UNROLL)

    def stage2(k1, carry):
        pr = p_scr[pl.ds(k1, n1, stride=FFT_PITCH2), :]
        pi = p_scr[pl.ds(n1 + k1, n1, stride=FFT_PITCH2), :]
        z = _dot(g_ref[k1].astype(BF16), jnp.concatenate([pr, pi], axis=0).astype(BF16))
        rows = pl.ds(pl.multiple_of(k1 * FFT_PITCH, 8), n1)
        zr_scr[rows, :] = z[:n1]
        zi_scr[rows, :] = z[n1:]
        return carry

    lax.fori_loop(0, n1, stage2, 0, unroll=FFT_UNROLL)

    ccsc = jnp.concatenate([cc_ref[...], sc_ref[...]], axis=0).astype(BF16)

    def stage3(j, carry):
        parts = []
        for u in range(FFT_UNROLL):
            k2 = j * FFT_UNROLL + u
            parts.append(jnp.concatenate([zr_scr[pl.ds(k2, n1, stride=FFT_PITCH), :],
                                          zi_scr[pl.ds(k2, n1, stride=FFT_PITCH), :]], axis=1))
        out = _dot(jnp.concatenate(parts, axis=0).astype(BF16), ccsc)
        o_ref[0, pl.ds(pl.multiple_of(j * FFT_UNROLL * n1, FFT_UNROLL * n1), FFT_UNROLL * n1), :] = out.astype(BF16)
        return carry

    lax.fori_loop(0, n1 // FFT_UNROLL, stage3, 0)


def _fourier(f_padded, tables):
    b, tp, w = f_padded.shape
    t = tp // FFT_PITCH * GRID_W
    cw = FN_GROUP_CH
    cs, g, cc, sc = tables
    return pl.pallas_call(
        _fft_kernel,
        out_shape=jax.ShapeDtypeStruct((b, t, w), BF16),
        grid=(b, w // cw),
        in_specs=[pl.BlockSpec((1, tp, cw), lambda i, j: (i, 0, j)),
                  _resident(cs.shape), _resident(g.shape), _resident(cc.shape), _resident(sc.shape)],
        out_specs=pl.BlockSpec((1, t, cw), lambda i, j: (i, 0, j)),
        scratch_shapes=[pltpu.VMEM((GRID_W * FFT_PITCH2, cw), F32), pltpu.VMEM((tp, cw), F32),
                        pltpu.VMEM((tp, cw), F32)],
        compiler_params=_cparams("parallel", "parallel"),
        name="fourier_mix",
    )(f_padded, cs, g, cc, sc)


def _block_diag(x, block):
    head = lax.broadcasted_iota(jnp.int32, (1, x.shape[1]), 1) // block
    return jnp.concatenate([jnp.where(head == h, x, jnp.zeros_like(x)) for h in range(GLA_HEADS)], axis=0)


def _cum_rows(x, reverse):
    n, w = x.shape
    row = lax.broadcasted_iota(jnp.int32, (n, 1), 0)
    s = 1
    while s < n:
        if s < 8:
            if reverse:
                shifted = jnp.where(row < n - s, pltpu.roll(x, n - s, 0), 0.0)
            else:
                shifted = jnp.where(row >= s, pltpu.roll(x, s, 0), 0.0)
        else:
            pad = jnp.zeros((s, w), F32)
            shifted = jnp.concatenate([x[s:], pad] if reverse else [pad, x[:n - s]], axis=0)
        x = x + shifted
        s *= 2
    return x


def _gla_prep(q, k, la, reverse, want_out):
    c = GLA_CHUNK
    b = _cum_rows(la, reverse)
    b_last = b[0:1, :] if reverse else b[c - 1:c, :]
    kd = (k * jnp.exp(b_last - b)).astype(BF16)
    decay = jnp.exp(b_last)
    if not want_out:
        return kd, decay
    qe = (q * jnp.exp(b)).astype(BF16)
    ke = _block_diag(k * jnp.exp(-b), GLA_DK).astype(BF16)
    l = lax.broadcasted_iota(jnp.int32, (c, GLA_HEADS * c), 0)
    m = lax.broadcasted_iota(jnp.int32, (c, GLA_HEADS * c), 1) % c
    att = jnp.where((l <= m) if reverse else (l >= m), _dot_nt(qe, ke), 0.0).astype(BF16)
    return kd, decay, qe, att


def _gla_state_step(kd, decay, v, state):
    v_rows = jnp.concatenate([v[:, h * GLA_DV:(h + 1) * GLA_DV] for h in range(GLA_HEADS)], axis=0)
    return state * decay + _dot_tn(v_rows, _block_diag(kd, GLA_DK))


def _gla_apply(prep, v, state):
    c = GLA_CHUNK
    kd, decay, qe, att = prep
    inter = _dot_nt(_block_diag(qe, GLA_DK), state.astype(BF16))
    out = _dot(att, _block_diag(v, GLA_DV)) + jnp.concatenate(
        [inter[h * c:(h + 1) * c] for h in range(GLA_HEADS)], axis=1)
    return out, _gla_state_step(kd, decay, v, state)


def _gla_kernel(qkf_ref, vf_ref, laf_ref, gf_ref, qkb_ref, vb_ref, lab_ref, gb_ref,
                kc_ref, vc_ref, lac_ref, hw_ref, o_ref, sf_scr, sb_scr, acc_scr, *, n_tiles, tile):
    c = GLA_CHUNK
    i = pl.program_id(1)
    kw = GLA_KEY_WIDTH
    n_chunks = tile // c

    @pl.when(i == 0)
    def _():
        n_ctx = kc_ref.shape[1] // c
        sf = jnp.zeros((GLA_DV, kw), F32)
        sb = jnp.zeros((GLA_DV, kw), F32)
        for n in range(n_ctx):
            rf = slice(n * c, (n + 1) * c)
            rb = slice((n_ctx - 1 - n) * c, (n_ctx - n) * c)
            sf = _gla_state_step(*_gla_prep(None, kc_ref[0, rf, :], lac_ref[0, rf, :kw], False, False),
                                 vc_ref[0, rf, :], sf)
            sb = _gla_state_step(*_gla_prep(None, kc_ref[0, rb, :], lac_ref[0, rb, kw:], True, False),
                                 vc_ref[0, rb, :], sb)
        sf_scr[...] = sf
        sb_scr[...] = sb

    hw = hw_ref[...]

    def finish(o, g):
        parts = []
        for h in range(GLA_HEADS):
            oh = o[:, h * GLA_DV:(h + 1) * GLA_DV]
            parts.append(oh * lax.rsqrt(jnp.mean(oh * oh, axis=-1, keepdims=True) + RMS_EPS))
        return (jnp.concatenate(parts, axis=1) * hw * (g * jax.nn.sigmoid(g))).astype(BF16)

    def sweep(second_pass):
        def emit(o, g_ref, rloc, row0):
            rows = pl.ds(pl.multiple_of(row0, c), c)
            if second_pass:
                o_ref[0, rows, :] = finish(o + acc_scr[rows, :], g_ref[0, rloc, :])
            else:
                acc_scr[rows, :] = o

        def rows_of(n):
            nb = n_chunks - 1 - n
            return slice(n * c, (n + 1) * c), slice(nb * c, (nb + 1) * c)

        def prep(n):
            rf, rb = rows_of(n)
            qkf, qkb = qkf_ref[0, rf, :], qkb_ref[0, rb, :]
            return (_gla_prep(qkf[:, :kw], qkf[:, kw:], laf_ref[0, rf, :], False, True),
                    _gla_prep(qkb[:, :kw], qkb[:, kw:], lab_ref[0, rb, :], True, True))

        sf, sb = sf_scr[...], sb_scr[...]
        ready = {n: prep(n) for n in range(min(GLA_PREP_AHEAD, n_chunks))}
        for n in range(n_chunks):
            rf, rb = rows_of(n)
            pf, pb = ready.pop(n)
            of, sf = _gla_apply(pf, vf_ref[0, rf, :], sf)
            emit(of, gf_ref, rf, i * tile + rf.start)
            ob, sb = _gla_apply(pb, vb_ref[0, rb, :], sb)
            emit(ob, gb_ref, rb, (n_tiles - 1 - i) * tile + rb.start)
            if n + GLA_PREP_AHEAD < n_chunks:
                ready[n + GLA_PREP_AHEAD] = prep(n + GLA_PREP_AHEAD)
        sf_scr[...] = sf
        sb_scr[...] = sb

    @pl.when(i < n_tiles // 2)
    def _():
        sweep(False)

    @pl.when(i >= n_tiles // 2)
    def _():
        sweep(True)


def _gla(qk, v, la, g, qk_c, v_c, la_c, head_w):
    b, t, _ = qk.shape
    tile = GLA_TILE
    n_tiles = t // tile
    kw, vw = GLA_KEY_WIDTH, GLA_VAL_WIDTH
    fwd = lambda w, col: pl.BlockSpec((1, tile, w), lambda bi, i: (bi, i, col))
    bwd = lambda w, col: pl.BlockSpec((1, tile, w), lambda bi, i: (bi, n_tiles - 1 - i, col))
    ctx = lambda w: pl.BlockSpec((1, CTX_LEN, w), lambda bi, i: (bi, 0, 0))
    half = n_tiles // 2
    gate_fwd = pl.BlockSpec((1, tile, vw), lambda bi, i: (bi, jnp.maximum(i, half), 0))
    gate_bwd = pl.BlockSpec((1, tile, vw), lambda bi, i: (bi, n_tiles - 1 - jnp.maximum(i, half), 0))
    return pl.pallas_call(
        functools.partial(_gla_kernel, n_tiles=n_tiles, tile=tile),
        out_shape=jax.ShapeDtypeStruct((b, t, vw), BF16),
        grid=(b, n_tiles),
        in_specs=[fwd(2 * kw, 0), fwd(vw, 0), fwd(kw, 0), gate_fwd,
                  bwd(2 * kw, 0), bwd(vw, 0), bwd(kw, 1), gate_bwd,
                  pl.BlockSpec((1, CTX_LEN, kw), lambda bi, i: (bi, 0, 1)), ctx(vw), ctx(2 * kw),
                  pl.BlockSpec((1, vw), lambda bi, i: (0, 0))],
        out_specs=pl.BlockSpec((1, t, vw), lambda bi, i: (bi, 0, 0)),
        scratch_shapes=[pltpu.VMEM((GLA_DV, kw), F32), pltpu.VMEM((GLA_DV, kw), F32), pltpu.VMEM((t, vw), F32)],
        compiler_params=_cparams("parallel", "arbitrary"),
        name="gla",
    )(qk, v, la, g, qk, v, la, g, qk_c, v_c, la_c, head_w)


def kernel(x, c, ctx, c_ctx, ada_w, ada_b, norm_mix_w, norm_ffn_w, ffn_w_gate, ffn_w_up, ffn_w_down,
           ab_w_in, ab_w_out, ab_sgu_norm_w, ab_sgu_w, ab_sgu_b, ab_rel_bias,
           cd_w_in, cd_w_out, cd_decay_w_fwd, cd_decay_b_fwd, cd_decay_w_bwd, cd_decay_b_bwd, cd_head_norm_w,
           final_norm_w):
    bsz, t, d = x.shape
    n_ctx = ctx.shape[1]
    tpb = t // ROW_TILE
    x2 = x.reshape(bsz * t, d)
    ctx2 = ctx.reshape(bsz * n_ctx, d)
    row = lambda w: w.reshape(1, -1)

    cc = jnp.concatenate([c, c_ctx[None, :], jnp.zeros((16 - bsz - 1, d), F32)], axis=0)
    mod = _ada(cc, ada_w, ada_b)
    mod_x = [mod[i, :bsz].reshape(bsz, 6, d) for i in range(2)]
    mod_c = [mod[i, bsz:bsz + 1].reshape(1, 6, d) for i in range(2)]
    ffn = [(ffn_w_gate[i].astype(BF16), ffn_w_up[i].astype(BF16), ffn_w_down[i].astype(BF16)) for i in range(2)]

    w_in = ab_w_in[0].astype(BF16)
    sgu = (row(ab_sgu_norm_w[0]), ab_sgu_w[0].astype(BF16),
           jnp.broadcast_to(ab_sgu_b[0][:, :, None], (A_GROUPS, A_CHUNK, LANES)))
    a_l, q_l, k_l, v_l = _ab_in(x2, mod_x[0], tpb, ROW_TILE, row(norm_mix_w[0]), w_in, *sgu)
    a_c, q_c, k_c, v_c = _ab_in(ctx2, mod_c[0], None, ROW_TILE, row(norm_mix_w[0]), w_in, *sgu)
    seq = lambda z, n: z.reshape(bsz, n, z.shape[-1])
    b_l = _na(seq(q_l, t), seq(k_l, t), seq(v_l, t), seq(k_c, n_ctx), seq(v_c, n_ctx), _na_bias_table(ab_rel_bias[0]))
    b_c = _ctx_attn(seq(q_c, n_ctx), seq(k_c, n_ctx), seq(v_c, n_ctx))
    wo = ab_w_out[0].astype(BF16)
    x2 = _out_ffn(a_l, b_l.reshape(bsz * t, -1), x2, mod_x[0], t // FFN_ROW_TILE, FFN_ROW_TILE, wo,
                  row(norm_ffn_w[0]), *ffn[0], row(final_norm_w), False)
    ctx2 = _out_ffn(a_c, b_c.reshape(bsz * n_ctx, -1), ctx2, mod_c[0], None, ROW_TILE, wo, row(norm_ffn_w[0]),
                    *ffn[0], row(final_norm_w), False)

    w_in = cd_w_in[0]
    w_main = w_in[:, :O_A].astype(BF16)
    w_a = jnp.pad(w_in[:, O_A:], ((0, 0), (0, LANES - 2 * GLA_LOW_RANK))).astype(BF16)
    w_dec = jnp.zeros((LANES, 2 * GLA_KEY_WIDTH), F32)
    w_dec = w_dec.at[:GLA_LOW_RANK, :GLA_KEY_WIDTH].set(cd_decay_w_fwd[0])
    w_dec = w_dec.at[GLA_LOW_RANK:2 * GLA_LOW_RANK, GLA_KEY_WIDTH:].set(cd_decay_w_bwd[0]).astype(BF16)
    b_dec = jnp.concatenate([cd_decay_b_fwd[0], cd_decay_b_bwd[0]]).reshape(1, -1)
    cos, sin = _rope_tables(t)
    f_l, qk_l, v_l, g_l, la_l = _cd_in(x2, mod_x[1], tpb, ROW_TILE, row(norm_mix_w[1]), w_main, w_a, w_dec, b_dec,
                                      cos, sin, True)
    _, qk_c, v_c, _, la_c = _cd_in(ctx2, mod_c[1], None, ROW_TILE, row(norm_mix_w[1]), w_main, w_a, w_dec, b_dec,
                                   cos, sin, False)
    fm = _fourier(f_l.reshape(bsz, -1, FN_WIDTH), _fft_tables(t))
    head_w = jnp.tile(cd_head_norm_w[0], GLA_HEADS).reshape(1, -1)
    go = _gla(seq(qk_l, t), seq(v_l, t), seq(la_l, t), seq(g_l, t),
              seq(qk_c, n_ctx), seq(v_c, n_ctx), seq(la_c, n_ctx), head_w)
    out = _out_ffn(fm.reshape(bsz * t, -1), go.reshape(bsz * t, -1), x2, mod_x[1], t // FFN_ROW_TILE, FFN_ROW_TILE,
                   cd_w_out[0].astype(BF16), row(norm_ffn_w[1]), *ffn[1], row(final_norm_w), True)
    return out.reshape(bsz, t, d)
```

```python
import functools

import numpy as np
import jax
import jax.numpy as jnp
from jax import lax
from jax.experimental import pallas as pl
from jax.experimental.pallas import tpu as pltpu

F32 = jnp.float32
BF16 = jnp.bfloat16

D_MODEL = 1024
CTX_LEN = 256
GRID_W = 64
A_WIDTH = 512
A_GROUPS = 4
A_CHUNK = 128
NA_HEADS = 8
NA_HEAD_DIM = 64
NA_WIDTH = 512
NA_WIN_ROWS = 8
NA_WIN_COLS = 16
O_NA_Q = 2 * A_WIDTH
FN_WIDTH = 512
FN_GROUP_CH = 128
GLA_HEADS = 4
GLA_VAL_WIDTH = 512
GLA_KEY_WIDTH = 256
GLA_DK = 64
GLA_DV = 128
GLA_LOW_RANK = 16
GLA_GATE_TEMP = 16.0
GLA_CHUNK = 64
O_Q = FN_WIDTH
O_V = O_Q + 2 * GLA_KEY_WIDTH
O_G = O_V + GLA_VAL_WIDTH
O_A = O_G + GLA_VAL_WIDTH
FFN_HIDDEN = 2816
ROPE_BASE = 10000.0
RMS_EPS = 1e-6
NEG_INF = -1e30

LANES = 128
VMEM_LIMIT_BYTES = 56 * 2 ** 20

ROW_TILE = 1024
FFN_ROW_TILE = 1024
ROW_SUBTILE = 256
FFN_SPLIT = 2
GLA_TILE = 1024
NA_STAGE_LAG = 2
FFT_UNROLL = 32
FFT_PITCH = GRID_W + 8
FFT_PITCH2 = 2 * GRID_W + 8
GLA_PREP_AHEAD = 2


def _cparams(*sem):
    return pltpu.CompilerParams(dimension_semantics=sem, vmem_limit_bytes=VMEM_LIMIT_BYTES)


def _resident(shape):
    nd = len(shape)
    return pl.BlockSpec(shape, lambda *_: (0,) * nd, pipeline_mode=pl.Buffered(1))


def _dot(a, b):
    return jnp.dot(a, b, preferred_element_type=F32)


def _dot_nt(a, b):
    return lax.dot_general(a, b, (((1,), (1,)), ((), ())), preferred_element_type=F32)


def _dot_tn(a, b):
    return lax.dot_general(a, b, (((0,), (0,)), ((), ())), preferred_element_type=F32)


def _subtiles(n_rows):
    step = min(ROW_SUBTILE, n_rows)
    return [slice(r, r + step) for r in range(0, n_rows, step)]


def _rms(x, w):
    return x * lax.rsqrt(jnp.mean(x * x, axis=-1, keepdims=True) + RMS_EPS) * w


def _rms_mod(x, w, shift, scale):
    return x * lax.rsqrt(jnp.mean(x * x, axis=-1, keepdims=True) + RMS_EPS) * (w * (1.0 + scale)) + shift


def _ada_kernel(c_ref, w_ref, b_ref, o_ref):
    c = c_ref[...]
    s = c * jax.nn.sigmoid(c)
    o_ref[0] = _dot(s.astype(BF16), w_ref[0].astype(BF16)) + b_ref[0]


def _ada(cc, ada_w, ada_b):
    depth, d, n = ada_w.shape
    r = cc.shape[0]
    tn = 1536
    return pl.pallas_call(
        _ada_kernel,
        out_shape=jax.ShapeDtypeStruct((depth, r, n), F32),
        grid=(depth, n // tn),
        in_specs=[pl.BlockSpec((r, d), lambda i, j: (0, 0)),
                  pl.BlockSpec((1, d, tn), lambda i, j: (i, 0, j)),
                  pl.BlockSpec((1, 1, tn), lambda i, j: (i, 0, j))],
        out_specs=pl.BlockSpec((1, r, tn), lambda i, j: (i, 0, j)),
        compiler_params=_cparams("arbitrary", "arbitrary"),
        name="ada_mod",
    )(cc, ada_w, ada_b.reshape(depth, 1, n))


def _mod_spec(tiles_per_batch):
    if tiles_per_batch is None:
        return pl.BlockSpec((1, 6, D_MODEL), lambda i: (0, 0, 0))
    return pl.BlockSpec((1, 6, D_MODEL), lambda i: (i // tiles_per_batch, 0, 0))


def _ab_in_kernel(x_ref, mod_ref, nw_ref, w_ref, snw_ref, sguw_ref, sgub_ref,
                  a_ref, q_ref, k_ref, v_ref, *, tm):
    tiles = _subtiles(tm)
    hs = [_rms_mod(x_ref[rows, :], nw_ref[...], mod_ref[0, 0:1, :], mod_ref[0, 1:2, :]).astype(BF16)
          for rows in tiles]
    uvs = [_dot(h, w_ref[:, :O_NA_Q]) for h in hs]
    for rows, h, uv in zip(tiles, hs, uvs):
        qkv = _dot(h, w_ref[:, O_NA_Q:])
        q_ref[rows, :] = (qkv[:, :NA_WIDTH] * NA_HEAD_DIM ** -0.5).astype(BF16)
        k_ref[rows, :] = qkv[:, NA_WIDTH:2 * NA_WIDTH].astype(BF16)
        v_ref[rows, :] = qkv[:, 2 * NA_WIDTH:].astype(BF16)
        uv = jax.nn.gelu(uv)
        u = uv[:, :A_WIDTH]
        v = _rms(uv[:, A_WIDTH:], snw_ref[...]).astype(BF16)
        for ci in range(uv.shape[0] // A_CHUNK):
            rs = slice(ci * A_CHUNK, (ci + 1) * A_CHUNK)
            out_rows = slice(rows.start + rs.start, rows.start + rs.stop)
            for g in range(A_GROUPS):
                cs = slice(g * LANES, (g + 1) * LANES)
                gate = _dot(sguw_ref[g], v[rs, cs]) + sgub_ref[g]
                a_ref[out_rows, cs] = (u[rs, cs] * gate).astype(BF16)


def _ab_in(x2, mod, tiles_per_batch, tm, nw, w_bf, snw, sguw_bf, sgub_exp):
    m = x2.shape[0]
    n_in = w_bf.shape[1]
    row = lambda w: pl.BlockSpec((tm, w), lambda i: (i, 0))
    out = jax.ShapeDtypeStruct((m, A_WIDTH), BF16)
    return pl.pallas_call(
        functools.partial(_ab_in_kernel, tm=tm),
        out_shape=(out, out, out, out),
        grid=(m // tm,),
        in_specs=[row(D_MODEL), _mod_spec(tiles_per_batch), _resident((1, D_MODEL)),
                  _resident((D_MODEL, n_in)), _resident((1, A_WIDTH)),
                  _resident((A_GROUPS, A_CHUNK, A_CHUNK)), _resident((A_GROUPS, A_CHUNK, LANES))],
        out_specs=(row(A_WIDTH),) * 4,
        compiler_params=_cparams("parallel"),
        name="ab_in",
    )(x2, mod, nw, w_bf, snw, sguw_bf, sgub_exp)


def _na_bias_table(rel_bias):
    w, nrow = GRID_W, 2 * NA_WIN_ROWS - 1
    col = np.arange(w)
    col_start = np.clip(col - NA_WIN_COLS // 2, 0, w - NA_WIN_COLS)
    col_mask = (col[None, :] >= col_start[:, None]) & (col[None, :] < col_start[:, None] + NA_WIN_COLS)
    lo = w - NA_WIN_COLS
    r_ext = jnp.pad(rel_bias, ((0, 0), (0, 0), (lo, 2 * w - lo - (2 * NA_WIN_COLS - 1))))
    flat = jnp.tile(r_ext, (1, 1, w))[:, :, :w * (2 * w - 1)]
    toe = flat.reshape(NA_HEADS, nrow, w, 2 * w - 1)[:, :, :, w - 1:]
    toe = jnp.where(col_mask, toe, NEG_INF).reshape(NA_HEADS // 2, 2, nrow, w, w)
    strips = [jnp.concatenate([toe[:, :, j - o + NA_WIN_ROWS - 1] for j in range(NA_WIN_ROWS)], axis=-1)
              for o in range(NA_WIN_ROWS)]
    return jnp.stack([s.reshape(NA_HEADS // 2, 2 * w, NA_WIN_ROWS * w) for s in strips], axis=1)


def _stack_heads(qb):
    first = lax.broadcasted_iota(jnp.int32, (1, LANES), 1) < NA_HEAD_DIM
    zero = jnp.zeros_like(qb)
    return jnp.concatenate([jnp.where(first, qb, zero), jnp.where(first, zero, qb)], axis=0)


def _unstack_heads(r):
    n = r.shape[0] // 2
    first = lax.broadcasted_iota(jnp.int32, (1, LANES), 1) < NA_HEAD_DIM
    return jnp.where(first, r[:n], r[n:])


def _na_kernel(q_ref, k_ref, v_ref, kc_ref, vc_ref, tab_ref, o_ref, kt_scr, kct_scr, *, rows):
    t = rows * GRID_W
    kt_scr[0] = k_ref[0].T
    kt_scr[1, :, :t - GRID_W] = k_ref[0, GRID_W:, :].T
    kct_scr[...] = kc_ref[0].T
    kct = kct_scr[...]
    vc = vc_ref[0]
    win = NA_WIN_ROWS * GRID_W

    def window(r):
        return min(max(r - NA_WIN_ROWS // 2, 0), rows - NA_WIN_ROWS)

    def scores(r):
        start = window(r)
        odd = start % 2
        kt0 = (start - odd) * GRID_W
        qs = _stack_heads(q_ref[0, r * GRID_W:(r + 1) * GRID_W, :])
        return _dot(qs, kt_scr[odd, :, kt0:kt0 + win]) + tab_ref[0, r - start], _dot(qs, kct)

    def softmax(s):
        s_nb, s_cx = s
        m = jnp.maximum(jnp.max(s_nb, axis=-1, keepdims=True), jnp.max(s_cx, axis=-1, keepdims=True))
        return jnp.exp(s_nb - m).astype(BF16), jnp.exp(s_cx - m).astype(BF16)

    ones_nb = jnp.ones((win, LANES), BF16)
    vc_aug = jnp.concatenate([vc, jnp.ones((vc.shape[0], LANES), BF16)], axis=1)

    def values(r, p):
        e_nb, e_cx = p
        k0 = window(r) * GRID_W
        vb_aug = jnp.concatenate([v_ref[0, k0:k0 + win, :], ones_nb], axis=1)
        acc = _dot(e_nb, vb_aug) + _dot(e_cx, vc_aug)
        o_ref[0, r * GRID_W:(r + 1) * GRID_W, :] = _unstack_heads(acc[:, :LANES] / acc[:, LANES:]).astype(BF16)

    lag = NA_STAGE_LAG
    s_vals, p_vals = {}, {}
    for r in range(rows + 2 * lag):
        if 0 <= r - 2 * lag < rows:
            values(r - 2 * lag, p_vals.pop(r - 2 * lag))
        if 0 <= r - lag < rows:
            p_vals[r - lag] = softmax(s_vals.pop(r - lag))
        if r < rows:
            s_vals[r] = scores(r)


def _na(q, k, v, kc, vc, table):
    b, t, _ = q.shape
    rows = t // GRID_W
    lat = pl.BlockSpec((1, t, LANES), lambda j, i: (i, 0, j))
    ctx = pl.BlockSpec((1, CTX_LEN, LANES), lambda j, i: (i, 0, j))
    return pl.pallas_call(
        functools.partial(_na_kernel, rows=rows),
        out_shape=jax.ShapeDtypeStruct((b, t, NA_WIDTH), BF16),
        grid=(NA_HEADS // 2, b),
        in_specs=[lat, lat, lat, ctx, ctx,
                  pl.BlockSpec((1, NA_WIN_ROWS, 2 * GRID_W, NA_WIN_ROWS * GRID_W), lambda j, i: (j, 0, 0, 0))],
        out_specs=lat,
        scratch_shapes=[pltpu.VMEM((2, LANES, t), BF16), pltpu.VMEM((LANES, CTX_LEN), BF16)],
        compiler_params=_cparams("parallel", "parallel"),
        name="na_attn",
    )(q, k, v, kc, vc, table)


def _ctx_attn_kernel(q_ref, k_ref, v_ref, o_ref):
    for j in range(NA_HEADS // 2):
        cols = slice(j * LANES, (j + 1) * LANES)
        s = _dot_nt(_stack_heads(q_ref[0, :, cols]), k_ref[0, :, cols])
        e = jnp.exp(s - jnp.max(s, axis=-1, keepdims=True))
        acc = _dot(e.astype(BF16), v_ref[0, :, cols])
        o_ref[0, :, cols] = _unstack_heads(acc / jnp.sum(e, axis=-1, keepdims=True)).astype(BF16)


def _ctx_attn(q, k, v):
    b = q.shape[0]
    spec = pl.BlockSpec((1, CTX_LEN, NA_WIDTH), lambda i: (i, 0, 0))
    return pl.pallas_call(
        _ctx_attn_kernel,
        out_shape=jax.ShapeDtypeStruct((b, CTX_LEN, NA_WIDTH), BF16),
        grid=(b,),
        in_specs=[spec, spec, spec],
        out_specs=spec,
        compiler_params=_cparams("parallel"),
        name="ctx_attn",
    )(q, k, v)


def _out_ffn_kernel(a_ref, b_ref, x_ref, mod_ref, wo_ref, nw_ref, wg_ref, wu_ref, wd_ref, fnw_ref, o_ref,
                    *, final):
    half = wo_ref.shape[0] // 2
    piece = FFN_HIDDEN // FFN_SPLIT
    tiles = _subtiles(x_ref.shape[0])
    ys = [_dot(a_ref[rows, :], wo_ref[:half, :]) + _dot(b_ref[rows, :], wo_ref[half:, :]) for rows in tiles]
    x1s = [x_ref[rows, :] + mod_ref[0, 2:3, :] * y for rows, y in zip(tiles, ys)]
    hs = [_rms_mod(x1, nw_ref[...], mod_ref[0, 3:4, :], mod_ref[0, 4:5, :]).astype(BF16) for x1 in x1s]
    accs = [None] * len(tiles)
    for j in range(FFN_SPLIT):
        cs = slice(j * piece, (j + 1) * piece)
        gs = [_dot(h, wg_ref[:, cs]) for h in hs]
        acts = [(g * jax.nn.sigmoid(g) * _dot(h, wu_ref[:, cs])).astype(BF16) for g, h in zip(gs, hs)]
        parts = [_dot(act, wd_ref[cs, :]) for act in acts]
        accs = [part if acc is None else acc + part for acc, part in zip(accs, parts)]
    for rows, x1, acc in zip(tiles, x1s, accs):
        x2 = x1 + mod_ref[0, 5:6, :] * acc
        o_ref[rows, :] = _rms(x2, fnw_ref[...]) if final else x2


def _out_ffn(a, b, x2, mod, tiles_per_batch, tm, wo_bf, nw, layer, wg_bf, wu_bf, wd_bf, fnw, final):
    m = x2.shape[0]
    row = lambda w: pl.BlockSpec((tm, w), lambda i: (i, 0))
    layer_w = lambda w: pl.BlockSpec((None,) + w.shape[1:], lambda i: (layer, 0, 0), pipeline_mode=pl.Buffered(1))
    return pl.pallas_call(
        functools.partial(_out_ffn_kernel, final=final),
        out_shape=jax.ShapeDtypeStruct((m, D_MODEL), F32),
        grid=(m // tm,),
        in_specs=[row(a.shape[1]), row(b.shape[1]), row(D_MODEL), _mod_spec(tiles_per_batch),
                  _resident(wo_bf.shape), _resident((1, D_MODEL)), layer_w(wg_bf),
                  layer_w(wu_bf), layer_w(wd_bf), _resident((1, D_MODEL))],
        out_specs=row(D_MODEL),
        compiler_params=_cparams("parallel"),
        name="out_ffn_final" if final else "out_ffn",
    )(a, b, x2, mod, wo_bf, nw, wg_bf, wu_bf, wd_bf, fnw)


def _rope_tables(t):
    half = GLA_DK // 4
    inv_freq = ROPE_BASE ** (-np.arange(half, dtype=np.float64) / half)
    pos = np.arange(t)
    ang_r = (pos // GRID_W)[:, None] * inv_freq[None, :]
    ang_c = (pos % GRID_W)[:, None] * inv_freq[None, :]
    cos = np.concatenate([np.cos(ang_r)] * 2 + [np.cos(ang_c)] * 2, axis=1)
    sin = np.concatenate([-np.sin(ang_r), np.sin(ang_r), -np.sin(ang_c), np.sin(ang_c)], axis=1)
    return (jnp.asarray(np.tile(cos, (1, 2)), F32), jnp.asarray(np.tile(sin, (1, 2)), F32))


def _log_sigmoid(z):
    return jnp.minimum(z, 0.0) - jnp.log(1.0 + jnp.exp(-jnp.abs(z)))


def _cd_in_kernel(x_ref, mod_ref, nw_ref, w_ref, wa_ref, wdec_ref, bdec_ref, cos_ref, sin_ref,
                  f_ref, qk_ref, v_ref, g_ref, la_ref, *, rope):
    for rows in [slice(0, x_ref.shape[0])]:
        h = _rms_mod(x_ref[rows, :], nw_ref[...], mod_ref[0, 0:1, :], mod_ref[0, 1:2, :]).astype(BF16)
        f = _dot(h, w_ref[:, :O_Q])
        for a in range(f.shape[0] // GRID_W):
            r0 = (rows.start // GRID_W + a) * FFT_PITCH
            f_ref[r0:r0 + GRID_W, :] = f[a * GRID_W:(a + 1) * GRID_W]
            f_ref[r0 + GRID_W:r0 + FFT_PITCH, :] = jnp.zeros((FFT_PITCH - GRID_W, f.shape[1]), F32)
        qk = _dot(h, w_ref[:, O_Q:O_V])
        if rope:
            reps = qk.shape[1] // LANES
            cos = jnp.concatenate([cos_ref[rows, :]] * reps, axis=1)
            sin = jnp.concatenate([sin_ref[rows, :]] * reps, axis=1)
            w = qk.shape[1]
            q16 = GLA_DK // 4
            first = (lax.broadcasted_iota(jnp.int32, (1, w), 1) % (2 * q16)) < q16
            partner = jnp.where(first, pltpu.roll(qk, w - q16, 1), pltpu.roll(qk, q16, 1))
            qk = qk * cos + partner * sin
        is_q = lax.broadcasted_iota(jnp.int32, (1, qk.shape[1]), 1) < GLA_KEY_WIDTH
        qk_ref[rows, :] = jnp.where(is_q, qk * GLA_DK ** -0.5, qk)
        v_ref[rows, :] = _dot(h, w_ref[:, O_V:O_G]).astype(BF16)
        g_ref[rows, :] = _dot(h, w_ref[:, O_G:O_A])
        a = _dot(h, wa_ref[...]).astype(BF16)
        la_ref[rows, :] = _log_sigmoid(_dot(a, wdec_ref[...]) + bdec_ref[...]) * (1.0 / GLA_GATE_TEMP)


def _cd_in(x2, mod, tiles_per_batch, tm, nw, w_bf, wa_bf, wdec_bf, bdec, cos, sin, rope):
    m = x2.shape[0]
    row = lambda w: pl.BlockSpec((tm, w), lambda i: (i, 0))
    if rope:
        tab = pl.BlockSpec((tm, LANES), lambda i: (i % tiles_per_batch, 0))
    else:
        tab = pl.BlockSpec((tm, LANES), lambda i: (0, 0))
    w512 = 2 * GLA_KEY_WIDTH
    pad = lambda n: n // GRID_W * FFT_PITCH
    outs = (jax.ShapeDtypeStruct((pad(m), FN_WIDTH), F32), jax.ShapeDtypeStruct((m, w512), F32),
            jax.ShapeDtypeStruct((m, GLA_VAL_WIDTH), BF16), jax.ShapeDtypeStruct((m, GLA_VAL_WIDTH), F32),
            jax.ShapeDtypeStruct((m, w512), F32))
    return pl.pallas_call(
        functools.partial(_cd_in_kernel, rope=rope),
        out_shape=outs,
        grid=(m // tm,),
        in_specs=[row(D_MODEL), _mod_spec(tiles_per_batch), _resident((1, D_MODEL)),
                  _resident(w_bf.shape), _resident(wa_bf.shape), _resident(wdec_bf.shape),
                  _resident((1, w512)), tab, tab],
        out_specs=(pl.BlockSpec((pad(tm), FN_WIDTH), lambda i: (i, 0)),
                   row(w512), row(GLA_VAL_WIDTH), row(GLA_VAL_WIDTH), row(w512)),
        compiler_params=_cparams("parallel"),
        name="cd_in_rope" if rope else "cd_in_ctx",
    )(x2, mod, nw, w_bf, wa_bf, wdec_bf, bdec, cos, sin)


def _fft_tables(t):
    n1 = GRID_W
    assert t == n1 * n1
    a = np.arange(n1)
    ang1 = 2 * np.pi * np.outer(a, a) / n1
    cs = np.concatenate([np.cos(ang1), -np.sin(ang1)], axis=0)
    kap = a[:, None, None] + n1 * a[None, :, None]
    ang2 = 2 * np.pi * kap * a[None, None, :] / t
    gr, gi = np.cos(ang2), -np.sin(ang2)
    g = np.concatenate([np.concatenate([gr, -gi], axis=2), np.concatenate([gi, gr], axis=2)], axis=1)
    c = np.arange(FN_GROUP_CH)
    ang3 = 2 * np.pi * np.outer(c, c) / FN_GROUP_CH
    norm = 1.0 / np.sqrt(t * FN_GROUP_CH)
    return (jnp.asarray(cs, F32), jnp.asarray(g, F32),
            jnp.asarray(np.cos(ang3) * norm, F32), jnp.asarray(np.sin(ang3) * norm, F32))


def _fft_kernel(xa_ref, xb_ref, cs_ref, g_ref, cc_ref, sc_ref, o_ref, p_scr, zr_scr, zi_scr):
    n1 = GRID_W
    x_refs = (xa_ref, xb_ref)
    groups = range(len(x_refs))
    lanes = lambda gi: slice(gi * FN_GROUP_CH, (gi + 1) * FN_GROUP_CH)
    cs = cs_ref[...].astype(BF16)

    def stage1(b, carry):
        xb = jnp.concatenate([r[0, pl.ds(b, n1, stride=FFT_PITCH), :] for r in x_refs], axis=1)
        p = _dot(cs, xb.astype(BF16))
        rows = pl.ds(pl.multiple_of(b * FFT_PITCH2, 8), 2 * n1)
        for gi in groups:
            p_scr[gi, rows, :] = p[:, lanes(gi)]
        return carry

    lax.fori_loop(0, n1, stage1, 0, unroll=FFT_UNROLL)

    def stage2(k1, carry):
        pr = jnp.concatenate([p_scr[gi, pl.ds(k1, n1, stride=FFT_PITCH2), :] for gi in groups], axis=1)
        pi = jnp.concatenate([p_scr[gi, pl.ds(n1 + k1, n1, stride=FFT_PITCH2), :] for gi in groups], axis=1)
        z = _dot(g_ref[k1].astype(BF16), jnp.concatenate([pr, pi], axis=0).astype(BF16))
        rows = pl.ds(pl.multiple_of(k1 * FFT_PITCH, 8), n1)
        for gi in groups:
            zr_scr[gi, rows, :] = z[:n1, lanes(gi)]
            zi_scr[gi, rows, :] = z[n1:, lanes(gi)]
        return carry

    lax.fori_loop(0, n1, stage2, 0, unroll=FFT_UNROLL)

    ccsc = jnp.concatenate([cc_ref[...], sc_ref[...]], axis=0).astype(BF16)

    def stage3(j, carry):
        out_rows = pl.ds(pl.multiple_of(j * FFT_UNROLL * n1, FFT_UNROLL * n1), FFT_UNROLL * n1)
        for gi in groups:
            parts = []
            for u in range(FFT_UNROLL):
                k2 = j * FFT_UNROLL + u
                parts.append(jnp.concatenate([zr_scr[gi, pl.ds(k2, n1, stride=FFT_PITCH), :],
                                              zi_scr[gi, pl.ds(k2, n1, stride=FFT_PITCH), :]], axis=1))
            out = _dot(jnp.concatenate(parts, axis=0).astype(BF16), ccsc)
            o_ref[0, out_rows, lanes(gi)] = out.astype(BF16)
        return carry

    lax.fori_loop(0, n1 // FFT_UNROLL, stage3, 0)


def _fourier(f_padded, tables):
    b, tp, w = f_padded.shape
    t = tp // FFT_PITCH * GRID_W
    cw = FN_GROUP_CH
    cs, g, cc, sc = tables
    group = lambda k: pl.BlockSpec((1, tp, cw), lambda i, j: (i, 0, 2 * j + k))
    return pl.pallas_call(
        _fft_kernel,
        out_shape=jax.ShapeDtypeStruct((b, t, w), BF16),
        grid=(b, w // (2 * cw)),
        in_specs=[group(0), group(1),
                  _resident(cs.shape), _resident(g.shape), _resident(cc.shape), _resident(sc.shape)],
        out_specs=pl.BlockSpec((1, t, 2 * cw), lambda i, j: (i, 0, j)),
        scratch_shapes=[pltpu.VMEM((2, GRID_W * FFT_PITCH2, cw), F32), pltpu.VMEM((2, tp, cw), F32),
                        pltpu.VMEM((2, tp, cw), F32)],
        compiler_params=_cparams("parallel", "parallel"),
        name="fourier_mix",
    )(f_padded, f_padded, cs, g, cc, sc)


def _block_diag(x, block):
    head = lax.broadcasted_iota(jnp.int32, (1, x.shape[1]), 1) // block
    return jnp.concatenate([jnp.where(head == h, x, jnp.zeros_like(x)) for h in range(GLA_HEADS)], axis=0)


def _cum_rows(x, reverse):
    n, w = x.shape
    row = lax.broadcasted_iota(jnp.int32, (n, 1), 0)
    s = 1
    while s < n:
        if s < 8:
            if reverse:
                shifted = jnp.where(row < n - s, pltpu.roll(x, n - s, 0), 0.0)
            else:
                shifted = jnp.where(row >= s, pltpu.roll(x, s, 0), 0.0)
        else:
            pad = jnp.zeros((s, w), F32)
            shifted = jnp.concatenate([x[s:], pad] if reverse else [pad, x[:n - s]], axis=0)
        x = x + shifted
        s *= 2
    return x


def _gla_prep(q, k, la, reverse, want_out):
    c = GLA_CHUNK
    b = _cum_rows(la, reverse)
    b_last = b[0:1, :] if reverse else b[c - 1:c, :]
    kd = (k * jnp.exp(b_last - b)).astype(BF16)
    decay = jnp.exp(b_last)
    if not want_out:
        return kd, decay
    qe = (q * jnp.exp(b)).astype(BF16)
    ke = _block_diag(k * jnp.exp(-b), GLA_DK).astype(BF16)
    l = lax.broadcasted_iota(jnp.int32, (c, GLA_HEADS * c), 0)
    m = lax.broadcasted_iota(jnp.int32, (c, GLA_HEADS * c), 1) % c
    att = jnp.where((l <= m) if reverse else (l >= m), _dot_nt(qe, ke), 0.0).astype(BF16)
    return kd, decay, qe, att


def _gla_state_step(kd, decay, v, state):
    v_rows = jnp.concatenate([v[:, h * GLA_DV:(h + 1) * GLA_DV] for h in range(GLA_HEADS)], axis=0)
    return state * decay + _dot_tn(v_rows, _block_diag(kd, GLA_DK))


def _gla_apply(prep, v, state):
    c = GLA_CHUNK
    kd, decay, qe, att = prep
    inter = _dot_nt(_block_diag(qe, GLA_DK), state.astype(BF16))
    out = _dot(att, _block_diag(v, GLA_DV)) + jnp.concatenate(
        [inter[h * c:(h + 1) * c] for h in range(GLA_HEADS)], axis=1)
    return out, _gla_state_step(kd, decay, v, state)


def _gla_kernel(qkf_ref, vf_ref, laf_ref, gf_ref, qkb_ref, vb_ref, lab_ref, gb_ref,
                kc_ref, vc_ref, lac_ref, hw_ref, o_ref, sf_scr, sb_scr, acc_scr, *, n_tiles, tile):
    c = GLA_CHUNK
    i = pl.program_id(1)
    kw = GLA_KEY_WIDTH
    n_chunks = tile // c

    @pl.when(i == 0)
    def _():
        n_ctx = kc_ref.shape[1] // c
        sf = jnp.zeros((GLA_DV, kw), F32)
        sb = jnp.zeros((GLA_DV, kw), F32)
        for n in range(n_ctx):
            rf = slice(n * c, (n + 1) * c)
            rb = slice((n_ctx - 1 - n) * c, (n_ctx - n) * c)
            sf = _gla_state_step(*_gla_prep(None, kc_ref[0, rf, :], lac_ref[0, rf, :kw], False, False),
                                 vc_ref[0, rf, :], sf)
            sb = _gla_state_step(*_gla_prep(None, kc_ref[0, rb, :], lac_ref[0, rb, kw:], True, False),
                                 vc_ref[0, rb, :], sb)
        sf_scr[...] = sf
        sb_scr[...] = sb

    hw = hw_ref[...]

    def finish(o, g):
        parts = []
        for h in range(GLA_HEADS):
            oh = o[:, h * GLA_DV:(h + 1) * GLA_DV]
            parts.append(oh * lax.rsqrt(jnp.mean(oh * oh, axis=-1, keepdims=True) + RMS_EPS))
        return (jnp.concatenate(parts, axis=1) * hw * (g * jax.nn.sigmoid(g))).astype(BF16)

    def sweep(second_pass):
        def emit(o, g_ref, rloc, row0):
            rows = pl.ds(pl.multiple_of(row0, c), c)
            if second_pass:
                o_ref[0, rows, :] = finish(o + acc_scr[rows, :], g_ref[0, rloc, :])
            else:
                acc_scr[rows, :] = o

        def rows_of(n):
            nb = n_chunks - 1 - n
            return slice(n * c, (n + 1) * c), slice(nb * c, (nb + 1) * c)

        def prep(n):
            rf, rb = rows_of(n)
            qkf, qkb = qkf_ref[0, rf, :], qkb_ref[0, rb, :]
            return (_gla_prep(qkf[:, :kw], qkf[:, kw:], laf_ref[0, rf, :], False, True),
                    _gla_prep(qkb[:, :kw], qkb[:, kw:], lab_ref[0, rb, :], True, True))

        sf, sb = sf_scr[...], sb_scr[...]
        ready = {n: prep(n) for n in range(min(GLA_PREP_AHEAD, n_chunks))}
        for n in range(n_chunks):
            rf, rb = rows_of(n)
            pf, pb = ready.pop(n)
            of, sf = _gla_apply(pf, vf_ref[0, rf, :], sf)
            emit(of, gf_ref, rf, i * tile + rf.start)
            ob, sb = _gla_apply(pb, vb_ref[0, rb, :], sb)
            emit(ob, gb_ref, rb, (n_tiles - 1 - i) * tile + rb.start)
            if n + GLA_PREP_AHEAD < n_chunks:
                ready[n + GLA_PREP_AHEAD] = prep(n + GLA_PREP_AHEAD)
        sf_scr[...] = sf
        sb_scr[...] = sb

    @pl.when(i < n_tiles // 2)
    def _():
        sweep(False)

    @pl.when(i >= n_tiles // 2)
    def _():
        sweep(True)


def _gla(qk, v, la, g, qk_c, v_c, la_c, head_w):
    b, t, _ = qk.shape
    tile = GLA_TILE
    n_tiles = t // tile
    kw, vw = GLA_KEY_WIDTH, GLA_VAL_WIDTH
    fwd = lambda w, col: pl.BlockSpec((1, tile, w), lambda bi, i: (bi, i, col))
    bwd = lambda w, col: pl.BlockSpec((1, tile, w), lambda bi, i: (bi, n_tiles - 1 - i, col))
    ctx = lambda w: pl.BlockSpec((1, CTX_LEN, w), lambda bi, i: (bi, 0, 0))
    half = n_tiles // 2
    gate_fwd = pl.BlockSpec((1, tile, vw), lambda bi, i: (bi, jnp.maximum(i, half), 0))
    gate_bwd = pl.BlockSpec((1, tile, vw), lambda bi, i: (bi, n_tiles - 1 - jnp.maximum(i, half), 0))
    return pl.pallas_call(
        functools.partial(_gla_kernel, n_tiles=n_tiles, tile=tile),
        out_shape=jax.ShapeDtypeStruct((b, t, vw), BF16),
        grid=(b, n_tiles),
        in_specs=[fwd(2 * kw, 0), fwd(vw, 0), fwd(kw, 0), gate_fwd,
                  bwd(2 * kw, 0), bwd(vw, 0), bwd(kw, 1), gate_bwd,
                  pl.BlockSpec((1, CTX_LEN, kw), lambda bi, i: (bi, 0, 1)), ctx(vw), ctx(2 * kw),
                  pl.BlockSpec((1, vw), lambda bi, i: (0, 0))],
        out_specs=pl.BlockSpec((1, t, vw), lambda bi, i: (bi, 0, 0)),
        scratch_shapes=[pltpu.VMEM((GLA_DV, kw), F32), pltpu.VMEM((GLA_DV, kw), F32), pltpu.VMEM((t, vw), F32)],
        compiler_params=_cparams("parallel", "arbitrary"),
        name="gla",
    )(qk, v, la, g, qk, v, la, g, qk_c, v_c, la_c, head_w)


def kernel(x, c, ctx, c_ctx, ada_w, ada_b, norm_mix_w, norm_ffn_w, ffn_w_gate, ffn_w_up, ffn_w_down,
           ab_w_in, ab_w_out, ab_sgu_norm_w, ab_sgu_w, ab_sgu_b, ab_rel_bias,
           cd_w_in, cd_w_out, cd_decay_w_fwd, cd_decay_b_fwd, cd_decay_w_bwd, cd_decay_b_bwd, cd_head_norm_w,
           final_norm_w):
    bsz, t, d = x.shape
    n_ctx = ctx.shape[1]
    tpb = t // ROW_TILE
    x2 = x.reshape(bsz * t, d)
    ctx2 = ctx.reshape(bsz * n_ctx, d)
    row = lambda w: w.reshape(1, -1)

    cc = jnp.concatenate([c, c_ctx[None, :], jnp.zeros((16 - bsz - 1, d), F32)], axis=0)
    mod = _ada(cc, ada_w, ada_b)
    mod_x = [mod[i, :bsz].reshape(bsz, 6, d) for i in range(2)]
    mod_c = [mod[i, bsz:bsz + 1].reshape(1, 6, d) for i in range(2)]
    ffn = (ffn_w_gate.astype(BF16), ffn_w_up.astype(BF16), ffn_w_down.astype(BF16))

    w_in = ab_w_in[0].astype(BF16)
    sgu = (row(ab_sgu_norm_w[0]), ab_sgu_w[0].astype(BF16),
           jnp.broadcast_to(ab_sgu_b[0][:, :, None], (A_GROUPS, A_CHUNK, LANES)))
    a_l, q_l, k_l, v_l = _ab_in(x2, mod_x[0], tpb, ROW_TILE, row(norm_mix_w[0]), w_in, *sgu)
    a_c, q_c, k_c, v_c = _ab_in(ctx2, mod_c[0], None, ROW_TILE, row(norm_mix_w[0]), w_in, *sgu)
    seq = lambda z, n: z.reshape(bsz, n, z.shape[-1])
    b_l = _na(seq(q_l, t), seq(k_l, t), seq(v_l, t), seq(k_c, n_ctx), seq(v_c, n_ctx), _na_bias_table(ab_rel_bias[0]))
    b_c = _ctx_attn(seq(q_c, n_ctx), seq(k_c, n_ctx), seq(v_c, n_ctx))
    wo = ab_w_out[0].astype(BF16)
    x2 = _out_ffn(a_l, b_l.reshape(bsz * t, -1), x2, mod_x[0], t // FFN_ROW_TILE, FFN_ROW_TILE, wo,
                  row(norm_ffn_w[0]), 0, *ffn, row(final_norm_w), False)
    ctx2 = _out_ffn(a_c, b_c.reshape(bsz * n_ctx, -1), ctx2, mod_c[0], None, ROW_TILE, wo, row(norm_ffn_w[0]),
                    0, *ffn, row(final_norm_w), False)

    w_in = cd_w_in[0]
    w_main = w_in[:, :O_A].astype(BF16)
    w_a = jnp.pad(w_in[:, O_A:], ((0, 0), (0, LANES - 2 * GLA_LOW_RANK))).astype(BF16)
    w_dec = jnp.zeros((LANES, 2 * GLA_KEY_WIDTH), F32)
    w_dec = w_dec.at[:GLA_LOW_RANK, :GLA_KEY_WIDTH].set(cd_decay_w_fwd[0])
    w_dec = w_dec.at[GLA_LOW_RANK:2 * GLA_LOW_RANK, GLA_KEY_WIDTH:].set(cd_decay_w_bwd[0]).astype(BF16)
    b_dec = jnp.concatenate([cd_decay_b_fwd[0], cd_decay_b_bwd[0]]).reshape(1, -1)
    cos, sin = _rope_tables(t)
    f_l, qk_l, v_l, g_l, la_l = _cd_in(x2, mod_x[1], tpb, ROW_TILE, row(norm_mix_w[1]), w_main, w_a, w_dec, b_dec,
                                      cos, sin, True)
    _, qk_c, v_c, _, la_c = _cd_in(ctx2, mod_c[1], None, ROW_TILE, row(norm_mix_w[1]), w_main, w_a, w_dec, b_dec,
                                   cos, sin, False)
    fm = _fourier(f_l.reshape(bsz, -1, FN_WIDTH), _fft_tables(t))
    head_w = jnp.tile(cd_head_norm_w[0], GLA_HEADS).reshape(1, -1)
    go = _gla(seq(qk_l, t), seq(v_l, t), seq(la_l, t), seq(g_l, t),
              seq(qk_c, n_ctx), seq(v_c, n_ctx), seq(la_c, n_ctx), head_w)
    out = _out_ffn(fm.reshape(bsz * t, -1), go.reshape(bsz * t, -1), x2, mod_x[1], t // FFN_ROW_TILE, FFN_ROW_TILE,
                   cd_w_out[0].astype(BF16), row(norm_ffn_w[1]), 1, *ffn, row(final_norm_w), True)
    return out.reshape(bsz, t, d)
```

```python
import functools

import numpy as np
import jax
import jax.numpy as jnp
from jax import lax
from jax.experimental import pallas as pl
from jax.experimental.pallas import tpu as pltpu

F32 = jnp.float32
BF16 = jnp.bfloat16

D_MODEL = 1024
CTX_LEN = 256
GRID_W = 64
A_WIDTH = 512
A_GROUPS = 4
A_CHUNK = 128
NA_HEADS = 8
NA_HEAD_DIM = 64
NA_WIDTH = 512
NA_WIN_ROWS = 8
NA_WIN_COLS = 16
O_NA_Q = 2 * A_WIDTH
FN_WIDTH = 512
FN_GROUP_CH = 128
GLA_HEADS = 4
GLA_VAL_WIDTH = 512
GLA_KEY_WIDTH = 256
GLA_DK = 64
GLA_DV = 128
GLA_LOW_RANK = 16
GLA_GATE_TEMP = 16.0
GLA_CHUNK = 64
O_Q = FN_WIDTH
O_V = O_Q + 2 * GLA_KEY_WIDTH
O_G = O_V + GLA_VAL_WIDTH
O_A = O_G + GLA_VAL_WIDTH
FFN_HIDDEN = 2816
ROPE_BASE = 10000.0
RMS_EPS = 1e-6
NEG_INF = -1e30

LANES = 128
VMEM_LIMIT_BYTES = 56 * 2 ** 20

ROW_TILE = 1024
FFN_ROW_TILE = 1024
ROW_SUBTILE = 256
FFN_SPLIT = 2
GLA_TILE = 1024
NA_STAGE_LAG = 3
FFT_UNROLL = 32
FFT_PITCH = GRID_W + 8
FFT_PITCH2 = 2 * GRID_W + 8
GLA_PREP_AHEAD = 2


def _cparams(*sem):
    return pltpu.CompilerParams(dimension_semantics=sem, vmem_limit_bytes=VMEM_LIMIT_BYTES)


def _resident(shape):
    nd = len(shape)
    return pl.BlockSpec(shape, lambda *_: (0,) * nd, pipeline_mode=pl.Buffered(1))


def _dot(a, b):
    return jnp.dot(a, b, preferred_element_type=F32)


def _dot_nt(a, b):
    return lax.dot_general(a, b, (((1,), (1,)), ((), ())), preferred_element_type=F32)


def _dot_tn(a, b):
    return lax.dot_general(a, b, (((0,), (0,)), ((), ())), preferred_element_type=F32)


def _subtiles(n_rows):
    step = min(ROW_SUBTILE, n_rows)
    return [slice(r, r + step) for r in range(0, n_rows, step)]


def _rms(x, w):
    return x * lax.rsqrt(jnp.mean(x * x, axis=-1, keepdims=True) + RMS_EPS) * w


def _rms_mod(x, w, shift, scale):
    return x * lax.rsqrt(jnp.mean(x * x, axis=-1, keepdims=True) + RMS_EPS) * (w * (1.0 + scale)) + shift


def _ada_kernel(c_ref, w_ref, b_ref, o_ref):
    c = c_ref[...]
    s = c * jax.nn.sigmoid(c)
    o_ref[0] = _dot(s.astype(BF16), w_ref[0].astype(BF16)) + b_ref[0]


def _ada(cc, ada_w, ada_b):
    depth, d, n = ada_w.shape
    r = cc.shape[0]
    tn = 1536
    return pl.pallas_call(
        _ada_kernel,
        out_shape=jax.ShapeDtypeStruct((depth, r, n), F32),
        grid=(depth, n // tn),
        in_specs=[pl.BlockSpec((r, d), lambda i, j: (0, 0)),
                  pl.BlockSpec((1, d, tn), lambda i, j: (i, 0, j)),
                  pl.BlockSpec((1, 1, tn), lambda i, j: (i, 0, j))],
        out_specs=pl.BlockSpec((1, r, tn), lambda i, j: (i, 0, j)),
        compiler_params=_cparams("arbitrary", "arbitrary"),
        name="ada_mod",
    )(cc, ada_w, ada_b.reshape(depth, 1, n))


def _mod_spec(tiles_per_batch):
    if tiles_per_batch is None:
        return pl.BlockSpec((1, 6, D_MODEL), lambda i: (0, 0, 0))
    return pl.BlockSpec((1, 6, D_MODEL), lambda i: (i // tiles_per_batch, 0, 0))


def _ab_in_kernel(x_ref, mod_ref, nw_ref, w_ref, snw_ref, sguw_ref, sgub_ref,
                  a_ref, q_ref, k_ref, v_ref, *, tm):
    tiles = _subtiles(tm)
    hs = [_rms_mod(x_ref[rows, :], nw_ref[...], mod_ref[0, 0:1, :], mod_ref[0, 1:2, :]).astype(BF16)
          for rows in tiles]
    uvs = [_dot(h, w_ref[:, :O_NA_Q]) for h in hs]
    for rows, h, uv in zip(tiles, hs, uvs):
        qkv = _dot(h, w_ref[:, O_NA_Q:])
        q_ref[rows, :] = (qkv[:, :NA_WIDTH] * NA_HEAD_DIM ** -0.5).astype(BF16)
        k_ref[rows, :] = qkv[:, NA_WIDTH:2 * NA_WIDTH].astype(BF16)
        v_ref[rows, :] = qkv[:, 2 * NA_WIDTH:].astype(BF16)
        uv = jax.nn.gelu(uv)
        u = uv[:, :A_WIDTH]
        v = _rms(uv[:, A_WIDTH:], snw_ref[...]).astype(BF16)
        for ci in range(uv.shape[0] // A_CHUNK):
            rs = slice(ci * A_CHUNK, (ci + 1) * A_CHUNK)
            out_rows = slice(rows.start + rs.start, rows.start + rs.stop)
            for g in range(A_GROUPS):
                cs = slice(g * LANES, (g + 1) * LANES)
                gate = _dot(sguw_ref[g], v[rs, cs]) + sgub_ref[g]
                a_ref[out_rows, cs] = (u[rs, cs] * gate).astype(BF16)


def _ab_in(x2, mod, tiles_per_batch, tm, nw, w_bf, snw, sguw_bf, sgub_exp):
    m = x2.shape[0]
    n_in = w_bf.shape[1]
    row = lambda w: pl.BlockSpec((tm, w), lambda i: (i, 0))
    out = jax.ShapeDtypeStruct((m, A_WIDTH), BF16)
    return pl.pallas_call(
        functools.partial(_ab_in_kernel, tm=tm),
        out_shape=(out, out, out, out),
        grid=(m // tm,),
        in_specs=[row(D_MODEL), _mod_spec(tiles_per_batch), _resident((1, D_MODEL)),
                  _resident((D_MODEL, n_in)), _resident((1, A_WIDTH)),
                  _resident((A_GROUPS, A_CHUNK, A_CHUNK)), _resident((A_GROUPS, A_CHUNK, LANES))],
        out_specs=(row(A_WIDTH),) * 4,
        compiler_params=_cparams("parallel"),
        name="ab_in",
    )(x2, mod, nw, w_bf, snw, sguw_bf, sgub_exp)


def _na_bias_table(rel_bias):
    w, nrow = GRID_W, 2 * NA_WIN_ROWS - 1
    col = np.arange(w)
    col_start = np.clip(col - NA_WIN_COLS // 2, 0, w - NA_WIN_COLS)
    col_mask = (col[None, :] >= col_start[:, None]) & (col[None, :] < col_start[:, None] + NA_WIN_COLS)
    lo = w - NA_WIN_COLS
    r_ext = jnp.pad(rel_bias, ((0, 0), (0, 0), (lo, 2 * w - lo - (2 * NA_WIN_COLS - 1))))
    flat = jnp.tile(r_ext, (1, 1, w))[:, :, :w * (2 * w - 1)]
    toe = flat.reshape(NA_HEADS, nrow, w, 2 * w - 1)[:, :, :, w - 1:]
    toe = jnp.where(col_mask, toe, NEG_INF).reshape(NA_HEADS // 2, 2, nrow, w, w)
    strips = [jnp.concatenate([toe[:, :, j - o + NA_WIN_ROWS - 1] for j in range(NA_WIN_ROWS)], axis=-1)
              for o in range(NA_WIN_ROWS)]
    return jnp.stack([s.reshape(NA_HEADS // 2, 2 * w, NA_WIN_ROWS * w) for s in strips], axis=1)


def _stack_heads(qb):
    first = lax.broadcasted_iota(jnp.int32, (1, LANES), 1) < NA_HEAD_DIM
    zero = jnp.zeros_like(qb)
    return jnp.concatenate([jnp.where(first, qb, zero), jnp.where(first, zero, qb)], axis=0)


def _unstack_heads(r):
    n = r.shape[0] // 2
    first = lax.broadcasted_iota(jnp.int32, (1, LANES), 1) < NA_HEAD_DIM
    return jnp.where(first, r[:n], r[n:])


def _na_kernel(q_ref, k_ref, v_ref, kc_ref, vc_ref, tab_ref, o_ref, kt_scr, kct_scr, *, rows):
    t = rows * GRID_W
    kt_scr[0] = k_ref[0].T
    kt_scr[1, :, :t - GRID_W] = k_ref[0, GRID_W:, :].T
    kct_scr[...] = kc_ref[0].T
    kct = kct_scr[...]
    vc = vc_ref[0]
    win = NA_WIN_ROWS * GRID_W

    def window(r):
        return min(max(r - NA_WIN_ROWS // 2, 0), rows - NA_WIN_ROWS)

    def scores(r):
        start = window(r)
        odd = start % 2
        kt0 = (start - odd) * GRID_W
        qs = _stack_heads(q_ref[0, r * GRID_W:(r + 1) * GRID_W, :])
        return _dot(qs, kt_scr[odd, :, kt0:kt0 + win]) + tab_ref[0, r - start], _dot(qs, kct)

    def softmax(s):
        s_nb, s_cx = s
        m = jnp.maximum(jnp.max(s_nb, axis=-1, keepdims=True), jnp.max(s_cx, axis=-1, keepdims=True))
        return jnp.exp(s_nb - m).astype(BF16), jnp.exp(s_cx - m).astype(BF16)

    ones_nb = jnp.ones((win, LANES), BF16)
    vc_aug = jnp.concatenate([vc, jnp.ones((vc.shape[0], LANES), BF16)], axis=1)

    def values(r, p):
        e_nb, e_cx = p
        k0 = window(r) * GRID_W
        vb_aug = jnp.concatenate([v_ref[0, k0:k0 + win, :], ones_nb], axis=1)
        acc = _dot(e_nb, vb_aug) + _dot(e_cx, vc_aug)
        o_ref[0, r * GRID_W:(r + 1) * GRID_W, :] = _unstack_heads(acc[:, :LANES] / acc[:, LANES:]).astype(BF16)

    lag = NA_STAGE_LAG
    s_vals, p_vals = {}, {}
    for r in range(rows + 2 * lag):
        if 0 <= r - 2 * lag < rows:
            values(r - 2 * lag, p_vals.pop(r - 2 * lag))
        if 0 <= r - lag < rows:
            p_vals[r - lag] = softmax(s_vals.pop(r - lag))
        if r < rows:
            s_vals[r] = scores(r)


def _na(q, k, v, kc, vc, table):
    b, t, _ = q.shape
    rows = t // GRID_W
    lat = pl.BlockSpec((1, t, LANES), lambda j, i: (i, 0, j))
    ctx = pl.BlockSpec((1, CTX_LEN, LANES), lambda j, i: (i, 0, j))
    return pl.pallas_call(
        functools.partial(_na_kernel, rows=rows),
        out_shape=jax.ShapeDtypeStruct((b, t, NA_WIDTH), BF16),
        grid=(NA_HEADS // 2, b),
        in_specs=[lat, lat, lat, ctx, ctx,
                  pl.BlockSpec((1, NA_WIN_ROWS, 2 * GRID_W, NA_WIN_ROWS * GRID_W), lambda j, i: (j, 0, 0, 0))],
        out_specs=lat,
        scratch_shapes=[pltpu.VMEM((2, LANES, t), BF16), pltpu.VMEM((LANES, CTX_LEN), BF16)],
        compiler_params=_cparams("parallel", "parallel"),
        name="na_attn",
    )(q, k, v, kc, vc, table)


def _ctx_attn_kernel(q_ref, k_ref, v_ref, o_ref):
    for j in range(NA_HEADS // 2):
        cols = slice(j * LANES, (j + 1) * LANES)
        s = _dot_nt(_stack_heads(q_ref[0, :, cols]), k_ref[0, :, cols])
        e = jnp.exp(s - jnp.max(s, axis=-1, keepdims=True))
        acc = _dot(e.astype(BF16), v_ref[0, :, cols])
        o_ref[0, :, cols] = _unstack_heads(acc / jnp.sum(e, axis=-1, keepdims=True)).astype(BF16)


def _ctx_attn(q, k, v):
    b = q.shape[0]
    spec = pl.BlockSpec((1, CTX_LEN, NA_WIDTH), lambda i: (i, 0, 0))
    return pl.pallas_call(
        _ctx_attn_kernel,
        out_shape=jax.ShapeDtypeStruct((b, CTX_LEN, NA_WIDTH), BF16),
        grid=(b,),
        in_specs=[spec, spec, spec],
        out_specs=spec,
        compiler_params=_cparams("parallel"),
        name="ctx_attn",
    )(q, k, v)


def _out_ffn_kernel(a_ref, b_ref, x_ref, mod_ref, wo_ref, nw_ref, wg_ref, wu_ref, wd_ref, fnw_ref, o_ref,
                    *, final):
    half = wo_ref.shape[0] // 2
    piece = FFN_HIDDEN // FFN_SPLIT
    tiles = _subtiles(x_ref.shape[0])
    ys = [_dot(a_ref[rows, :], wo_ref[:half, :]) + _dot(b_ref[rows, :], wo_ref[half:, :]) for rows in tiles]
    x1s = [x_ref[rows, :] + mod_ref[0, 2:3, :] * y for rows, y in zip(tiles, ys)]
    hs = [_rms_mod(x1, nw_ref[...], mod_ref[0, 3:4, :], mod_ref[0, 4:5, :]).astype(BF16) for x1 in x1s]
    accs = [None] * len(tiles)
    for j in range(FFN_SPLIT):
        cs = slice(j * piece, (j + 1) * piece)
        gs = [_dot(h, wg_ref[:, cs]) for h in hs]
        acts = [(g * jax.nn.sigmoid(g) * _dot(h, wu_ref[:, cs])).astype(BF16) for g, h in zip(gs, hs)]
        parts = [_dot(act, wd_ref[cs, :]) for act in acts]
        accs = [part if acc is None else acc + part for acc, part in zip(accs, parts)]
    for rows, x1, acc in zip(tiles, x1s, accs):
        x2 = x1 + mod_ref[0, 5:6, :] * acc
        o_ref[rows, :] = _rms(x2, fnw_ref[...]) if final else x2


def _out_ffn(a, b, x2, mod, tiles_per_batch, tm, wo_bf, nw, layer, wg_bf, wu_bf, wd_bf, fnw, final):
    m = x2.shape[0]
    row = lambda w: pl.BlockSpec((tm, w), lambda i: (i, 0))
    layer_w = lambda w: pl.BlockSpec((None,) + w.shape[1:], lambda i: (layer, 0, 0), pipeline_mode=pl.Buffered(1))
    return pl.pallas_call(
        functools.partial(_out_ffn_kernel, final=final),
        out_shape=jax.ShapeDtypeStruct((m, D_MODEL), F32),
        grid=(m // tm,),
        in_specs=[row(a.shape[1]), row(b.shape[1]), row(D_MODEL), _mod_spec(tiles_per_batch),
                  _resident(wo_bf.shape), _resident((1, D_MODEL)), layer_w(wg_bf),
                  layer_w(wu_bf), layer_w(wd_bf), _resident((1, D_MODEL))],
        out_specs=row(D_MODEL),
        compiler_params=_cparams("parallel"),
        name="out_ffn_final" if final else "out_ffn",
    )(a, b, x2, mod, wo_bf, nw, wg_bf, wu_bf, wd_bf, fnw)


def _rope_tables(t):
    half = GLA_DK // 4
    inv_freq = ROPE_BASE ** (-np.arange(half, dtype=np.float64) / half)
    pos = np.arange(t)
    ang_r = (pos // GRID_W)[:, None] * inv_freq[None, :]
    ang_c = (pos % GRID_W)[:, None] * inv_freq[None, :]
    cos = np.concatenate([np.cos(ang_r)] * 2 + [np.cos(ang_c)] * 2, axis=1)
    sin = np.concatenate([-np.sin(ang_r), np.sin(ang_r), -np.sin(ang_c), np.sin(ang_c)], axis=1)
    return (jnp.asarray(np.tile(cos, (1, 2)), F32), jnp.asarray(np.tile(sin, (1, 2)), F32))


def _log_sigmoid(z):
    return jnp.minimum(z, 0.0) - jnp.log(1.0 + jnp.exp(-jnp.abs(z)))


def _decay_logs(h, wa_ref, wdec_ref, bdec_ref):
    a = _dot(h, wa_ref[...]).astype(BF16)
    return _log_sigmoid(_dot(a, wdec_ref[...]) + bdec_ref[...]) * (1.0 / GLA_GATE_TEMP)


def _cd_in_kernel(x_ref, mod_ref, nw_ref, w_ref, wa_ref, wdec_ref, bdec_ref, cos_ref, sin_ref,
                  f_ref, qk_ref, v_ref, g_ref, la_ref):
    h = _rms_mod(x_ref[...], nw_ref[...], mod_ref[0, 0:1, :], mod_ref[0, 1:2, :]).astype(BF16)
    la_ref[...] = _decay_logs(h, wa_ref, wdec_ref, bdec_ref)
    f = _dot(h, w_ref[:, :O_Q])
    for a in range(f.shape[0] // GRID_W):
        f_ref[a * FFT_PITCH:a * FFT_PITCH + GRID_W, :] = f[a * GRID_W:(a + 1) * GRID_W]
        f_ref[a * FFT_PITCH + GRID_W:(a + 1) * FFT_PITCH, :] = jnp.zeros((FFT_PITCH - GRID_W, f.shape[1]), F32)
    qk = _dot(h, w_ref[:, O_Q:O_V])
    w = qk.shape[1]
    cos = jnp.concatenate([cos_ref[...]] * (w // LANES), axis=1)
    sin = jnp.concatenate([sin_ref[...]] * (w // LANES), axis=1)
    q16 = GLA_DK // 4
    first = (lax.broadcasted_iota(jnp.int32, (1, w), 1) % (2 * q16)) < q16
    partner = jnp.where(first, pltpu.roll(qk, w - q16, 1), pltpu.roll(qk, q16, 1))
    qk = qk * cos + partner * sin
    is_q = lax.broadcasted_iota(jnp.int32, (1, w), 1) < GLA_KEY_WIDTH
    qk_ref[...] = jnp.where(is_q, qk * GLA_DK ** -0.5, qk)
    v_ref[...] = _dot(h, w_ref[:, O_V:O_G]).astype(BF16)
    g_ref[...] = _dot(h, w_ref[:, O_G:O_A])


def _cd_in(x2, mod, tiles_per_batch, tm, nw, w_bf, wa_bf, wdec_bf, bdec, cos, sin):
    m = x2.shape[0]
    row = lambda w: pl.BlockSpec((tm, w), lambda i: (i, 0))
    tab = pl.BlockSpec((tm, LANES), lambda i: (i % tiles_per_batch, 0))
    w512 = 2 * GLA_KEY_WIDTH
    pad = lambda n: n // GRID_W * FFT_PITCH
    outs = (jax.ShapeDtypeStruct((pad(m), FN_WIDTH), F32), jax.ShapeDtypeStruct((m, w512), F32),
            jax.ShapeDtypeStruct((m, GLA_VAL_WIDTH), BF16), jax.ShapeDtypeStruct((m, GLA_VAL_WIDTH), F32),
            jax.ShapeDtypeStruct((m, w512), F32))
    return pl.pallas_call(
        _cd_in_kernel,
        out_shape=outs,
        grid=(m // tm,),
        in_specs=[row(D_MODEL), _mod_spec(tiles_per_batch), _resident((1, D_MODEL)),
                  _resident(w_bf.shape), _resident(wa_bf.shape), _resident(wdec_bf.shape),
                  _resident((1, w512)), tab, tab],
        out_specs=(pl.BlockSpec((pad(tm), FN_WIDTH), lambda i: (i, 0)),
                   row(w512), row(GLA_VAL_WIDTH), row(GLA_VAL_WIDTH), row(w512)),
        compiler_params=_cparams("parallel"),
        name="cd_in",
    )(x2, mod, nw, w_bf, wa_bf, wdec_bf, bdec, cos, sin)


def _cd_ctx_kernel(x_ref, mod_ref, nw_ref, w_ref, wa_ref, wdec_ref, bdec_ref, k_ref, v_ref, la_ref):
    h = _rms_mod(x_ref[...], nw_ref[...], mod_ref[0, 0:1, :], mod_ref[0, 1:2, :]).astype(BF16)
    la_ref[...] = _decay_logs(h, wa_ref, wdec_ref, bdec_ref)
    k_ref[...] = _dot(h, w_ref[:, O_Q + GLA_KEY_WIDTH:O_V])
    v_ref[...] = _dot(h, w_ref[:, O_V:O_G]).astype(BF16)


def _cd_ctx(x2, mod, tm, nw, w_bf, wa_bf, wdec_bf, bdec):
    m = x2.shape[0]
    row = lambda w: pl.BlockSpec((tm, w), lambda i: (i, 0))
    w512 = 2 * GLA_KEY_WIDTH
    outs = (jax.ShapeDtypeStruct((m, GLA_KEY_WIDTH), F32), jax.ShapeDtypeStruct((m, GLA_VAL_WIDTH), BF16),
            jax.ShapeDtypeStruct((m, w512), F32))
    return pl.pallas_call(
        _cd_ctx_kernel,
        out_shape=outs,
        grid=(m // tm,),
        in_specs=[row(D_MODEL), _mod_spec(None), _resident((1, D_MODEL)),
                  _resident(w_bf.shape), _resident(wa_bf.shape), _resident(wdec_bf.shape), _resident((1, w512))],
        out_specs=(row(GLA_KEY_WIDTH), row(GLA_VAL_WIDTH), row(w512)),
        compiler_params=_cparams("parallel"),
        name="cd_ctx",
    )(x2, mod, nw, w_bf, wa_bf, wdec_bf, bdec)


def _fft_tables(t):
    n1 = GRID_W
    assert t == n1 * n1
    a = np.arange(n1)
    ang1 = 2 * np.pi * np.outer(a, a) / n1
    cs = np.concatenate([np.cos(ang1), -np.sin(ang1)], axis=0)
    kap = a[:, None, None] + n1 * a[None, :, None]
    ang2 = 2 * np.pi * kap * a[None, None, :] / t
    gr, gi = np.cos(ang2), -np.sin(ang2)
    g = np.concatenate([np.concatenate([gr, -gi], axis=2), np.concatenate([gi, gr], axis=2)], axis=1)
    c = np.arange(FN_GROUP_CH)
    ang3 = 2 * np.pi * np.outer(c, c) / FN_GROUP_CH
    norm = 1.0 / np.sqrt(t * FN_GROUP_CH)
    return (jnp.asarray(cs, F32), jnp.asarray(g, F32),
            jnp.asarray(np.cos(ang3) * norm, F32), jnp.asarray(np.sin(ang3) * norm, F32))


def _fft_kernel(xa_ref, xb_ref, cs_ref, g_ref, cc_ref, sc_ref, o_ref, p_scr, zr_scr, zi_scr):
    n1 = GRID_W
    x_refs = (xa_ref, xb_ref)
    groups = range(len(x_refs))
    lanes = lambda gi: slice(gi * FN_GROUP_CH, (gi + 1) * FN_GROUP_CH)
    cs = cs_ref[...].astype(BF16)

    def stage1(b, carry):
        xb = jnp.concatenate([r[0, pl.ds(b, n1, stride=FFT_PITCH), :] for r in x_refs], axis=1)
        p = _dot(cs, xb.astype(BF16))
        rows = pl.ds(pl.multiple_of(b * FFT_PITCH2, 8), 2 * n1)
        for gi in groups:
            p_scr[gi, rows, :] = p[:, lanes(gi)]
        return carry

    lax.fori_loop(0, n1, stage1, 0, unroll=FFT_UNROLL)

    def stage2(k1, carry):
        pr = jnp.concatenate([p_scr[gi, pl.ds(k1, n1, stride=FFT_PITCH2), :] for gi in groups], axis=1)
        pi = jnp.concatenate([p_scr[gi, pl.ds(n1 + k1, n1, stride=FFT_PITCH2), :] for gi in groups], axis=1)
        z = _dot(g_ref[k1].astype(BF16), jnp.concatenate([pr, pi], axis=0).astype(BF16))
        rows = pl.ds(pl.multiple_of(k1 * FFT_PITCH, 8), n1)
        for gi in groups:
            zr_scr[gi, rows, :] = z[:n1, lanes(gi)]
            zi_scr[gi, rows, :] = z[n1:, lanes(gi)]
        return carry

    lax.fori_loop(0, n1, stage2, 0, unroll=FFT_UNROLL)

    ccsc = jnp.concatenate([cc_ref[...], sc_ref[...]], axis=0).astype(BF16)

    def stage3(j, carry):
        out_rows = pl.ds(pl.multiple_of(j * FFT_UNROLL * n1, FFT_UNROLL * n1), FFT_UNROLL * n1)
        for gi in groups:
            parts = []
            for u in range(FFT_UNROLL):
                k2 = j * FFT_UNROLL + u
                parts.append(jnp.concatenate([zr_scr[gi, pl.ds(k2, n1, stride=FFT_PITCH), :],
                                              zi_scr[gi, pl.ds(k2, n1, stride=FFT_PITCH), :]], axis=1))
            out = _dot(jnp.concatenate(parts, axis=0).astype(BF16), ccsc)
            o_ref[0, out_rows, lanes(gi)] = out.astype(BF16)
        return carry

    lax.fori_loop(0, n1 // FFT_UNROLL, stage3, 0)


def _fourier(f_padded, tables):
    b, tp, w = f_padded.shape
    t = tp // FFT_PITCH * GRID_W
    cw = FN_GROUP_CH
    cs, g, cc, sc = tables
    group = lambda k: pl.BlockSpec((1, tp, cw), lambda i, j: (i, 0, 2 * j + k))
    return pl.pallas_call(
        _fft_kernel,
        out_shape=jax.ShapeDtypeStruct((b, t, w), BF16),
        grid=(b, w // (2 * cw)),
        in_specs=[group(0), group(1),
                  _resident(cs.shape), _resident(g.shape), _resident(cc.shape), _resident(sc.shape)],
        out_specs=pl.BlockSpec((1, t, 2 * cw), lambda i, j: (i, 0, j)),
        scratch_shapes=[pltpu.VMEM((2, GRID_W * FFT_PITCH2, cw), F32), pltpu.VMEM((2, tp, cw), F32),
                        pltpu.VMEM((2, tp, cw), F32)],
        compiler_params=_cparams("parallel", "parallel"),
        name="fourier_mix",
    )(f_padded, f_padded, cs, g, cc, sc)


def _block_diag(x, block):
    head = lax.broadcasted_iota(jnp.int32, (1, x.shape[1]), 1) // block
    return jnp.concatenate([jnp.where(head == h, x, jnp.zeros_like(x)) for h in range(GLA_HEADS)], axis=0)


def _cum_rows(x, reverse):
    n, w = x.shape
    row = lax.broadcasted_iota(jnp.int32, (n, 1), 0)
    s = 1
    while s < n:
        if s < 8:
            if reverse:
                shifted = jnp.where(row < n - s, pltpu.roll(x, n - s, 0), 0.0)
            else:
                shifted = jnp.where(row >= s, pltpu.roll(x, s, 0), 0.0)
        else:
            pad = jnp.zeros((s, w), F32)
            shifted = jnp.concatenate([x[s:], pad] if reverse else [pad, x[:n - s]], axis=0)
        x = x + shifted
        s *= 2
    return x


def _gla_prep(q, k, la, reverse, want_out):
    c = GLA_CHUNK
    b = _cum_rows(la, reverse)
    b_last = b[0:1, :] if reverse else b[c - 1:c, :]
    kd = (k * jnp.exp(b_last - b)).astype(BF16)
    decay = jnp.exp(b_last)
    if not want_out:
        return kd, decay
    qe = (q * jnp.exp(b)).astype(BF16)
    ke = _block_diag(k * jnp.exp(-b), GLA_DK).astype(BF16)
    l = lax.broadcasted_iota(jnp.int32, (c, GLA_HEADS * c), 0)
    m = lax.broadcasted_iota(jnp.int32, (c, GLA_HEADS * c), 1) % c
    att = jnp.where((l <= m) if reverse else (l >= m), _dot_nt(qe, ke), 0.0).astype(BF16)
    return kd, decay, qe, att


def _gla_state_step(kd, decay, v, state):
    v_rows = jnp.concatenate([v[:, h * GLA_DV:(h + 1) * GLA_DV] for h in range(GLA_HEADS)], axis=0)
    return state * decay + _dot_tn(v_rows, _block_diag(kd, GLA_DK))


def _gla_apply(prep, v, state):
    c = GLA_CHUNK
    kd, decay, qe, att = prep
    inter = _dot_nt(_block_diag(qe, GLA_DK), state.astype(BF16))
    out = _dot(att, _block_diag(v, GLA_DV)) + jnp.concatenate(
        [inter[h * c:(h + 1) * c] for h in range(GLA_HEADS)], axis=1)
    return out, _gla_state_step(kd, decay, v, state)


def _gla_kernel(qkf_ref, vf_ref, laf_ref, gf_ref, qkb_ref, vb_ref, lab_ref, gb_ref,
                kc_ref, vc_ref, lac_ref, hw_ref, o_ref, sf_scr, sb_scr, acc_scr, *, n_tiles, tile):
    c = GLA_CHUNK
    i = pl.program_id(1)
    kw = GLA_KEY_WIDTH
    n_chunks = tile // c

    @pl.when(i == 0)
    def _():
        n_ctx = kc_ref.shape[1] // c
        sf = jnp.zeros((GLA_DV, kw), F32)
        sb = jnp.zeros((GLA_DV, kw), F32)
        for n in range(n_ctx):
            rf = slice(n * c, (n + 1) * c)
            rb = slice((n_ctx - 1 - n) * c, (n_ctx - n) * c)
            sf = _gla_state_step(*_gla_prep(None, kc_ref[0, rf, :], lac_ref[0, rf, :kw], False, False),
                                 vc_ref[0, rf, :], sf)
            sb = _gla_state_step(*_gla_prep(None, kc_ref[0, rb, :], lac_ref[0, rb, kw:], True, False),
                                 vc_ref[0, rb, :], sb)
        sf_scr[...] = sf
        sb_scr[...] = sb

    hw = hw_ref[...]

    def finish(o, g):
        parts = []
        for h in range(GLA_HEADS):
            oh = o[:, h * GLA_DV:(h + 1) * GLA_DV]
            parts.append(oh * lax.rsqrt(jnp.mean(oh * oh, axis=-1, keepdims=True) + RMS_EPS))
        return (jnp.concatenate(parts, axis=1) * hw * (g * jax.nn.sigmoid(g))).astype(BF16)

    def sweep(second_pass):
        def emit(o, g_ref, rloc, row0):
            rows = pl.ds(pl.multiple_of(row0, c), c)
            if second_pass:
                o_ref[0, rows, :] = finish(o + acc_scr[rows, :], g_ref[0, rloc, :])
            else:
                acc_scr[rows, :] = o

        def rows_of(n):
            nb = n_chunks - 1 - n
            return slice(n * c, (n + 1) * c), slice(nb * c, (nb + 1) * c)

        def prep(n):
            rf, rb = rows_of(n)
            qkf, qkb = qkf_ref[0, rf, :], qkb_ref[0, rb, :]
            return (_gla_prep(qkf[:, :kw], qkf[:, kw:], laf_ref[0, rf, :], False, True),
                    _gla_prep(qkb[:, :kw], qkb[:, kw:], lab_ref[0, rb, :], True, True))

        sf, sb = sf_scr[...], sb_scr[...]
        ready = {n: prep(n) for n in range(min(GLA_PREP_AHEAD, n_chunks))}
        for n in range(n_chunks):
            rf, rb = rows_of(n)
            pf, pb = ready.pop(n)
            of, sf = _gla_apply(pf, vf_ref[0, rf, :], sf)
            emit(of, gf_ref, rf, i * tile + rf.start)
            ob, sb = _gla_apply(pb, vb_ref[0, rb, :], sb)
            emit(ob, gb_ref, rb, (n_tiles - 1 - i) * tile + rb.start)
            if n + GLA_PREP_AHEAD < n_chunks:
                ready[n + GLA_PREP_AHEAD] = prep(n + GLA_PREP_AHEAD)
        sf_scr[...] = sf
        sb_scr[...] = sb

    @pl.when(i < n_tiles // 2)
    def _():
        sweep(False)

    @pl.when(i >= n_tiles // 2)
    def _():
        sweep(True)


def _gla(qk, v, la, g, k_c, v_c, la_c, head_w):
    b, t, _ = qk.shape
    tile = GLA_TILE
    n_tiles = t // tile
    kw, vw = GLA_KEY_WIDTH, GLA_VAL_WIDTH
    fwd = lambda w, col: pl.BlockSpec((1, tile, w), lambda bi, i: (bi, i, col))
    bwd = lambda w, col: pl.BlockSpec((1, tile, w), lambda bi, i: (bi, n_tiles - 1 - i, col))
    ctx = lambda w: pl.BlockSpec((1, CTX_LEN, w), lambda bi, i: (bi, 0, 0))
    half = n_tiles // 2
    gate_fwd = pl.BlockSpec((1, tile, vw), lambda bi, i: (bi, jnp.maximum(i, half), 0))
    gate_bwd = pl.BlockSpec((1, tile, vw), lambda bi, i: (bi, n_tiles - 1 - jnp.maximum(i, half), 0))
    return pl.pallas_call(
        functools.partial(_gla_kernel, n_tiles=n_tiles, tile=tile),
        out_shape=jax.ShapeDtypeStruct((b, t, vw), BF16),
        grid=(b, n_tiles),
        in_specs=[fwd(2 * kw, 0), fwd(vw, 0), fwd(kw, 0), gate_fwd,
                  bwd(2 * kw, 0), bwd(vw, 0), bwd(kw, 1), gate_bwd,
                  ctx(kw), ctx(vw), ctx(2 * kw),
                  pl.BlockSpec((1, vw), lambda bi, i: (0, 0))],
        out_specs=pl.BlockSpec((1, t, vw), lambda bi, i: (bi, 0, 0)),
        scratch_shapes=[pltpu.VMEM((GLA_DV, kw), F32), pltpu.VMEM((GLA_DV, kw), F32), pltpu.VMEM((t, vw), F32)],
        compiler_params=_cparams("parallel", "arbitrary"),
        name="gla",
    )(qk, v, la, g, qk, v, la, g, k_c, v_c, la_c, head_w)


def kernel(x, c, ctx, c_ctx, ada_w, ada_b, norm_mix_w, norm_ffn_w, ffn_w_gate, ffn_w_up, ffn_w_down,
           ab_w_in, ab_w_out, ab_sgu_norm_w, ab_sgu_w, ab_sgu_b, ab_rel_bias,
           cd_w_in, cd_w_out, cd_decay_w_fwd, cd_decay_b_fwd, cd_decay_w_bwd, cd_decay_b_bwd, cd_head_norm_w,
           final_norm_w):
    bsz, t, d = x.shape
    n_ctx = ctx.shape[1]
    tpb = t // ROW_TILE
    x2 = x.reshape(bsz * t, d)
    ctx2 = ctx.reshape(bsz * n_ctx, d)
    row = lambda w: w.reshape(1, -1)

    cc = jnp.concatenate([c, c_ctx[None, :], jnp.zeros((16 - bsz - 1, d), F32)], axis=0)
    mod = _ada(cc, ada_w, ada_b)
    mod_x = [mod[i, :bsz].reshape(bsz, 6, d) for i in range(2)]
    mod_c = [mod[i, bsz:bsz + 1].reshape(1, 6, d) for i in range(2)]
    ffn = (ffn_w_gate.astype(BF16), ffn_w_up.astype(BF16), ffn_w_down.astype(BF16))

    w_in = ab_w_in[0].astype(BF16)
    sgu = (row(ab_sgu_norm_w[0]), ab_sgu_w[0].astype(BF16),
           jnp.broadcast_to(ab_sgu_b[0][:, :, None], (A_GROUPS, A_CHUNK, LANES)))
    a_l, q_l, k_l, v_l = _ab_in(x2, mod_x[0], tpb, ROW_TILE, row(norm_mix_w[0]), w_in, *sgu)
    a_c, q_c, k_c, v_c = _ab_in(ctx2, mod_c[0], None, ROW_TILE, row(norm_mix_w[0]), w_in, *sgu)
    seq = lambda z, n: z.reshape(bsz, n, z.shape[-1])
    b_l = _na(seq(q_l, t), seq(k_l, t), seq(v_l, t), seq(k_c, n_ctx), seq(v_c, n_ctx), _na_bias_table(ab_rel_bias[0]))
    b_c = _ctx_attn(seq(q_c, n_ctx), seq(k_c, n_ctx), seq(v_c, n_ctx))
    wo = ab_w_out[0].astype(BF16)
    x2 = _out_ffn(a_l, b_l.reshape(bsz * t, -1), x2, mod_x[0], t // FFN_ROW_TILE, FFN_ROW_TILE, wo,
                  row(norm_ffn_w[0]), 0, *ffn, row(final_norm_w), False)
    ctx2 = _out_ffn(a_c, b_c.reshape(bsz * n_ctx, -1), ctx2, mod_c[0], None, ROW_TILE, wo, row(norm_ffn_w[0]),
                    0, *ffn, row(final_norm_w), False)

    w_in = cd_w_in[0]
    w_main = w_in[:, :O_A].astype(BF16)
    w_a = jnp.pad(w_in[:, O_A:], ((0, 0), (0, LANES - 2 * GLA_LOW_RANK))).astype(BF16)
    w_dec = jnp.zeros((LANES, 2 * GLA_KEY_WIDTH), F32)
    w_dec = w_dec.at[:GLA_LOW_RANK, :GLA_KEY_WIDTH].set(cd_decay_w_fwd[0])
    w_dec = w_dec.at[GLA_LOW_RANK:2 * GLA_LOW_RANK, GLA_KEY_WIDTH:].set(cd_decay_w_bwd[0]).astype(BF16)
    b_dec = jnp.concatenate([cd_decay_b_fwd[0], cd_decay_b_bwd[0]]).reshape(1, -1)
    cos, sin = _rope_tables(t)
    f_l, qk_l, v_l, g_l, la_l = _cd_in(x2, mod_x[1], tpb, ROW_TILE, row(norm_mix_w[1]), w_main, w_a, w_dec, b_dec,
                                      cos, sin)
    k_c, v_c, la_c = _cd_ctx(ctx2, mod_c[1], ROW_TILE, row(norm_mix_w[1]), w_main, w_a, w_dec, b_dec)
    fm = _fourier(f_l.reshape(bsz, -1, FN_WIDTH), _fft_tables(t))
    head_w = jnp.tile(cd_head_norm_w[0], GLA_HEADS).reshape(1, -1)
    go = _gla(seq(qk_l, t), seq(v_l, t), seq(la_l, t), seq(g_l, t),
              seq(k_c, n_ctx), seq(v_c, n_ctx), seq(la_c, n_ctx), head_w)
    out = _out_ffn(fm.reshape(bsz * t, -1), go.reshape(bsz * t, -1), x2, mod_x[1], t // FFN_ROW_TILE, FFN_ROW_TILE,
                   cd_w_out[0].astype(BF16), row(norm_ffn_w[1]), 1, *ffn, row(final_norm_w), True)
    return out.reshape(bsz, t, d)
```

```python
import functools

import numpy as np
import jax
import jax.numpy as jnp
from jax import lax
from jax.experimental import pallas as pl
from jax.experimental.pallas import tpu as pltpu

F32 = jnp.float32
BF16 = jnp.bfloat16

D_MODEL = 1024
CTX_LEN = 256
GRID_W = 64
A_WIDTH = 512
A_GROUPS = 4
A_CHUNK = 128
NA_HEADS = 8
NA_HEAD_DIM = 64
NA_WIDTH = 512
NA_WIN_ROWS = 8
NA_WIN_COLS = 16
O_NA_Q = 2 * A_WIDTH
FN_WIDTH = 512
FN_GROUP_CH = 128
GLA_HEADS = 4
GLA_VAL_WIDTH = 512
GLA_KEY_WIDTH = 256
GLA_DK = 64
GLA_DV = 128
GLA_LOW_RANK = 16
GLA_GATE_TEMP = 16.0
GLA_CHUNK = 64
O_Q = FN_WIDTH
O_V = O_Q + 2 * GLA_KEY_WIDTH
O_G = O_V + GLA_VAL_WIDTH
O_A = O_G + GLA_VAL_WIDTH
FFN_HIDDEN = 2816
ROPE_BASE = 10000.0
RMS_EPS = 1e-6
NEG_INF = -1e30

LANES = 128
MXU_TILE = 256
VMEM_LIMIT_BYTES = 56 * 2 ** 20

MOD_ROWS = 6
ADA_ROWS = 16
ADA_COL_TILE = 1536
ROW_TILE = 1024
FFN_ROW_TILE = 1024
ROW_SUBTILE = 256
FFN_SPLIT = 2
GLA_TILE = 1024
NA_STAGE_LAG = 3
FFT_UNROLL = 32
FFT_PITCH = GRID_W + 8
FFT_PITCH2 = 2 * GRID_W + 8
GLA_PREP_AHEAD = 2


def _cparams(*sem):
    return pltpu.CompilerParams(dimension_semantics=sem, vmem_limit_bytes=VMEM_LIMIT_BYTES)


def _resident(shape):
    nd = len(shape)
    return pl.BlockSpec(shape, lambda *_: (0,) * nd, pipeline_mode=pl.Buffered(1))


def _dot(a, b):
    return jnp.dot(a, b, preferred_element_type=F32)


def _dot_nt(a, b):
    return lax.dot_general(a, b, (((1,), (1,)), ((), ())), preferred_element_type=F32)


def _dot_tn(a, b):
    return lax.dot_general(a, b, (((0,), (0,)), ((), ())), preferred_element_type=F32)


def _subtiles(n_rows):
    step = min(ROW_SUBTILE, n_rows)
    return [slice(r, r + step) for r in range(0, n_rows, step)]


def _hidden_pieces(n):
    n_tiles = n // MXU_TILE
    assert n_tiles * MXU_TILE == n
    bounds = [round(j * n_tiles / FFN_SPLIT) * MXU_TILE for j in range(FFN_SPLIT + 1)]
    return [slice(lo, hi) for lo, hi in zip(bounds[:-1], bounds[1:])]


def _rms(x, w):
    return x * lax.rsqrt(jnp.mean(x * x, axis=-1, keepdims=True) + RMS_EPS) * w


def _rms_mod(x, w, shift, scale):
    return x * lax.rsqrt(jnp.mean(x * x, axis=-1, keepdims=True) + RMS_EPS) * (w * (1.0 + scale)) + shift


def _ada_kernel(c_ref, w_ref, b_ref, o_ref):
    c = c_ref[...]
    s = c * jax.nn.sigmoid(c)
    o_ref[0] = _dot(s.astype(BF16), w_ref[0].astype(BF16)) + b_ref[0]


def _ada(cc, ada_w, ada_b):
    depth, d, n = ada_w.shape
    r = cc.shape[0]
    tn = ADA_COL_TILE
    return pl.pallas_call(
        _ada_kernel,
        out_shape=jax.ShapeDtypeStruct((depth, r, n), F32),
        grid=(depth, n // tn),
        in_specs=[pl.BlockSpec((r, d), lambda i, j: (0, 0)),
                  pl.BlockSpec((1, d, tn), lambda i, j: (i, 0, j)),
                  pl.BlockSpec((1, 1, tn), lambda i, j: (i, 0, j))],
        out_specs=pl.BlockSpec((1, r, tn), lambda i, j: (i, 0, j)),
        compiler_params=_cparams("arbitrary", "arbitrary"),
        name="ada_mod",
    )(cc, ada_w, ada_b.reshape(depth, 1, n))


def _mod_spec(tiles_per_batch):
    if tiles_per_batch is None:
        return pl.BlockSpec((1, MOD_ROWS, D_MODEL), lambda i: (0, 0, 0))
    return pl.BlockSpec((1, MOD_ROWS, D_MODEL), lambda i: (i // tiles_per_batch, 0, 0))


def _ab_in_kernel(x_ref, mod_ref, nw_ref, w_ref, snw_ref, sguw_ref, sgub_ref,
                  a_ref, q_ref, k_ref, v_ref, *, tm):
    tiles = _subtiles(tm)
    hs = [_rms_mod(x_ref[rows, :], nw_ref[...], mod_ref[0, 0:1, :], mod_ref[0, 1:2, :]).astype(BF16)
          for rows in tiles]
    uvs = [_dot(h, w_ref[:, :O_NA_Q]) for h in hs]
    for rows, h, uv in zip(tiles, hs, uvs):
        qkv = _dot(h, w_ref[:, O_NA_Q:])
        q_ref[rows, :] = (qkv[:, :NA_WIDTH] * NA_HEAD_DIM ** -0.5).astype(BF16)
        k_ref[rows, :] = qkv[:, NA_WIDTH:2 * NA_WIDTH].astype(BF16)
        v_ref[rows, :] = qkv[:, 2 * NA_WIDTH:].astype(BF16)
        uv = jax.nn.gelu(uv)
        u = uv[:, :A_WIDTH]
        v = _rms(uv[:, A_WIDTH:], snw_ref[...]).astype(BF16)
        for ci in range(uv.shape[0] // A_CHUNK):
            rs = slice(ci * A_CHUNK, (ci + 1) * A_CHUNK)
            out_rows = slice(rows.start + rs.start, rows.start + rs.stop)
            for g in range(A_GROUPS):
                cs = slice(g * LANES, (g + 1) * LANES)
                gate = _dot(sguw_ref[g], v[rs, cs]) + sgub_ref[g]
                a_ref[out_rows, cs] = (u[rs, cs] * gate).astype(BF16)


def _ab_in(x2, mod, tiles_per_batch, tm, nw, w_bf, snw, sguw_bf, sgub_exp):
    m = x2.shape[0]
    n_in = w_bf.shape[1]
    row = lambda w: pl.BlockSpec((tm, w), lambda i: (i, 0))
    out = jax.ShapeDtypeStruct((m, A_WIDTH), BF16)
    return pl.pallas_call(
        functools.partial(_ab_in_kernel, tm=tm),
        out_shape=(out, out, out, out),
        grid=(m // tm,),
        in_specs=[row(D_MODEL), _mod_spec(tiles_per_batch), _resident((1, D_MODEL)),
                  _resident((D_MODEL, n_in)), _resident((1, A_WIDTH)),
                  _resident((A_GROUPS, A_CHUNK, A_CHUNK)), _resident((A_GROUPS, A_CHUNK, LANES))],
        out_specs=(row(A_WIDTH),) * 4,
        compiler_params=_cparams("parallel"),
        name="ab_in",
    )(x2, mod, nw, w_bf, snw, sguw_bf, sgub_exp)


def _na_bias_table(rel_bias):
    w, nrow = GRID_W, 2 * NA_WIN_ROWS - 1
    col = np.arange(w)
    col_start = np.clip(col - NA_WIN_COLS // 2, 0, w - NA_WIN_COLS)
    col_mask = (col[None, :] >= col_start[:, None]) & (col[None, :] < col_start[:, None] + NA_WIN_COLS)
    lo = w - NA_WIN_COLS
    r_ext = jnp.pad(rel_bias, ((0, 0), (0, 0), (lo, 2 * w - lo - (2 * NA_WIN_COLS - 1))))
    flat = jnp.tile(r_ext, (1, 1, w))[:, :, :w * (2 * w - 1)]
    toe = flat.reshape(NA_HEADS, nrow, w, 2 * w - 1)[:, :, :, w - 1:]
    toe = jnp.where(col_mask, toe, NEG_INF).reshape(NA_HEADS // 2, 2, nrow, w, w)
    strips = [jnp.concatenate([toe[:, :, j - o + NA_WIN_ROWS - 1] for j in range(NA_WIN_ROWS)], axis=-1)
              for o in range(NA_WIN_ROWS)]
    return jnp.stack([s.reshape(NA_HEADS // 2, 2 * w, NA_WIN_ROWS * w) for s in strips], axis=1)


def _stack_heads(qb):
    first = lax.broadcasted_iota(jnp.int32, (1, LANES), 1) < NA_HEAD_DIM
    zero = jnp.zeros_like(qb)
    return jnp.concatenate([jnp.where(first, qb, zero), jnp.where(first, zero, qb)], axis=0)


def _unstack_heads(r):
    n = r.shape[0] // 2
    first = lax.broadcasted_iota(jnp.int32, (1, LANES), 1) < NA_HEAD_DIM
    return jnp.where(first, r[:n], r[n:])


def _na_kernel(q_ref, k_ref, v_ref, kc_ref, vc_ref, tab_ref, o_ref, kt_scr, kct_scr, *, rows):
    t = rows * GRID_W
    kt_scr[0] = k_ref[0].T
    kt_scr[1, :, :t - GRID_W] = k_ref[0, GRID_W:, :].T
    kct_scr[...] = kc_ref[0].T
    kct = kct_scr[...]
    vc = vc_ref[0]
    win = NA_WIN_ROWS * GRID_W

    def window(r):
        return min(max(r - NA_WIN_ROWS // 2, 0), rows - NA_WIN_ROWS)

    def scores(r):
        start = window(r)
        odd = start % 2
        kt0 = (start - odd) * GRID_W
        qs = _stack_heads(q_ref[0, r * GRID_W:(r + 1) * GRID_W, :])
        return _dot(qs, kt_scr[odd, :, kt0:kt0 + win]) + tab_ref[0, r - start], _dot(qs, kct)

    def softmax(s):
        s_nb, s_cx = s
        m = jnp.maximum(jnp.max(s_nb, axis=-1, keepdims=True), jnp.max(s_cx, axis=-1, keepdims=True))
        return jnp.exp(s_nb - m).astype(BF16), jnp.exp(s_cx - m).astype(BF16)

    ones_nb = jnp.ones((win, LANES), BF16)
    vc_aug = jnp.concatenate([vc, jnp.ones((vc.shape[0], LANES), BF16)], axis=1)

    def values(r, p):
        e_nb, e_cx = p
        k0 = window(r) * GRID_W
        vb_aug = jnp.concatenate([v_ref[0, k0:k0 + win, :], ones_nb], axis=1)
        acc = _dot(e_nb, vb_aug) + _dot(e_cx, vc_aug)
        o_ref[0, r * GRID_W:(r + 1) * GRID_W, :] = _unstack_heads(acc[:, :LANES] / acc[:, LANES:]).astype(BF16)

    lag = NA_STAGE_LAG
    s_vals, p_vals = {}, {}
    for r in range(rows + 2 * lag):
        if 0 <= r - 2 * lag < rows:
            values(r - 2 * lag, p_vals.pop(r - 2 * lag))
        if 0 <= r - lag < rows:
            p_vals[r - lag] = softmax(s_vals.pop(r - lag))
        if r < rows:
            s_vals[r] = scores(r)


def _na(q, k, v, kc, vc, table):
    b, t, _ = q.shape
    rows = t // GRID_W
    lat = pl.BlockSpec((1, t, LANES), lambda j, i: (i, 0, j))
    ctx = pl.BlockSpec((1, CTX_LEN, LANES), lambda j, i: (i, 0, j))
    return pl.pallas_call(
        functools.partial(_na_kernel, rows=rows),
        out_shape=jax.ShapeDtypeStruct((b, t, NA_WIDTH), BF16),
        grid=(NA_HEADS // 2, b),
        in_specs=[lat, lat, lat, ctx, ctx,
                  pl.BlockSpec((1, NA_WIN_ROWS, 2 * GRID_W, NA_WIN_ROWS * GRID_W), lambda j, i: (j, 0, 0, 0))],
        out_specs=lat,
        scratch_shapes=[pltpu.VMEM((2, LANES, t), BF16), pltpu.VMEM((LANES, CTX_LEN), BF16)],
        compiler_params=_cparams("parallel", "parallel"),
        name="na_attn",
    )(q, k, v, kc, vc, table)


def _ctx_attn_kernel(q_ref, k_ref, v_ref, o_ref):
    for j in range(NA_HEADS // 2):
        cols = slice(j * LANES, (j + 1) * LANES)
        s = _dot_nt(_stack_heads(q_ref[0, :, cols]), k_ref[0, :, cols])
        e = jnp.exp(s - jnp.max(s, axis=-1, keepdims=True))
        acc = _dot(e.astype(BF16), v_ref[0, :, cols])
        o_ref[0, :, cols] = _unstack_heads(acc / jnp.sum(e, axis=-1, keepdims=True)).astype(BF16)


def _ctx_attn(q, k, v):
    b = q.shape[0]
    spec = pl.BlockSpec((1, CTX_LEN, NA_WIDTH), lambda i: (i, 0, 0))
    return pl.pallas_call(
        _ctx_attn_kernel,
        out_shape=jax.ShapeDtypeStruct((b, CTX_LEN, NA_WIDTH), BF16),
        grid=(b,),
        in_specs=[spec, spec, spec],
        out_specs=spec,
        compiler_params=_cparams("parallel"),
        name="ctx_attn",
    )(q, k, v)


def _out_ffn_kernel(a_ref, b_ref, x_ref, mod_ref, wo_ref, nw_ref, wg_ref, wu_ref, wd_ref, fnw_ref, o_ref,
                    *, final):
    half = wo_ref.shape[0] // 2
    tiles = _subtiles(x_ref.shape[0])
    ys = [_dot(a_ref[rows, :], wo_ref[:half, :]) + _dot(b_ref[rows, :], wo_ref[half:, :]) for rows in tiles]
    x1s = [x_ref[rows, :] + mod_ref[0, 2:3, :] * y for rows, y in zip(tiles, ys)]
    hs = [_rms_mod(x1, nw_ref[...], mod_ref[0, 3:4, :], mod_ref[0, 4:5, :]).astype(BF16) for x1 in x1s]
    accs = [None] * len(tiles)
    for cs in _hidden_pieces(wg_ref.shape[1]):
        gs = [_dot(h, wg_ref[:, cs]) for h in hs]
        acts = [(g * jax.nn.sigmoid(g) * _dot(h, wu_ref[:, cs])).astype(BF16) for g, h in zip(gs, hs)]
        parts = [_dot(act, wd_ref[cs, :]) for act in acts]
        accs = [part if acc is None else acc + part for acc, part in zip(accs, parts)]
    for rows, x1, acc in zip(tiles, x1s, accs):
        x2 = x1 + mod_ref[0, 5:6, :] * acc
        o_ref[rows, :] = _rms(x2, fnw_ref[...]) if final else x2


def _out_ffn(a, b, x2, mod, tiles_per_batch, tm, wo_bf, nw, layer, wg_bf, wu_bf, wd_bf, fnw, final):
    m = x2.shape[0]
    row = lambda w: pl.BlockSpec((tm, w), lambda i: (i, 0))
    layer_w = lambda w: pl.BlockSpec((None,) + w.shape[1:], lambda i: (layer, 0, 0), pipeline_mode=pl.Buffered(1))
    return pl.pallas_call(
        functools.partial(_out_ffn_kernel, final=final),
        out_shape=jax.ShapeDtypeStruct((m, D_MODEL), F32),
        grid=(m // tm,),
        in_specs=[row(a.shape[1]), row(b.shape[1]), row(D_MODEL), _mod_spec(tiles_per_batch),
                  _resident(wo_bf.shape), _resident((1, D_MODEL)), layer_w(wg_bf),
                  layer_w(wu_bf), layer_w(wd_bf), _resident((1, D_MODEL))],
        out_specs=row(D_MODEL),
        compiler_params=_cparams("parallel"),
        name="out_ffn_final" if final else "out_ffn",
    )(a, b, x2, mod, wo_bf, nw, wg_bf, wu_bf, wd_bf, fnw)


def _rope_tables(t):
    half = GLA_DK // 4
    inv_freq = ROPE_BASE ** (-np.arange(half, dtype=np.float64) / half)
    pos = np.arange(t)
    ang_r = (pos // GRID_W)[:, None] * inv_freq[None, :]
    ang_c = (pos % GRID_W)[:, None] * inv_freq[None, :]
    cos = np.concatenate([np.cos(ang_r)] * 2 + [np.cos(ang_c)] * 2, axis=1)
    sin = np.concatenate([-np.sin(ang_r), np.sin(ang_r), -np.sin(ang_c), np.sin(ang_c)], axis=1)
    return (jnp.asarray(np.tile(cos, (1, 2)), F32), jnp.asarray(np.tile(sin, (1, 2)), F32))


def _log_sigmoid(z):
    return jnp.minimum(z, 0.0) - jnp.log(1.0 + jnp.exp(-jnp.abs(z)))


def _decay_logs(h, wa_ref, wdec_ref, bdec_ref):
    a = _dot(h, wa_ref[...]).astype(BF16)
    return _log_sigmoid(_dot(a, wdec_ref[...]) + bdec_ref[...]) * (1.0 / GLA_GATE_TEMP)


def _cd_in_kernel(x_ref, mod_ref, nw_ref, w_ref, wa_ref, wdec_ref, bdec_ref, cos_ref, sin_ref,
                  f_ref, qk_ref, v_ref, g_ref, la_ref):
    h = _rms_mod(x_ref[...], nw_ref[...], mod_ref[0, 0:1, :], mod_ref[0, 1:2, :]).astype(BF16)
    la_ref[...] = _decay_logs(h, wa_ref, wdec_ref, bdec_ref)
    f = _dot(h, w_ref[:, :O_Q])
    for a in range(f.shape[0] // GRID_W):
        f_ref[a * FFT_PITCH:a * FFT_PITCH + GRID_W, :] = f[a * GRID_W:(a + 1) * GRID_W]
        f_ref[a * FFT_PITCH + GRID_W:(a + 1) * FFT_PITCH, :] = jnp.zeros((FFT_PITCH - GRID_W, f.shape[1]), F32)
    qk = _dot(h, w_ref[:, O_Q:O_V])
    w = qk.shape[1]
    cos = jnp.concatenate([cos_ref[...]] * (w // LANES), axis=1)
    sin = jnp.concatenate([sin_ref[...]] * (w // LANES), axis=1)
    q16 = GLA_DK // 4
    first = (lax.broadcasted_iota(jnp.int32, (1, w), 1) % (2 * q16)) < q16
    partner = jnp.where(first, pltpu.roll(qk, w - q16, 1), pltpu.roll(qk, q16, 1))
    qk = qk * cos + partner * sin
    is_q = lax.broadcasted_iota(jnp.int32, (1, w), 1) < GLA_KEY_WIDTH
    qk_ref[...] = jnp.where(is_q, qk * GLA_DK ** -0.5, qk)
    v_ref[...] = _dot(h, w_ref[:, O_V:O_G]).astype(BF16)
    g_ref[...] = _dot(h, w_ref[:, O_G:O_A])


def _cd_in(x2, mod, tiles_per_batch, tm, nw, w_bf, wa_bf, wdec_bf, bdec, cos, sin):
    m = x2.shape[0]
    row = lambda w: pl.BlockSpec((tm, w), lambda i: (i, 0))
    tab = pl.BlockSpec((tm, LANES), lambda i: (i % tiles_per_batch, 0))
    w512 = 2 * GLA_KEY_WIDTH
    pad = lambda n: n // GRID_W * FFT_PITCH
    outs = (jax.ShapeDtypeStruct((pad(m), FN_WIDTH), F32), jax.ShapeDtypeStruct((m, w512), F32),
            jax.ShapeDtypeStruct((m, GLA_VAL_WIDTH), BF16), jax.ShapeDtypeStruct((m, GLA_VAL_WIDTH), F32),
            jax.ShapeDtypeStruct((m, w512), F32))
    return pl.pallas_call(
        _cd_in_kernel,
        out_shape=outs,
        grid=(m // tm,),
        in_specs=[row(D_MODEL), _mod_spec(tiles_per_batch), _resident((1, D_MODEL)),
                  _resident(w_bf.shape), _resident(wa_bf.shape), _resident(wdec_bf.shape),
                  _resident((1, w512)), tab, tab],
        out_specs=(pl.BlockSpec((pad(tm), FN_WIDTH), lambda i: (i, 0)),
                   row(w512), row(GLA_VAL_WIDTH), row(GLA_VAL_WIDTH), row(w512)),
        compiler_params=_cparams("parallel"),
        name="cd_in",
    )(x2, mod, nw, w_bf, wa_bf, wdec_bf, bdec, cos, sin)


def _cd_ctx_kernel(x_ref, mod_ref, nw_ref, w_ref, wa_ref, wdec_ref, bdec_ref, k_ref, v_ref, la_ref):
    h = _rms_mod(x_ref[...], nw_ref[...], mod_ref[0, 0:1, :], mod_ref[0, 1:2, :]).astype(BF16)
    la_ref[...] = _decay_logs(h, wa_ref, wdec_ref, bdec_ref)
    k_ref[...] = _dot(h, w_ref[:, O_Q + GLA_KEY_WIDTH:O_V])
    v_ref[...] = _dot(h, w_ref[:, O_V:O_G]).astype(BF16)


def _cd_ctx(x2, mod, tm, nw, w_bf, wa_bf, wdec_bf, bdec):
    m = x2.shape[0]
    row = lambda w: pl.BlockSpec((tm, w), lambda i: (i, 0))
    w512 = 2 * GLA_KEY_WIDTH
    outs = (jax.ShapeDtypeStruct((m, GLA_KEY_WIDTH), F32), jax.ShapeDtypeStruct((m, GLA_VAL_WIDTH), BF16),
            jax.ShapeDtypeStruct((m, w512), F32))
    return pl.pallas_call(
        _cd_ctx_kernel,
        out_shape=outs,
        grid=(m // tm,),
        in_specs=[row(D_MODEL), _mod_spec(None), _resident((1, D_MODEL)),
                  _resident(w_bf.shape), _resident(wa_bf.shape), _resident(wdec_bf.shape), _resident((1, w512))],
        out_specs=(row(GLA_KEY_WIDTH), row(GLA_VAL_WIDTH), row(w512)),
        compiler_params=_cparams("parallel"),
        name="cd_ctx",
    )(x2, mod, nw, w_bf, wa_bf, wdec_bf, bdec)


def _fft_tables(t):
    n1 = GRID_W
    assert t == n1 * n1
    a = np.arange(n1)
    ang1 = 2 * np.pi * np.outer(a, a) / n1
    cs = np.concatenate([np.cos(ang1), -np.sin(ang1)], axis=0)
    kap = a[:, None, None] + n1 * a[None, :, None]
    ang2 = 2 * np.pi * kap * a[None, None, :] / t
    gr, gi = np.cos(ang2), -np.sin(ang2)
    g = np.concatenate([np.concatenate([gr, -gi], axis=2), np.concatenate([gi, gr], axis=2)], axis=1)
    c = np.arange(FN_GROUP_CH)
    ang3 = 2 * np.pi * np.outer(c, c) / FN_GROUP_CH
    norm = 1.0 / np.sqrt(t * FN_GROUP_CH)
    return (jnp.asarray(cs, F32), jnp.asarray(g, F32),
            jnp.asarray(np.cos(ang3) * norm, F32), jnp.asarray(np.sin(ang3) * norm, F32))


def _fft_kernel(xa_ref, xb_ref, cs_ref, g_ref, cc_ref, sc_ref, o_ref, p_scr, zr_scr, zi_scr):
    n1 = GRID_W
    x_refs = (xa_ref, xb_ref)
    groups = range(len(x_refs))
    lanes = lambda gi: slice(gi * FN_GROUP_CH, (gi + 1) * FN_GROUP_CH)
    cs = cs_ref[...].astype(BF16)

    def stage1(b, carry):
        xb = jnp.concatenate([r[0, pl.ds(b, n1, stride=FFT_PITCH), :] for r in x_refs], axis=1)
        p = _dot(cs, xb.astype(BF16))
        rows = pl.ds(pl.multiple_of(b * FFT_PITCH2, 8), 2 * n1)
        for gi in groups:
            p_scr[gi, rows, :] = p[:, lanes(gi)]
        return carry

    lax.fori_loop(0, n1, stage1, 0, unroll=FFT_UNROLL)

    def stage2(k1, carry):
        pr = jnp.concatenate([p_scr[gi, pl.ds(k1, n1, stride=FFT_PITCH2), :] for gi in groups], axis=1)
        pi = jnp.concatenate([p_scr[gi, pl.ds(n1 + k1, n1, stride=FFT_PITCH2), :] for gi in groups], axis=1)
        z = _dot(g_ref[k1].astype(BF16), jnp.concatenate([pr, pi], axis=0).astype(BF16))
        rows = pl.ds(pl.multiple_of(k1 * FFT_PITCH, 8), n1)
        for gi in groups:
            zr_scr[gi, rows, :] = z[:n1, lanes(gi)]
            zi_scr[gi, rows, :] = z[n1:, lanes(gi)]
        return carry

    lax.fori_loop(0, n1, stage2, 0, unroll=FFT_UNROLL)

    ccsc = jnp.concatenate([cc_ref[...], sc_ref[...]], axis=0).astype(BF16)

    def stage3(j, carry):
        out_rows = pl.ds(pl.multiple_of(j * FFT_UNROLL * n1, FFT_UNROLL * n1), FFT_UNROLL * n1)
        for gi in groups:
            parts = []
            for u in range(FFT_UNROLL):
                k2 = j * FFT_UNROLL + u
                parts.append(jnp.concatenate([zr_scr[gi, pl.ds(k2, n1, stride=FFT_PITCH), :],
                                              zi_scr[gi, pl.ds(k2, n1, stride=FFT_PITCH), :]], axis=1))
            out = _dot(jnp.concatenate(parts, axis=0).astype(BF16), ccsc)
            o_ref[0, out_rows, lanes(gi)] = out.astype(BF16)
        return carry

    lax.fori_loop(0, n1 // FFT_UNROLL, stage3, 0)


def _fourier(f_padded, tables):
    b, tp, w = f_padded.shape
    t = tp // FFT_PITCH * GRID_W
    cw = FN_GROUP_CH
    cs, g, cc, sc = tables
    group = lambda k: pl.BlockSpec((1, tp, cw), lambda i, j: (i, 0, 2 * j + k))
    return pl.pallas_call(
        _fft_kernel,
        out_shape=jax.ShapeDtypeStruct((b, t, w), BF16),
        grid=(b, w // (2 * cw)),
        in_specs=[group(0), group(1),
                  _resident(cs.shape), _resident(g.shape), _resident(cc.shape), _resident(sc.shape)],
        out_specs=pl.BlockSpec((1, t, 2 * cw), lambda i, j: (i, 0, j)),
        scratch_shapes=[pltpu.VMEM((2, GRID_W * FFT_PITCH2, cw), F32), pltpu.VMEM((2, tp, cw), F32),
                        pltpu.VMEM((2, tp, cw), F32)],
        compiler_params=_cparams("parallel", "parallel"),
        name="fourier_mix",
    )(f_padded, f_padded, cs, g, cc, sc)


def _block_diag(x, block):
    head = lax.broadcasted_iota(jnp.int32, (1, x.shape[1]), 1) // block
    return jnp.concatenate([jnp.where(head == h, x, jnp.zeros_like(x)) for h in range(GLA_HEADS)], axis=0)


def _cum_rows(x, reverse):
    n, w = x.shape
    row = lax.broadcasted_iota(jnp.int32, (n, 1), 0)
    s = 1
    while s < n:
        if s < 8:
            if reverse:
                shifted = jnp.where(row < n - s, pltpu.roll(x, n - s, 0), 0.0)
            else:
                shifted = jnp.where(row >= s, pltpu.roll(x, s, 0), 0.0)
        else:
            pad = jnp.zeros((s, w), F32)
            shifted = jnp.concatenate([x[s:], pad] if reverse else [pad, x[:n - s]], axis=0)
        x = x + shifted
        s *= 2
    return x


def _gla_prep(q, k, la, reverse, want_out):
    c = GLA_CHUNK
    b = _cum_rows(la, reverse)
    b_last = b[0:1, :] if reverse else b[c - 1:c, :]
    kd = (k * jnp.exp(b_last - b)).astype(BF16)
    decay = jnp.exp(b_last)
    if not want_out:
        return kd, decay
    qe = (q * jnp.exp(b)).astype(BF16)
    ke = _block_diag(k * jnp.exp(-b), GLA_DK).astype(BF16)
    l = lax.broadcasted_iota(jnp.int32, (c, GLA_HEADS * c), 0)
    m = lax.broadcasted_iota(jnp.int32, (c, GLA_HEADS * c), 1) % c
    att = jnp.where((l <= m) if reverse else (l >= m), _dot_nt(qe, ke), 0.0).astype(BF16)
    return kd, decay, qe, att


def _gla_state_step(kd, decay, v, state):
    v_rows = jnp.concatenate([v[:, h * GLA_DV:(h + 1) * GLA_DV] for h in range(GLA_HEADS)], axis=0)
    return state * decay + _dot_tn(v_rows, _block_diag(kd, GLA_DK))


def _gla_apply(prep, v, state):
    c = GLA_CHUNK
    kd, decay, qe, att = prep
    inter = _dot_nt(_block_diag(qe, GLA_DK), state.astype(BF16))
    out = _dot(att, _block_diag(v, GLA_DV)) + jnp.concatenate(
        [inter[h * c:(h + 1) * c] for h in range(GLA_HEADS)], axis=1)
    return out, _gla_state_step(kd, decay, v, state)


def _gla_kernel(qkf_ref, vf_ref, laf_ref, gf_ref, qkb_ref, vb_ref, lab_ref, gb_ref,
                kc_ref, vc_ref, lac_ref, hw_ref, o_ref, sf_scr, sb_scr, acc_scr, *, n_tiles, tile):
    c = GLA_CHUNK
    i = pl.program_id(1)
    kw = GLA_KEY_WIDTH
    n_chunks = tile // c

    @pl.when(i == 0)
    def _():
        n_ctx = kc_ref.shape[1] // c
        sf = jnp.zeros((GLA_DV, kw), F32)
        sb = jnp.zeros((GLA_DV, kw), F32)
        for n in range(n_ctx):
            rf = slice(n * c, (n + 1) * c)
            rb = slice((n_ctx - 1 - n) * c, (n_ctx - n) * c)
            sf = _gla_state_step(*_gla_prep(None, kc_ref[0, rf, :], lac_ref[0, rf, :kw], False, False),
                                 vc_ref[0, rf, :], sf)
            sb = _gla_state_step(*_gla_prep(None, kc_ref[0, rb, :], lac_ref[0, rb, kw:], True, False),
                                 vc_ref[0, rb, :], sb)
        sf_scr[...] = sf
        sb_scr[...] = sb

    hw = hw_ref[...]

    def finish(o, g):
        parts = []
        for h in range(GLA_HEADS):
            oh = o[:, h * GLA_DV:(h + 1) * GLA_DV]
            parts.append(oh * lax.rsqrt(jnp.mean(oh * oh, axis=-1, keepdims=True) + RMS_EPS))
        return (jnp.concatenate(parts, axis=1) * hw * (g * jax.nn.sigmoid(g))).astype(BF16)

    def sweep(second_pass):
        def emit(o, g_ref, rloc, row0):
            rows = pl.ds(pl.multiple_of(row0, c), c)
            if second_pass:
                o_ref[0, rows, :] = finish(o + acc_scr[rows, :], g_ref[0, rloc, :])
            else:
                acc_scr[rows, :] = o

        def rows_of(n):
            nb = n_chunks - 1 - n
            return slice(n * c, (n + 1) * c), slice(nb * c, (nb + 1) * c)

        def prep(n):
            rf, rb = rows_of(n)
            qkf, qkb = qkf_ref[0, rf, :], qkb_ref[0, rb, :]
            return (_gla_prep(qkf[:, :kw], qkf[:, kw:], laf_ref[0, rf, :], False, True),
                    _gla_prep(qkb[:, :kw], qkb[:, kw:], lab_ref[0, rb, :], True, True))

        sf, sb = sf_scr[...], sb_scr[...]
        ready = {n: prep(n) for n in range(min(GLA_PREP_AHEAD, n_chunks))}
        for n in range(n_chunks):
            rf, rb = rows_of(n)
            pf, pb = ready.pop(n)
            of, sf = _gla_apply(pf, vf_ref[0, rf, :], sf)
            emit(of, gf_ref, rf, i * tile + rf.start)
            ob, sb = _gla_apply(pb, vb_ref[0, rb, :], sb)
            emit(ob, gb_ref, rb, (n_tiles - 1 - i) * tile + rb.start)
            if n + GLA_PREP_AHEAD < n_chunks:
                ready[n + GLA_PREP_AHEAD] = prep(n + GLA_PREP_AHEAD)
        sf_scr[...] = sf
        sb_scr[...] = sb

    @pl.when(i < n_tiles // 2)
    def _():
        sweep(False)

    @pl.when(i >= n_tiles // 2)
    def _():
        sweep(True)


def _gla(qk, v, la, g, k_c, v_c, la_c, head_w):
    b, t, _ = qk.shape
    tile = GLA_TILE
    n_tiles = t // tile
    kw, vw = GLA_KEY_WIDTH, GLA_VAL_WIDTH
    fwd = lambda w, col: pl.BlockSpec((1, tile, w), lambda bi, i: (bi, i, col))
    bwd = lambda w, col: pl.BlockSpec((1, tile, w), lambda bi, i: (bi, n_tiles - 1 - i, col))
    ctx = lambda w: pl.BlockSpec((1, CTX_LEN, w), lambda bi, i: (bi, 0, 0))
    half = n_tiles // 2
    gate_fwd = pl.BlockSpec((1, tile, vw), lambda bi, i: (bi, jnp.maximum(i, half), 0))
    gate_bwd = pl.BlockSpec((1, tile, vw), lambda bi, i: (bi, n_tiles - 1 - jnp.maximum(i, half), 0))
    return pl.pallas_call(
        functools.partial(_gla_kernel, n_tiles=n_tiles, tile=tile),
        out_shape=jax.ShapeDtypeStruct((b, t, vw), BF16),
        grid=(b, n_tiles),
        in_specs=[fwd(2 * kw, 0), fwd(vw, 0), fwd(kw, 0), gate_fwd,
                  bwd(2 * kw, 0), bwd(vw, 0), bwd(kw, 1), gate_bwd,
                  ctx(kw), ctx(vw), ctx(2 * kw),
                  pl.BlockSpec((1, vw), lambda bi, i: (0, 0))],
        out_specs=pl.BlockSpec((1, t, vw), lambda bi, i: (bi, 0, 0)),
        scratch_shapes=[pltpu.VMEM((GLA_DV, kw), F32), pltpu.VMEM((GLA_DV, kw), F32), pltpu.VMEM((t, vw), F32)],
        compiler_params=_cparams("parallel", "arbitrary"),
        name="gla",
    )(qk, v, la, g, qk, v, la, g, k_c, v_c, la_c, head_w)


def kernel(x, c, ctx, c_ctx, ada_w, ada_b, norm_mix_w, norm_ffn_w, ffn_w_gate, ffn_w_up, ffn_w_down,
           ab_w_in, ab_w_out, ab_sgu_norm_w, ab_sgu_w, ab_sgu_b, ab_rel_bias,
           cd_w_in, cd_w_out, cd_decay_w_fwd, cd_decay_b_fwd, cd_decay_w_bwd, cd_decay_b_bwd, cd_head_norm_w,
           final_norm_w):
    bsz, t, d = x.shape
    n_ctx = ctx.shape[1]
    tpb = t // ROW_TILE
    x2 = x.reshape(bsz * t, d)
    ctx2 = ctx.reshape(bsz * n_ctx, d)
    row = lambda w: w.reshape(1, -1)

    cc = jnp.concatenate([c, c_ctx[None, :], jnp.zeros((ADA_ROWS - bsz - 1, d), F32)], axis=0)
    mod = _ada(cc, ada_w, ada_b)
    mod_x = [mod[i, :bsz].reshape(bsz, MOD_ROWS, d) for i in range(2)]
    mod_c = [mod[i, bsz:bsz + 1].reshape(1, MOD_ROWS, d) for i in range(2)]
    ffn = (ffn_w_gate.astype(BF16), ffn_w_up.astype(BF16), ffn_w_down.astype(BF16))

    w_in = ab_w_in[0].astype(BF16)
    sgu = (row(ab_sgu_norm_w[0]), ab_sgu_w[0].astype(BF16),
           jnp.broadcast_to(ab_sgu_b[0][:, :, None], (A_GROUPS, A_CHUNK, LANES)))
    a_l, q_l, k_l, v_l = _ab_in(x2, mod_x[0], tpb, ROW_TILE, row(norm_mix_w[0]), w_in, *sgu)
    a_c, q_c, k_c, v_c = _ab_in(ctx2, mod_c[0], None, ROW_TILE, row(norm_mix_w[0]), w_in, *sgu)
    seq = lambda z, n: z.reshape(bsz, n, z.shape[-1])
    b_l = _na(seq(q_l, t), seq(k_l, t), seq(v_l, t), seq(k_c, n_ctx), seq(v_c, n_ctx), _na_bias_table(ab_rel_bias[0]))
    b_c = _ctx_attn(seq(q_c, n_ctx), seq(k_c, n_ctx), seq(v_c, n_ctx))
    wo = ab_w_out[0].astype(BF16)
    x2 = _out_ffn(a_l, b_l.reshape(bsz * t, -1), x2, mod_x[0], t // FFN_ROW_TILE, FFN_ROW_TILE, wo,
                  row(norm_ffn_w[0]), 0, *ffn, row(final_norm_w), False)
    ctx2 = _out_ffn(a_c, b_c.reshape(bsz * n_ctx, -1), ctx2, mod_c[0], None, ROW_TILE, wo, row(norm_ffn_w[0]),
                    0, *ffn, row(final_norm_w), False)

    w_in = cd_w_in[0]
    w_main = w_in[:, :O_A].astype(BF16)
    w_a = jnp.pad(w_in[:, O_A:], ((0, 0), (0, LANES - 2 * GLA_LOW_RANK))).astype(BF16)
    w_dec = jnp.zeros((LANES, 2 * GLA_KEY_WIDTH), F32)
    w_dec = w_dec.at[:GLA_LOW_RANK, :GLA_KEY_WIDTH].set(cd_decay_w_fwd[0])
    w_dec = w_dec.at[GLA_LOW_RANK:2 * GLA_LOW_RANK, GLA_KEY_WIDTH:].set(cd_decay_w_bwd[0]).astype(BF16)
    b_dec = jnp.concatenate([cd_decay_b_fwd[0], cd_decay_b_bwd[0]]).reshape(1, -1)
    cos, sin = _rope_tables(t)
    f_l, qk_l, v_l, g_l, la_l = _cd_in(x2, mod_x[1], tpb, ROW_TILE, row(norm_mix_w[1]), w_main, w_a, w_dec, b_dec,
                                      cos, sin)
    k_c, v_c, la_c = _cd_ctx(ctx2, mod_c[1], ROW_TILE, row(norm_mix_w[1]), w_main, w_a, w_dec, b_dec)
    fm = _fourier(f_l.reshape(bsz, -1, FN_WIDTH), _fft_tables(t))
    head_w = jnp.tile(cd_head_norm_w[0], GLA_HEADS).reshape(1, -1)
    go = _gla(seq(qk_l, t), seq(v_l, t), seq(la_l, t), seq(g_l, t),
              seq(k_c, n_ctx), seq(v_c, n_ctx), seq(la_c, n_ctx), head_w)
    out = _out_ffn(fm.reshape(bsz * t, -1), go.reshape(bsz * t, -1), x2, mod_x[1], t // FFN_ROW_TILE, FFN_ROW_TILE,
                   cd_w_out[0].astype(BF16), row(norm_ffn_w[1]), 1, *ffn, row(final_norm_w), True)
    return out.reshape(bsz, t, d)
```

```python
import functools

import numpy as np
import jax
import jax.numpy as jnp
from jax import lax
from jax.experimental import pallas as pl
from jax.experimental.pallas import tpu as pltpu

F32 = jnp.float32
BF16 = jnp.bfloat16

D_MODEL = 1024
CTX_LEN = 256
GRID_W = 64
A_WIDTH = 512
A_GROUPS = 4
A_CHUNK = 128
NA_HEADS = 8
NA_HEAD_DIM = 64
NA_WIDTH = 512
NA_WIN_ROWS = 8
NA_WIN_COLS = 16
O_NA_Q = 2 * A_WIDTH
FN_WIDTH = 512
FN_GROUP_CH = 128
GLA_HEADS = 4
GLA_VAL_WIDTH = 512
GLA_KEY_WIDTH = 256
GLA_DK = 64
GLA_DV = 128
GLA_LOW_RANK = 16
GLA_GATE_TEMP = 16.0
GLA_CHUNK = 64
O_Q = FN_WIDTH
O_V = O_Q + 2 * GLA_KEY_WIDTH
O_G = O_V + GLA_VAL_WIDTH
O_A = O_G + GLA_VAL_WIDTH
FFN_HIDDEN = 2816
ROPE_BASE = 10000.0
RMS_EPS = 1e-6
NEG_INF = -1e30

LANES = 128
MXU_TILE = 256
VMEM_LIMIT_BYTES = 56 * 2 ** 20

MOD_ROWS = 6
ADA_ROWS = 16
ADA_COL_TILE = 1536
ROW_TILE = 1024
FFN_ROW_TILE = 1024
ROW_SUBTILE = 256
FFN_SPLIT = 2
GLA_TILE = 1024
NA_STAGE_LAG = 3
FFT_UNROLL = 64
FFT_PITCH = GRID_W + 8
FFT_PITCH2 = 2 * GRID_W + 8
GLA_PREP_AHEAD = 2


def _cparams(*sem):
    return pltpu.CompilerParams(dimension_semantics=sem, vmem_limit_bytes=VMEM_LIMIT_BYTES)


def _resident(shape):
    nd = len(shape)
    return pl.BlockSpec(shape, lambda *_: (0,) * nd, pipeline_mode=pl.Buffered(1))


def _dot(a, b):
    return jnp.dot(a, b, preferred_element_type=F32)


def _dot_nt(a, b):
    return lax.dot_general(a, b, (((1,), (1,)), ((), ())), preferred_element_type=F32)


def _dot_tn(a, b):
    return lax.dot_general(a, b, (((0,), (0,)), ((), ())), preferred_element_type=F32)


def _subtiles(n_rows):
    step = min(ROW_SUBTILE, n_rows)
    return [slice(r, r + step) for r in range(0, n_rows, step)]


def _hidden_pieces(n):
    n_tiles = n // MXU_TILE
    assert n_tiles * MXU_TILE == n
    bounds = [round(j * n_tiles / FFN_SPLIT) * MXU_TILE for j in range(FFN_SPLIT + 1)]
    return [slice(lo, hi) for lo, hi in zip(bounds[:-1], bounds[1:])]


def _rms(x, w):
    return x * lax.rsqrt(jnp.mean(x * x, axis=-1, keepdims=True) + RMS_EPS) * w


def _rms_mod(x, w, shift, scale):
    return x * lax.rsqrt(jnp.mean(x * x, axis=-1, keepdims=True) + RMS_EPS) * (w * (1.0 + scale)) + shift


def _ada_kernel(c_ref, w_ref, b_ref, o_ref):
    c = c_ref[...]
    s = c * jax.nn.sigmoid(c)
    o_ref[0] = _dot(s.astype(BF16), w_ref[0].astype(BF16)) + b_ref[0]


def _ada(cc, ada_w, ada_b):
    depth, d, n = ada_w.shape
    r = cc.shape[0]
    tn = ADA_COL_TILE
    return pl.pallas_call(
        _ada_kernel,
        out_shape=jax.ShapeDtypeStruct((depth, r, n), F32),
        grid=(depth, n // tn),
        in_specs=[pl.BlockSpec((r, d), lambda i, j: (0, 0)),
                  pl.BlockSpec((1, d, tn), lambda i, j: (i, 0, j)),
                  pl.BlockSpec((1, 1, tn), lambda i, j: (i, 0, j))],
        out_specs=pl.BlockSpec((1, r, tn), lambda i, j: (i, 0, j)),
        compiler_params=_cparams("arbitrary", "arbitrary"),
        name="ada_mod",
    )(cc, ada_w, ada_b.reshape(depth, 1, n))


def _mod_spec(tiles_per_batch):
    if tiles_per_batch is None:
        return pl.BlockSpec((1, MOD_ROWS, D_MODEL), lambda i: (0, 0, 0))
    return pl.BlockSpec((1, MOD_ROWS, D_MODEL), lambda i: (i // tiles_per_batch, 0, 0))


def _ab_in_kernel(x_ref, mod_ref, nw_ref, w_ref, snw_ref, sguw_ref, sgub_ref,
                  a_ref, q_ref, k_ref, v_ref, *, tm):
    tiles = _subtiles(tm)
    hs = [_rms_mod(x_ref[rows, :], nw_ref[...], mod_ref[0, 0:1, :], mod_ref[0, 1:2, :]).astype(BF16)
          for rows in tiles]
    uvs = [_dot(h, w_ref[:, :O_NA_Q]) for h in hs]
    for rows, h, uv in zip(tiles, hs, uvs):
        qkv = _dot(h, w_ref[:, O_NA_Q:])
        q_ref[rows, :] = (qkv[:, :NA_WIDTH] * NA_HEAD_DIM ** -0.5).astype(BF16)
        k_ref[rows, :] = qkv[:, NA_WIDTH:2 * NA_WIDTH].astype(BF16)
        v_ref[rows, :] = qkv[:, 2 * NA_WIDTH:].astype(BF16)
        uv = jax.nn.gelu(uv)
        u = uv[:, :A_WIDTH]
        v = _rms(uv[:, A_WIDTH:], snw_ref[...]).astype(BF16)
        for ci in range(uv.shape[0] // A_CHUNK):
            rs = slice(ci * A_CHUNK, (ci + 1) * A_CHUNK)
            out_rows = slice(rows.start + rs.start, rows.start + rs.stop)
            for g in range(A_GROUPS):
                cs = slice(g * LANES, (g + 1) * LANES)
                gate = _dot(sguw_ref[g], v[rs, cs]) + sgub_ref[g]
                a_ref[out_rows, cs] = (u[rs, cs] * gate).astype(BF16)


def _ab_in(x2, mod, tiles_per_batch, tm, nw, w_bf, snw, sguw_bf, sgub_exp):
    m = x2.shape[0]
    n_in = w_bf.shape[1]
    row = lambda w: pl.BlockSpec((tm, w), lambda i: (i, 0))
    out = jax.ShapeDtypeStruct((m, A_WIDTH), BF16)
    return pl.pallas_call(
        functools.partial(_ab_in_kernel, tm=tm),
        out_shape=(out, out, out, out),
        grid=(m // tm,),
        in_specs=[row(D_MODEL), _mod_spec(tiles_per_batch), _resident((1, D_MODEL)),
                  _resident((D_MODEL, n_in)), _resident((1, A_WIDTH)),
                  _resident((A_GROUPS, A_CHUNK, A_CHUNK)), _resident((A_GROUPS, A_CHUNK, LANES))],
        out_specs=(row(A_WIDTH),) * 4,
        compiler_params=_cparams("parallel"),
        name="ab_in",
    )(x2, mod, nw, w_bf, snw, sguw_bf, sgub_exp)


def _na_bias_table(rel_bias):
    w, nrow = GRID_W, 2 * NA_WIN_ROWS - 1
    col = np.arange(w)
    col_start = np.clip(col - NA_WIN_COLS // 2, 0, w - NA_WIN_COLS)
    col_mask = (col[None, :] >= col_start[:, None]) & (col[None, :] < col_start[:, None] + NA_WIN_COLS)
    lo = w - NA_WIN_COLS
    r_ext = jnp.pad(rel_bias, ((0, 0), (0, 0), (lo, 2 * w - lo - (2 * NA_WIN_COLS - 1))))
    flat = jnp.tile(r_ext, (1, 1, w))[:, :, :w * (2 * w - 1)]
    toe = flat.reshape(NA_HEADS, nrow, w, 2 * w - 1)[:, :, :, w - 1:]
    toe = jnp.where(col_mask, toe, NEG_INF).reshape(NA_HEADS // 2, 2, nrow, w, w)
    strips = [jnp.concatenate([toe[:, :, j - o + NA_WIN_ROWS - 1] for j in range(NA_WIN_ROWS)], axis=-1)
              for o in range(NA_WIN_ROWS)]
    return jnp.stack([s.reshape(NA_HEADS // 2, 2 * w, NA_WIN_ROWS * w) for s in strips], axis=1)


def _stack_heads(qb):
    first = lax.broadcasted_iota(jnp.int32, (1, LANES), 1) < NA_HEAD_DIM
    zero = jnp.zeros_like(qb)
    return jnp.concatenate([jnp.where(first, qb, zero), jnp.where(first, zero, qb)], axis=0)


def _unstack_heads(r):
    n = r.shape[0] // 2
    first = lax.broadcasted_iota(jnp.int32, (1, LANES), 1) < NA_HEAD_DIM
    return jnp.where(first, r[:n], r[n:])


def _na_kernel(q_ref, k_ref, v_ref, kc_ref, vc_ref, tab_ref, o_ref, kt_scr, kct_scr, *, rows):
    t = rows * GRID_W
    kt_scr[0] = k_ref[0].T
    kt_scr[1, :, :t - GRID_W] = k_ref[0, GRID_W:, :].T
    kct_scr[...] = kc_ref[0].T
    kct = kct_scr[...]
    vc = vc_ref[0]
    win = NA_WIN_ROWS * GRID_W

    def window(r):
        return min(max(r - NA_WIN_ROWS // 2, 0), rows - NA_WIN_ROWS)

    def scores(r):
        start = window(r)
        odd = start % 2
        kt0 = (start - odd) * GRID_W
        qs = _stack_heads(q_ref[0, r * GRID_W:(r + 1) * GRID_W, :])
        return _dot(qs, kt_scr[odd, :, kt0:kt0 + win]) + tab_ref[0, r - start], _dot(qs, kct)

    def softmax(s):
        s_nb, s_cx = s
        m = jnp.maximum(jnp.max(s_nb, axis=-1, keepdims=True), jnp.max(s_cx, axis=-1, keepdims=True))
        return jnp.exp(s_nb - m).astype(BF16), jnp.exp(s_cx - m).astype(BF16)

    ones_nb = jnp.ones((win, LANES), BF16)
    vc_aug = jnp.concatenate([vc, jnp.ones((vc.shape[0], LANES), BF16)], axis=1)

    def values(r, p):
        e_nb, e_cx = p
        k0 = window(r) * GRID_W
        vb_aug = jnp.concatenate([v_ref[0, k0:k0 + win, :], ones_nb], axis=1)
        acc = _dot(e_nb, vb_aug) + _dot(e_cx, vc_aug)
        o_ref[0, r * GRID_W:(r + 1) * GRID_W, :] = _unstack_heads(acc[:, :LANES] / acc[:, LANES:]).astype(BF16)

    lag = NA_STAGE_LAG
    s_vals, p_vals = {}, {}
    for r in range(rows + 2 * lag):
        if 0 <= r - 2 * lag < rows:
            values(r - 2 * lag, p_vals.pop(r - 2 * lag))
        if 0 <= r - lag < rows:
            p_vals[r - lag] = softmax(s_vals.pop(r - lag))
        if r < rows:
            s_vals[r] = scores(r)


def _na(q, k, v, kc, vc, table):
    b, t, _ = q.shape
    rows = t // GRID_W
    lat = pl.BlockSpec((1, t, LANES), lambda j, i: (i, 0, j))
    ctx = pl.BlockSpec((1, CTX_LEN, LANES), lambda j, i: (i, 0, j))
    return pl.pallas_call(
        functools.partial(_na_kernel, rows=rows),
        out_shape=jax.ShapeDtypeStruct((b, t, NA_WIDTH), BF16),
        grid=(NA_HEADS // 2, b),
        in_specs=[lat, lat, lat, ctx, ctx,
                  pl.BlockSpec((1, NA_WIN_ROWS, 2 * GRID_W, NA_WIN_ROWS * GRID_W), lambda j, i: (j, 0, 0, 0))],
        out_specs=lat,
        scratch_shapes=[pltpu.VMEM((2, LANES, t), BF16), pltpu.VMEM((LANES, CTX_LEN), BF16)],
        compiler_params=_cparams("parallel", "parallel"),
        name="na_attn",
    )(q, k, v, kc, vc, table)


def _ctx_attn_kernel(q_ref, k_ref, v_ref, o_ref):
    for j in range(NA_HEADS // 2):
        cols = slice(j * LANES, (j + 1) * LANES)
        s = _dot_nt(_stack_heads(q_ref[0, :, cols]), k_ref[0, :, cols])
        e = jnp.exp(s - jnp.max(s, axis=-1, keepdims=True))
        acc = _dot(e.astype(BF16), v_ref[0, :, cols])
        o_ref[0, :, cols] = _unstack_heads(acc / jnp.sum(e, axis=-1, keepdims=True)).astype(BF16)


def _ctx_attn(q, k, v):
    b = q.shape[0]
    spec = pl.BlockSpec((1, CTX_LEN, NA_WIDTH), lambda i: (i, 0, 0))
    return pl.pallas_call(
        _ctx_attn_kernel,
        out_shape=jax.ShapeDtypeStruct((b, CTX_LEN, NA_WIDTH), BF16),
        grid=(b,),
        in_specs=[spec, spec, spec],
        out_specs=spec,
        compiler_params=_cparams("parallel"),
        name="ctx_attn",
    )(q, k, v)


def _out_ffn_kernel(a_ref, b_ref, x_ref, mod_ref, wo_ref, nw_ref, wg_ref, wu_ref, wd_ref, fnw_ref, o_ref,
                    *, final):
    half = wo_ref.shape[0] // 2
    tiles = _subtiles(x_ref.shape[0])
    ys = [_dot(a_ref[rows, :], wo_ref[:half, :]) + _dot(b_ref[rows, :], wo_ref[half:, :]) for rows in tiles]
    x1s = [x_ref[rows, :] + mod_ref[0, 2:3, :] * y for rows, y in zip(tiles, ys)]
    hs = [_rms_mod(x1, nw_ref[...], mod_ref[0, 3:4, :], mod_ref[0, 4:5, :]).astype(BF16) for x1 in x1s]
    accs = [None] * len(tiles)
    for cs in _hidden_pieces(wg_ref.shape[1]):
        gs = [_dot(h, wg_ref[:, cs]) for h in hs]
        acts = [(g * jax.nn.sigmoid(g) * _dot(h, wu_ref[:, cs])).astype(BF16) for g, h in zip(gs, hs)]
        parts = [_dot(act, wd_ref[cs, :]) for act in acts]
        accs = [part if acc is None else acc + part for acc, part in zip(accs, parts)]
    for rows, x1, acc in zip(tiles, x1s, accs):
        x2 = x1 + mod_ref[0, 5:6, :] * acc
        o_ref[rows, :] = _rms(x2, fnw_ref[...]) if final else x2


def _out_ffn(a, b, x2, mod, tiles_per_batch, tm, wo_bf, nw, layer, wg_bf, wu_bf, wd_bf, fnw, final):
    m = x2.shape[0]
    row = lambda w: pl.BlockSpec((tm, w), lambda i: (i, 0))
    layer_w = lambda w: pl.BlockSpec((None,) + w.shape[1:], lambda i: (layer, 0, 0), pipeline_mode=pl.Buffered(1))
    return pl.pallas_call(
        functools.partial(_out_ffn_kernel, final=final),
        out_shape=jax.ShapeDtypeStruct((m, D_MODEL), F32),
        grid=(m // tm,),
        in_specs=[row(a.shape[1]), row(b.shape[1]), row(D_MODEL), _mod_spec(tiles_per_batch),
                  _resident(wo_bf.shape), _resident((1, D_MODEL)), layer_w(wg_bf),
                  layer_w(wu_bf), layer_w(wd_bf), _resident((1, D_MODEL))],
        out_specs=row(D_MODEL),
        compiler_params=_cparams("parallel"),
        name="out_ffn_final" if final else "out_ffn",
    )(a, b, x2, mod, wo_bf, nw, wg_bf, wu_bf, wd_bf, fnw)


def _rope_tables(t):
    half = GLA_DK // 4
    inv_freq = ROPE_BASE ** (-np.arange(half, dtype=np.float64) / half)
    pos = np.arange(t)
    ang_r = (pos // GRID_W)[:, None] * inv_freq[None, :]
    ang_c = (pos % GRID_W)[:, None] * inv_freq[None, :]
    cos = np.concatenate([np.cos(ang_r)] * 2 + [np.cos(ang_c)] * 2, axis=1)
    sin = np.concatenate([-np.sin(ang_r), np.sin(ang_r), -np.sin(ang_c), np.sin(ang_c)], axis=1)
    return (jnp.asarray(np.tile(cos, (1, 2)), F32), jnp.asarray(np.tile(sin, (1, 2)), F32))


def _log_sigmoid(z):
    return jnp.minimum(z, 0.0) - jnp.log(1.0 + jnp.exp(-jnp.abs(z)))


def _decay_logs(h, wa_ref, wdec_ref, bdec_ref):
    a = _dot(h, wa_ref[...]).astype(BF16)
    return _log_sigmoid(_dot(a, wdec_ref[...]) + bdec_ref[...]) * (1.0 / GLA_GATE_TEMP)


def _cd_in_kernel(x_ref, mod_ref, nw_ref, w_ref, wa_ref, wdec_ref, bdec_ref, cos_ref, sin_ref,
                  f_ref, qk_ref, v_ref, g_ref, la_ref):
    h = _rms_mod(x_ref[...], nw_ref[...], mod_ref[0, 0:1, :], mod_ref[0, 1:2, :]).astype(BF16)
    la_ref[...] = _decay_logs(h, wa_ref, wdec_ref, bdec_ref)
    f = _dot(h, w_ref[:, :O_Q])
    for a in range(f.shape[0] // GRID_W):
        f_ref[a * FFT_PITCH:a * FFT_PITCH + GRID_W, :] = f[a * GRID_W:(a + 1) * GRID_W]
        f_ref[a * FFT_PITCH + GRID_W:(a + 1) * FFT_PITCH, :] = jnp.zeros((FFT_PITCH - GRID_W, f.shape[1]), F32)
    qk = _dot(h, w_ref[:, O_Q:O_V])
    w = qk.shape[1]
    cos = jnp.concatenate([cos_ref[...]] * (w // LANES), axis=1)
    sin = jnp.concatenate([sin_ref[...]] * (w // LANES), axis=1)
    q16 = GLA_DK // 4
    first = (lax.broadcasted_iota(jnp.int32, (1, w), 1) % (2 * q16)) < q16
    partner = jnp.where(first, pltpu.roll(qk, w - q16, 1), pltpu.roll(qk, q16, 1))
    qk = qk * cos + partner * sin
    is_q = lax.broadcasted_iota(jnp.int32, (1, w), 1) < GLA_KEY_WIDTH
    qk_ref[...] = jnp.where(is_q, qk * GLA_DK ** -0.5, qk)
    v_ref[...] = _dot(h, w_ref[:, O_V:O_G]).astype(BF16)
    g_ref[...] = _dot(h, w_ref[:, O_G:O_A])


def _cd_in(x2, mod, tiles_per_batch, tm, nw, w_bf, wa_bf, wdec_bf, bdec, cos, sin):
    m = x2.shape[0]
    row = lambda w: pl.BlockSpec((tm, w), lambda i: (i, 0))
    tab = pl.BlockSpec((tm, LANES), lambda i: (i % tiles_per_batch, 0))
    w512 = 2 * GLA_KEY_WIDTH
    pad = lambda n: n // GRID_W * FFT_PITCH
    outs = (jax.ShapeDtypeStruct((pad(m), FN_WIDTH), F32), jax.ShapeDtypeStruct((m, w512), F32),
            jax.ShapeDtypeStruct((m, GLA_VAL_WIDTH), BF16), jax.ShapeDtypeStruct((m, GLA_VAL_WIDTH), F32),
            jax.ShapeDtypeStruct((m, w512), F32))
    return pl.pallas_call(
        _cd_in_kernel,
        out_shape=outs,
        grid=(m // tm,),
        in_specs=[row(D_MODEL), _mod_spec(tiles_per_batch), _resident((1, D_MODEL)),
                  _resident(w_bf.shape), _resident(wa_bf.shape), _resident(wdec_bf.shape),
                  _resident((1, w512)), tab, tab],
        out_specs=(pl.BlockSpec((pad(tm), FN_WIDTH), lambda i: (i, 0)),
                   row(w512), row(GLA_VAL_WIDTH), row(GLA_VAL_WIDTH), row(w512)),
        compiler_params=_cparams("parallel"),
        name="cd_in",
    )(x2, mod, nw, w_bf, wa_bf, wdec_bf, bdec, cos, sin)


def _cd_ctx_kernel(x_ref, mod_ref, nw_ref, w_ref, wa_ref, wdec_ref, bdec_ref, k_ref, v_ref, la_ref):
    h = _rms_mod(x_ref[...], nw_ref[...], mod_ref[0, 0:1, :], mod_ref[0, 1:2, :]).astype(BF16)
    la_ref[...] = _decay_logs(h, wa_ref, wdec_ref, bdec_ref)
    k_ref[...] = _dot(h, w_ref[:, O_Q + GLA_KEY_WIDTH:O_V])
    v_ref[...] = _dot(h, w_ref[:, O_V:O_G]).astype(BF16)


def _cd_ctx(x2, mod, tm, nw, w_bf, wa_bf, wdec_bf, bdec):
    m = x2.shape[0]
    row = lambda w: pl.BlockSpec((tm, w), lambda i: (i, 0))
    w512 = 2 * GLA_KEY_WIDTH
    outs = (jax.ShapeDtypeStruct((m, GLA_KEY_WIDTH), F32), jax.ShapeDtypeStruct((m, GLA_VAL_WIDTH), BF16),
            jax.ShapeDtypeStruct((m, w512), F32))
    return pl.pallas_call(
        _cd_ctx_kernel,
        out_shape=outs,
        grid=(m // tm,),
        in_specs=[row(D_MODEL), _mod_spec(None), _resident((1, D_MODEL)),
                  _resident(w_bf.shape), _resident(wa_bf.shape), _resident(wdec_bf.shape), _resident((1, w512))],
        out_specs=(row(GLA_KEY_WIDTH), row(GLA_VAL_WIDTH), row(w512)),
        compiler_params=_cparams("parallel"),
        name="cd_ctx",
    )(x2, mod, nw, w_bf, wa_bf, wdec_bf, bdec)


def _fft_tables(t):
    n1 = GRID_W
    assert t == n1 * n1
    a = np.arange(n1)
    ang1 = 2 * np.pi * np.outer(a, a) / n1
    cs = np.concatenate([np.cos(ang1), -np.sin(ang1)], axis=0)
    kap = a[:, None, None] + n1 * a[None, :, None]
    ang2 = 2 * np.pi * kap * a[None, None, :] / t
    gr, gi = np.cos(ang2), -np.sin(ang2)
    g = np.concatenate([np.concatenate([gr, -gi], axis=2), np.concatenate([gi, gr], axis=2)], axis=1)
    c = np.arange(FN_GROUP_CH)
    ang3 = 2 * np.pi * np.outer(c, c) / FN_GROUP_CH
    norm = 1.0 / np.sqrt(t * FN_GROUP_CH)
    return (jnp.asarray(cs, F32), jnp.asarray(g, F32),
            jnp.asarray(np.cos(ang3) * norm, F32), jnp.asarray(np.sin(ang3) * norm, F32))


def _fft_kernel(xa_ref, xb_ref, cs_ref, g_ref, cc_ref, sc_ref, o_ref, p_scr, zr_scr, zi_scr):
    n1 = GRID_W
    x_refs = (xa_ref, xb_ref)
    groups = range(len(x_refs))
    lanes = lambda gi: slice(gi * FN_GROUP_CH, (gi + 1) * FN_GROUP_CH)
    cs = cs_ref[...].astype(BF16)

    def stage1(b, carry):
        xb = jnp.concatenate([r[0, pl.ds(b, n1, stride=FFT_PITCH), :] for r in x_refs], axis=1)
        p = _dot(cs, xb.astype(BF16))
        rows = pl.ds(pl.multiple_of(b * FFT_PITCH2, 8), 2 * n1)
        for gi in groups:
            p_scr[gi, rows, :] = p[:, lanes(gi)]
        return carry

    lax.fori_loop(0, n1, stage1, 0, unroll=FFT_UNROLL)

    def stage2(k1, carry):
        pr = jnp.concatenate([p_scr[gi, pl.ds(k1, n1, stride=FFT_PITCH2), :] for gi in groups], axis=1)
        pi = jnp.concatenate([p_scr[gi, pl.ds(n1 + k1, n1, stride=FFT_PITCH2), :] for gi in groups], axis=1)
        z = _dot(g_ref[k1].astype(BF16), jnp.concatenate([pr, pi], axis=0).astype(BF16))
        rows = pl.ds(pl.multiple_of(k1 * FFT_PITCH, 8), n1)
        for gi in groups:
            zr_scr[gi, rows, :] = z[:n1, lanes(gi)]
            zi_scr[gi, rows, :] = z[n1:, lanes(gi)]
        return carry

    lax.fori_loop(0, n1, stage2, 0, unroll=FFT_UNROLL)

    ccsc = jnp.concatenate([cc_ref[...], sc_ref[...]], axis=0).astype(BF16)

    def stage3(j, carry):
        out_rows = pl.ds(pl.multiple_of(j * FFT_UNROLL * n1, FFT_UNROLL * n1), FFT_UNROLL * n1)
        for gi in groups:
            parts = []
            for u in range(FFT_UNROLL):
                k2 = j * FFT_UNROLL + u
                parts.append(jnp.concatenate([zr_scr[gi, pl.ds(k2, n1, stride=FFT_PITCH), :],
                                              zi_scr[gi, pl.ds(k2, n1, stride=FFT_PITCH), :]], axis=1))
            out = _dot(jnp.concatenate(parts, axis=0).astype(BF16), ccsc)
            o_ref[0, out_rows, lanes(gi)] = out.astype(BF16)
        return carry

    lax.fori_loop(0, n1 // FFT_UNROLL, stage3, 0)


def _fourier(f_padded, tables):
    b, tp, w = f_padded.shape
    t = tp // FFT_PITCH * GRID_W
    cw = FN_GROUP_CH
    cs, g, cc, sc = tables
    group = lambda k: pl.BlockSpec((1, tp, cw), lambda i, j: (i, 0, 2 * j + k))
    return pl.pallas_call(
        _fft_kernel,
        out_shape=jax.ShapeDtypeStruct((b, t, w), BF16),
        grid=(b, w // (2 * cw)),
        in_specs=[group(0), group(1),
                  _resident(cs.shape), _resident(g.shape), _resident(cc.shape), _resident(sc.shape)],
        out_specs=pl.BlockSpec((1, t, 2 * cw), lambda i, j: (i, 0, j)),
        scratch_shapes=[pltpu.VMEM((2, GRID_W * FFT_PITCH2, cw), F32), pltpu.VMEM((2, tp, cw), F32),
                        pltpu.VMEM((2, tp, cw), F32)],
        compiler_params=_cparams("parallel", "parallel"),
        name="fourier_mix",
    )(f_padded, f_padded, cs, g, cc, sc)


def _block_diag(x, block):
    head = lax.broadcasted_iota(jnp.int32, (1, x.shape[1]), 1) // block
    return jnp.concatenate([jnp.where(head == h, x, jnp.zeros_like(x)) for h in range(GLA_HEADS)], axis=0)


def _cum_rows(x, reverse):
    n, w = x.shape
    row = lax.broadcasted_iota(jnp.int32, (n, 1), 0)
    s = 1
    while s < n:
        if s < 8:
            if reverse:
                shifted = jnp.where(row < n - s, pltpu.roll(x, n - s, 0), 0.0)
            else:
                shifted = jnp.where(row >= s, pltpu.roll(x, s, 0), 0.0)
        else:
            pad = jnp.zeros((s, w), F32)
            shifted = jnp.concatenate([x[s:], pad] if reverse else [pad, x[:n - s]], axis=0)
        x = x + shifted
        s *= 2
    return x


def _gla_prep(q, k, la, reverse, want_out):
    c = GLA_CHUNK
    b = _cum_rows(la, reverse)
    b_last = b[0:1, :] if reverse else b[c - 1:c, :]
    kd = (k * jnp.exp(b_last - b)).astype(BF16)
    decay = jnp.exp(b_last)
    if not want_out:
        return kd, decay
    qe = (q * jnp.exp(b)).astype(BF16)
    ke = _block_diag(k * jnp.exp(-b), GLA_DK).astype(BF16)
    l = lax.broadcasted_iota(jnp.int32, (c, GLA_HEADS * c), 0)
    m = lax.broadcasted_iota(jnp.int32, (c, GLA_HEADS * c), 1) % c
    att = jnp.where((l <= m) if reverse else (l >= m), _dot_nt(qe, ke), 0.0).astype(BF16)
    return kd, decay, qe, att


def _gla_state_step(kd, decay, v, state):
    v_rows = jnp.concatenate([v[:, h * GLA_DV:(h + 1) * GLA_DV] for h in range(GLA_HEADS)], axis=0)
    return state * decay + _dot_tn(v_rows, _block_diag(kd, GLA_DK))


def _gla_apply(prep, v, state):
    c = GLA_CHUNK
    kd, decay, qe, att = prep
    inter = _dot_nt(_block_diag(qe, GLA_DK), state.astype(BF16))
    out = _dot(att, _block_diag(v, GLA_DV)) + jnp.concatenate(
        [inter[h * c:(h + 1) * c] for h in range(GLA_HEADS)], axis=1)
    return out, _gla_state_step(kd, decay, v, state)


def _gla_kernel(qkf_ref, vf_ref, laf_ref, gf_ref, qkb_ref, vb_ref, lab_ref, gb_ref,
                kc_ref, vc_ref, lac_ref, hw_ref, o_ref, sf_scr, sb_scr, acc_scr, *, n_tiles, tile):
    c = GLA_CHUNK
    i = pl.program_id(1)
    kw = GLA_KEY_WIDTH
    n_chunks = tile // c

    @pl.when(i == 0)
    def _():
        n_ctx = kc_ref.shape[1] // c
        sf = jnp.zeros((GLA_DV, kw), F32)
        sb = jnp.zeros((GLA_DV, kw), F32)
        for n in range(n_ctx):
            rf = slice(n * c, (n + 1) * c)
            rb = slice((n_ctx - 1 - n) * c, (n_ctx - n) * c)
            sf = _gla_state_step(*_gla_prep(None, kc_ref[0, rf, :], lac_ref[0, rf, :kw], False, False),
                                 vc_ref[0, rf, :], sf)
            sb = _gla_state_step(*_gla_prep(None, kc_ref[0, rb, :], lac_ref[0, rb, kw:], True, False),
                                 vc_ref[0, rb, :], sb)
        sf_scr[...] = sf
        sb_scr[...] = sb

    hw = hw_ref[...]

    def finish(o, g):
        parts = []
        for h in range(GLA_HEADS):
            oh = o[:, h * GLA_DV:(h + 1) * GLA_DV]
            parts.append(oh * lax.rsqrt(jnp.mean(oh * oh, axis=-1, keepdims=True) + RMS_EPS))
        return (jnp.concatenate(parts, axis=1) * hw * (g * jax.nn.sigmoid(g))).astype(BF16)

    def sweep(second_pass):
        def emit(o, g_ref, rloc, row0):
            rows = pl.ds(pl.multiple_of(row0, c), c)
            if second_pass:
                o_ref[0, rows, :] = finish(o + acc_scr[rows, :], g_ref[0, rloc, :])
            else:
                acc_scr[rows, :] = o

        def rows_of(n):
            nb = n_chunks - 1 - n
            return slice(n * c, (n + 1) * c), slice(nb * c, (nb + 1) * c)

        def prep(n):
            rf, rb = rows_of(n)
            qkf, qkb = qkf_ref[0, rf, :], qkb_ref[0, rb, :]
            return (_gla_prep(qkf[:, :kw], qkf[:, kw:], laf_ref[0, rf, :], False, True),
                    _gla_prep(qkb[:, :kw], qkb[:, kw:], lab_ref[0, rb, :], True, True))

        sf, sb = sf_scr[...], sb_scr[...]
        ready = {n: prep(n) for n in range(min(GLA_PREP_AHEAD, n_chunks))}
        for n in range(n_chunks):
            rf, rb = rows_of(n)
            pf, pb = ready.pop(n)
            of, sf = _gla_apply(pf, vf_ref[0, rf, :], sf)
            emit(of, gf_ref, rf, i * tile + rf.start)
            ob, sb = _gla_apply(pb, vb_ref[0, rb, :], sb)
            emit(ob, gb_ref, rb, (n_tiles - 1 - i) * tile + rb.start)
            if n + GLA_PREP_AHEAD < n_chunks:
                ready[n + GLA_PREP_AHEAD] = prep(n + GLA_PREP_AHEAD)
        sf_scr[...] = sf
        sb_scr[...] = sb

    @pl.when(i < n_tiles // 2)
    def _():
        sweep(False)

    @pl.when(i >= n_tiles // 2)
    def _():
        sweep(True)


def _gla(qk, v, la, g, k_c, v_c, la_c, head_w):
    b, t, _ = qk.shape
    tile = GLA_TILE
    n_tiles = t // tile
    kw, vw = GLA_KEY_WIDTH, GLA_VAL_WIDTH
    fwd = lambda w, col: pl.BlockSpec((1, tile, w), lambda bi, i: (bi, i, col))
    bwd = lambda w, col: pl.BlockSpec((1, tile, w), lambda bi, i: (bi, n_tiles - 1 - i, col))
    ctx = lambda w: pl.BlockSpec((1, CTX_LEN, w), lambda bi, i: (bi, 0, 0))
    half = n_tiles // 2
    gate_fwd = pl.BlockSpec((1, tile, vw), lambda bi, i: (bi, jnp.maximum(i, half), 0))
    gate_bwd = pl.BlockSpec((1, tile, vw), lambda bi, i: (bi, n_tiles - 1 - jnp.maximum(i, half), 0))
    return pl.pallas_call(
        functools.partial(_gla_kernel, n_tiles=n_tiles, tile=tile),
        out_shape=jax.ShapeDtypeStruct((b, t, vw), BF16),
        grid=(b, n_tiles),
        in_specs=[fwd(2 * kw, 0), fwd(vw, 0), fwd(kw, 0), gate_fwd,
                  bwd(2 * kw, 0), bwd(vw, 0), bwd(kw, 1), gate_bwd,
                  ctx(kw), ctx(vw), ctx(2 * kw),
                  pl.BlockSpec((1, vw), lambda bi, i: (0, 0))],
        out_specs=pl.BlockSpec((1, t, vw), lambda bi, i: (bi, 0, 0)),
        scratch_shapes=[pltpu.VMEM((GLA_DV, kw), F32), pltpu.VMEM((GLA_DV, kw), F32), pltpu.VMEM((t, vw), F32)],
        compiler_params=_cparams("parallel", "arbitrary"),
        name="gla",
    )(qk, v, la, g, qk, v, la, g, k_c, v_c, la_c, head_w)


def kernel(x, c, ctx, c_ctx, ada_w, ada_b, norm_mix_w, norm_ffn_w, ffn_w_gate, ffn_w_up, ffn_w_down,
           ab_w_in, ab_w_out, ab_sgu_norm_w, ab_sgu_w, ab_sgu_b, ab_rel_bias,
           cd_w_in, cd_w_out, cd_decay_w_fwd, cd_decay_b_fwd, cd_decay_w_bwd, cd_decay_b_bwd, cd_head_norm_w,
           final_norm_w):
    bsz, t, d = x.shape
    n_ctx = ctx.shape[1]
    tpb = t // ROW_TILE
    x2 = x.reshape(bsz * t, d)
    ctx2 = ctx.reshape(bsz * n_ctx, d)
    row = lambda w: w.reshape(1, -1)

    cc = jnp.concatenate([c, c_ctx[None, :], jnp.zeros((ADA_ROWS - bsz - 1, d), F32)], axis=0)
    mod = _ada(cc, ada_w, ada_b)
    mod_x = [mod[i, :bsz].reshape(bsz, MOD_ROWS, d) for i in range(2)]
    mod_c = [mod[i, bsz:bsz + 1].reshape(1, MOD_ROWS, d) for i in range(2)]
    ffn = (ffn_w_gate.astype(BF16), ffn_w_up.astype(BF16), ffn_w_down.astype(BF16))

    w_in = ab_w_in[0].astype(BF16)
    sgu = (row(ab_sgu_norm_w[0]), ab_sgu_w[0].astype(BF16),
           jnp.broadcast_to(ab_sgu_b[0][:, :, None], (A_GROUPS, A_CHUNK, LANES)))
    a_l, q_l, k_l, v_l = _ab_in(x2, mod_x[0], tpb, ROW_TILE, row(norm_mix_w[0]), w_in, *sgu)
    a_c, q_c, k_c, v_c = _ab_in(ctx2, mod_c[0], None, ROW_TILE, row(norm_mix_w[0]), w_in, *sgu)
    seq = lambda z, n: z.reshape(bsz, n, z.shape[-1])
    b_l = _na(seq(q_l, t), seq(k_l, t), seq(v_l, t), seq(k_c, n_ctx), seq(v_c, n_ctx), _na_bias_table(ab_rel_bias[0]))
    b_c = _ctx_attn(seq(q_c, n_ctx), seq(k_c, n_ctx), seq(v_c, n_ctx))
    wo = ab_w_out[0].astype(BF16)
    x2 = _out_ffn(a_l, b_l.reshape(bsz * t, -1), x2, mod_x[0], t // FFN_ROW_TILE, FFN_ROW_TILE, wo,
                  row(norm_ffn_w[0]), 0, *ffn, row(final_norm_w), False)
    ctx2 = _out_ffn(a_c, b_c.reshape(bsz * n_ctx, -1), ctx2, mod_c[0], None, ROW_TILE, wo, row(norm_ffn_w[0]),
                    0, *ffn, row(final_norm_w), False)

    w_in = cd_w_in[0]
    w_main = w_in[:, :O_A].astype(BF16)
    w_a = jnp.pad(w_in[:, O_A:], ((0, 0), (0, LANES - 2 * GLA_LOW_RANK))).astype(BF16)
    w_dec = jnp.zeros((LANES, 2 * GLA_KEY_WIDTH), F32)
    w_dec = w_dec.at[:GLA_LOW_RANK, :GLA_KEY_WIDTH].set(cd_decay_w_fwd[0])
    w_dec = w_dec.at[GLA_LOW_RANK:2 * GLA_LOW_RANK, GLA_KEY_WIDTH:].set(cd_decay_w_bwd[0]).astype(BF16)
    b_dec = jnp.concatenate([cd_decay_b_fwd[0], cd_decay_b_bwd[0]]).reshape(1, -1)
    cos, sin = _rope_tables(t)
    f_l, qk_l, v_l, g_l, la_l = _cd_in(x2, mod_x[1], tpb, ROW_TILE, row(norm_mix_w[1]), w_main, w_a, w_dec, b_dec,
                                      cos, sin)
    k_c, v_c, la_c = _cd_ctx(ctx2, mod_c[1], ROW_TILE, row(norm_mix_w[1]), w_main, w_a, w_dec, b_dec)
    fm = _fourier(f_l.reshape(bsz, -1, FN_WIDTH), _fft_tables(t))
    head_w = jnp.tile(cd_head_norm_w[0], GLA_HEADS).reshape(1, -1)
    go = _gla(seq(qk_l, t), seq(v_l, t), seq(la_l, t), seq(g_l, t),
              seq(k_c, n_ctx), seq(v_c, n_ctx), seq(la_c, n_ctx), head_w)
    out = _out_ffn(fm.reshape(bsz * t, -1), go.reshape(bsz * t, -1), x2, mod_x[1], t // FFN_ROW_TILE, FFN_ROW_TILE,
                   cd_w_out[0].astype(BF16), row(norm_ffn_w[1]), 1, *ffn, row(final_norm_w), True)
    return out.reshape(bsz, t, d)
```
